```python
import jax, jax.numpy as jnp
from jax import lax
import numpy as np

D_MODEL = 1024
BATCH = 16
SEQ = 2048
DEPTH = 1

MEM_LEN = 256
HG_HEADS = 4
HG_KEY_DIM = 128
HG_VAL_DIM = 128
HG_CHUNK = 64
HG_QK = HG_HEADS * HG_KEY_DIM
HG_V = HG_HEADS * HG_VAL_DIM
SB_HEADS = 8
SB_HEAD_DIM = 64
SB_W = SB_HEADS * SB_HEAD_DIM
SB_BLOCK = 128
XA_HEADS = 4
XA_HEAD_DIM = D_MODEL // XA_HEADS
XA_W = XA_HEADS * XA_HEAD_DIM
N_GROUPS = 4
EXPERTS_PER_GROUP = 4
N_EXPERTS = N_GROUPS * EXPERTS_PER_GROUP
TOP_K_INNER = 2
EXPERT_FF = 512
IN_SPLITS = (HG_QK, HG_QK, HG_V, HG_V, SB_W, SB_W, SB_W, D_MODEL, D_MODEL)
N_IN = HG_QK * 2 + HG_V * 2 + SB_W * 3 + D_MODEL * 2
LN_EPS = 1e-5
RMS_EPS = 1e-6
DN_ALPHA = (2 * DEPTH) ** 0.25
DN_BETA = (8 * DEPTH) ** -0.25

kernel_name = "hybrid_hgrn2_stickbreaking_hmoe_deepnorm"


def layer_norm(x, g, b):
    xf = x.astype(jnp.float32)
    mu = jnp.mean(xf, axis=-1, keepdims=True)
    xc = xf - mu
    var = jnp.mean(xc * xc, axis=-1, keepdims=True)
    y = xc * lax.rsqrt(var + LN_EPS) * g.astype(jnp.float32) + b.astype(jnp.float32)
    return y.astype(x.dtype)


def hgrn2_branch(q, f_logit, i, g, lb, norm_g):
    B, S, _ = q.shape
    nc = S // HG_CHUNK
    f32 = jnp.float32

    def heads(t, d):
        t = t.astype(f32).reshape(B, nc, HG_CHUNK, HG_HEADS, d)
        return t.transpose(1, 0, 3, 2, 4)

    lbf = lb.astype(f32)
    forget = lbf + (1.0 - lbf) * jax.nn.sigmoid(f_logit.astype(f32))
    log_f = heads(jnp.log(forget), HG_KEY_DIM)
    kh = heads(1.0 - forget, HG_KEY_DIM)
    qh = heads(q, HG_KEY_DIM)
    vh = heads(i, HG_VAL_DIM)
    causal = jnp.tril(jnp.ones((HG_CHUNK, HG_CHUNK), dtype=bool))

    def step(state, xs):
        qc, kc, vc, lfc = xs
        b = jnp.cumsum(lfc, axis=-2)
        rel = b[:, :, :, None, :] - b[:, :, None, :, :]
        decay = jnp.exp(jnp.where(causal[:, :, None], rel, -jnp.inf))
        scores = jnp.einsum('bhtd,bhsd,bhtsd->bhts', qc, kc, decay)
        out = (jnp.einsum('bhts,bhsv->bhtv', scores, vc)
               + jnp.einsum('bhtd,bhdv->bhtv', qc * jnp.exp(b), state))
        b_last = b[:, :, -1:, :]
        new_state = (jnp.exp(b_last[:, :, 0, :, None]) * state
                     + jnp.einsum('bhsd,bhsv->bhdv', kc * jnp.exp(b_last - b), vc))
        return new_state, out

    s0 = jnp.zeros((B, HG_HEADS, HG_KEY_DIM, HG_VAL_DIM), f32)
    _, o = lax.scan(step, s0, (qh, kh, vh, log_f))
    o = o.transpose(1, 0, 3, 2, 4).reshape(B, S, HG_HEADS, HG_VAL_DIM)
    o = o * lax.rsqrt(jnp.mean(o * o, axis=-1, keepdims=True) + RMS_EPS) * norm_g.astype(f32)
    gate = jax.nn.silu(g.astype(f32)).reshape(B, S, HG_HEADS, HG_VAL_DIM)
    return (o * gate).reshape(B, S, HG_V).astype(q.dtype)


def stick_breaking_branch(q, k, v):
    B, S, _ = q.shape

    def heads(t):
        return t.reshape(B, S, SB_HEADS, SB_HEAD_DIM).transpose(0, 2, 1, 3)

    qh, kh, vh = heads(q), heads(k), heads(v)
    scale = SB_HEAD_DIM ** -0.5
    outs = []
    for blk in range(S // SB_BLOCK):
        t0 = blk * SB_BLOCK
        t1 = t0 + SB_BLOCK
        z = jnp.einsum('bhtd,bhsd->bhts', qh[:, :, t0:t1], kh[:, :, :t1]).astype(jnp.float32) * scale
        before = jnp.arange(t1)[None, :] < (t0 + jnp.arange(SB_BLOCK))[:, None]
        log_keep = jnp.where(before, jax.nn.log_sigmoid(-z), 0.0)
        shifted = jnp.pad(log_keep[..., 1:], ((0, 0), (0, 0), (0, 0), (0, 1)))
        tail = lax.cumsum(shifted, axis=3, reverse=True)
        weights = jnp.where(before, jnp.exp(jax.nn.log_sigmoid(z) + tail), 0.0)
        outs.append(jnp.einsum('bhts,bhsd->bhtd', weights.astype(vh.dtype), vh[:, :, :t1]))
    o = jnp.concatenate(outs, axis=2)
    return o.transpose(0, 2, 1, 3).reshape(B, S, SB_W)


def memory_cross_attention(h, mem, wq, wk, wv, wo):
    B, S, _ = h.shape
    M = mem.shape[1]
    q = (h @ wq).reshape(B, S, XA_HEADS, XA_HEAD_DIM)
    k = (mem @ wk).reshape(B, M, XA_HEADS, XA_HEAD_DIM)
    v = (mem @ wv).reshape(B, M, XA_HEADS, XA_HEAD_DIM)
    s = jnp.einsum('bshd,bmhd->bhsm', q, k).astype(jnp.float32) * (XA_HEAD_DIM ** -0.5)
    p = jax.nn.softmax(s, axis=-1).astype(v.dtype)
    o = jnp.einsum('bhsm,bmhd->bshd', p, v).reshape(B, S, XA_W)
    return o @ wo


def hierarchical_moe(h, wg, bg, we, be, w1, w3, w2):
    B, S, D = h.shape
    T = B * S
    hf = h.reshape(T, D)
    g_prob = jax.nn.softmax((hf @ wg).astype(jnp.float32) + bg.astype(jnp.float32), axis=-1)
    g_top, g_idx = lax.top_k(g_prob, 1)
    e_logits = ((hf @ we).astype(jnp.float32) + be.astype(jnp.float32)).reshape(T, N_GROUPS, EXPERTS_PER_GROUP)
    sel_idx = jnp.broadcast_to(g_idx[:, :, None], (T, 1, EXPERTS_PER_GROUP))
    in_group = jnp.take_along_axis(e_logits, sel_idx, axis=1)[:, 0]
    e_prob = jax.nn.softmax(in_group, axis=-1)
    e_top, e_idx = lax.top_k(e_prob, TOP_K_INNER)
    gate_w = g_top * (e_top / jnp.sum(e_top, axis=-1, keepdims=True))
    expert_ids = g_idx * EXPERTS_PER_GROUP + e_idx
    combine = jnp.einsum('tk,tke->te', gate_w,
                         jax.nn.one_hot(expert_ids, N_EXPERTS, dtype=jnp.float32)).astype(hf.dtype)
    y = jnp.zeros_like(hf)
    for e in range(N_EXPERTS):
        hidden = jax.nn.silu(hf @ w1[e]) * (hf @ w3[e])
        y = y + combine[:, e:e + 1] * (hidden @ w2[e])
    return y.reshape(B, S, D)


def setup_inputs(seed: int = 0) -> dict:
    key = jax.random.key(seed)
    ks = jax.random.split(key, 32)
    f32 = jnp.float32

    def nrm(k, shape, scale):
        return jax.random.normal(k, shape, f32) * scale

    def gain(k, shape):
        return jnp.ones(shape, f32) + 0.02 * jax.random.normal(k, shape, f32)

    return {
        "x": nrm(ks[0], (BATCH, SEQ, D_MODEL), 1.0),
        "mem": nrm(ks[1], (BATCH, MEM_LEN, D_MODEL), 1.0),
        "ln_in_g": gain(ks[2], (D_MODEL,)),
        "ln_in_b": nrm(ks[3], (D_MODEL,), 0.02),
        "w_in": nrm(ks[4], (DEPTH, D_MODEL, N_IN), D_MODEL ** -0.5),
        "hg_lb_logits": nrm(ks[5], (DEPTH + 1, HG_QK), 0.1),
        "hg_norm_g": gain(ks[6], (DEPTH, HG_VAL_DIM)),
        "w_branch_a": nrm(ks[7], (DEPTH, HG_V, D_MODEL), HG_V ** -0.5 * DN_BETA),
        "w_branch_b": nrm(ks[8], (DEPTH, SB_W, D_MODEL), SB_W ** -0.5 * DN_BETA),
        "w_mix_out": nrm(ks[9], (DEPTH, D_MODEL, D_MODEL), D_MODEL ** -0.5 * DN_BETA),
        "ln1_g": gain(ks[10], (DEPTH, D_MODEL)),
        "ln1_b": nrm(ks[11], (DEPTH, D_MODEL), 0.02),
        "xa_wq": nrm(ks[12], (DEPTH, D_MODEL, XA_W), D_MODEL ** -0.5),
        "xa_wk": nrm(ks[13], (DEPTH, D_MODEL, XA_W), D_MODEL ** -0.5),
        "xa_wv": nrm(ks[14], (DEPTH, D_MODEL, XA_W), D_MODEL ** -0.5 * DN_BETA),
        "xa_wo": nrm(ks[15], (DEPTH, XA_W, D_MODEL), XA_W ** -0.5 * DN_BETA),
        "ln2_g": gain(ks[16], (DEPTH, D_MODEL)),
        "ln2_b": nrm(ks[17], (DEPTH, D_MODEL), 0.02),
        "router_wg": nrm(ks[18], (DEPTH, D_MODEL, N_GROUPS), D_MODEL ** -0.5),
        "router_bg": nrm(ks[19], (DEPTH, N_GROUPS), 0.01),
        "router_we": nrm(ks[20], (DEPTH, D_MODEL, N_EXPERTS), D_MODEL ** -0.5),
        "router_be": nrm(ks[21], (DEPTH, N_EXPERTS), 0.01),
        "moe_w1": nrm(ks[22], (DEPTH, N_EXPERTS, D_MODEL, EXPERT_FF), D_MODEL ** -0.5),
        "moe_w3": nrm(ks[23], (DEPTH, N_EXPERTS, D_MODEL, EXPERT_FF), D_MODEL ** -0.5),
        "moe_w2": nrm(ks[24], (DEPTH, N_EXPERTS, EXPERT_FF, D_MODEL), EXPERT_FF ** -0.5 * DN_BETA),
        "ln3_g": gain(ks[25], (DEPTH, D_MODEL)),
        "ln3_b": nrm(ks[26], (DEPTH, D_MODEL), 0.02),
    }


def reference(x, mem, ln_in_g, ln_in_b, w_in, hg_lb_logits, hg_norm_g, w_branch_a, w_branch_b,
              w_mix_out, ln1_g, ln1_b, xa_wq, xa_wk, xa_wv, xa_wo, ln2_g, ln2_b,
              router_wg, router_bg, router_we, router_be, moe_w1, moe_w3, moe_w2, ln3_g, ln3_b):
    lower_bounds = jnp.cumsum(jax.nn.softmax(hg_lb_logits.astype(jnp.float32), axis=0), axis=0)
    split_at = list(np.cumsum(IN_SPLITS)[:-1])
    h = layer_norm(x, ln_in_g, ln_in_b)
    for l in range(DEPTH):
        proj = h @ w_in[l]
        q_hg, f_hg, i_hg, g_hg, q_sb, k_sb, v_sb, gate_a, gate_b = jnp.split(proj, split_at, axis=-1)
        y_a = hgrn2_branch(q_hg, f_hg, i_hg, g_hg, lower_bounds[l], hg_norm_g[l]) @ w_branch_a[l]
        y_b = stick_breaking_branch(q_sb, k_sb, v_sb) @ w_branch_b[l]
        merged = jax.nn.sigmoid(gate_a) * y_a + jax.nn.sigmoid(gate_b) * y_b
        h = layer_norm(DN_ALPHA * h + merged @ w_mix_out[l], ln1_g[l], ln1_b[l])
        h = layer_norm(DN_ALPHA * h + memory_cross_attention(h, mem, xa_wq[l], xa_wk[l], xa_wv[l], xa_wo[l]),
                       ln2_g[l], ln2_b[l])
        moe_out = hierarchical_moe(h, router_wg[l], router_bg[l], router_we[l], router_be[l],
                                   moe_w1[l], moe_w3[l], moe_w2[l])
        h = layer_norm(DN_ALPHA * h + moe_out, ln3_g[l], ln3_b[l])
    return h
```

```python
import functools

import jax
import jax.numpy as jnp
from jax import lax
from jax.experimental import pallas as pl
from jax.experimental.pallas import tpu as pltpu

F32 = jnp.float32
BF16 = jnp.bfloat16

D_MODEL = 1024
HG_HEADS = 4
HG_DIM = 128
HG_W = HG_HEADS * HG_DIM
HG_CHUNK = 64
SB_HEADS = 8
SB_HEAD_DIM = 64
SB_W = SB_HEADS * SB_HEAD_DIM
SB_BLOCK = 128
XA_HEADS = 4
XA_HEAD_DIM = D_MODEL // XA_HEADS
N_GROUPS = 4
EXPERTS_PER_GROUP = 4
N_EXPERTS = N_GROUPS * EXPERTS_PER_GROUP
EXPERT_FF = 512
N_IN = HG_W * 4 + SB_W * 3 + D_MODEL * 2
COL_HG = 0
COL_SB = 4 * HG_W
COL_GATE = COL_SB + 3 * SB_W
LN_EPS = 1e-5
RMS_EPS = 1e-6
DEPTH = 1
DN_ALPHA = (2 * DEPTH) ** 0.25

LANES = 128
VMEM_LIMIT = 56 * 1024 * 1024


def _cparams(sem):
    return pltpu.CompilerParams(dimension_semantics=sem, vmem_limit_bytes=VMEM_LIMIT)


def _layer_norm(x, g, b):
    mu = jnp.mean(x, axis=-1, keepdims=True)
    xc = x - mu
    var = jnp.mean(xc * xc, axis=-1, keepdims=True)
    return xc * lax.rsqrt(var + LN_EPS) * g + b


def _dot(a, b):
    return jnp.dot(a, b, preferred_element_type=F32)


def _dot_nt(a, b):
    return lax.dot_general(a, b, (((1,), (1,)), ((), ())), preferred_element_type=F32)


def _dot_tn(a, b):
    return lax.dot_general(a, b, (((0,), (0,)), ((), ())), preferred_element_type=F32)


def _split3(a):
    hi = a.astype(BF16)
    r1 = a - hi.astype(F32)
    mid = r1.astype(BF16)
    lo = (r1 - mid.astype(F32)).astype(BF16)
    return hi, mid, lo


def _const_spec(shape):
    nd = len(shape)
    return pl.BlockSpec(shape, lambda *_: (0,) * nd)


def _in_proj_kernel(x_ref, g_ref, b_ref, w_ref, o_ref, *, col_chunk):
    h = _layer_norm(x_ref[...], g_ref[...], b_ref[...]).astype(BF16)
    for c in range(N_IN // col_chunk):
        sl = slice(c * col_chunk, (c + 1) * col_chunk)
        o_ref[:, sl] = _dot(h, w_ref[:, sl]).astype(BF16)


def _in_proj(x2, g, b, w_bf, tm=512, col_chunk=512):
    T = x2.shape[0]
    return pl.pallas_call(
        functools.partial(_in_proj_kernel, col_chunk=col_chunk),
        out_shape=jax.ShapeDtypeStruct((T, N_IN), BF16),
        grid=(T // tm,),
        in_specs=[
            pl.BlockSpec((tm, D_MODEL), lambda i: (i, 0)),
            _const_spec((1, D_MODEL)),
            _const_spec((1, D_MODEL)),
            _const_spec((D_MODEL, N_IN)),
        ],
        out_specs=pl.BlockSpec((tm, N_IN), lambda i: (i, 0)),
        compiler_params=_cparams(("parallel",)),
        name="in_proj",
    )(x2, g, b, w_bf)


def _hgrn2_kernel(lbl_ref, ng_ref, q_ref, f_ref, i_ref, g_ref, o_ref, state_ref, *, layer, tb):
    C = HG_CHUNK
    H2 = C // 2

    @pl.when(pl.program_id(1) == 0)
    def _():
        state_ref[...] = jnp.zeros_like(state_ref)

    lbl = lbl_ref[...]
    ex = jnp.exp(lbl - jnp.max(lbl, axis=0, keepdims=True))
    lb = jnp.sum(ex[: layer + 1], axis=0, keepdims=True) / jnp.sum(ex, axis=0, keepdims=True)

    row = lax.broadcasted_iota(jnp.int32, (C, C), 0)
    col = lax.broadcasted_iota(jnp.int32, (C, C), 1)
    tri = (col <= row).astype(BF16)
    same_half = (row < H2) == (col < H2)
    m_diag = same_half & (col <= row)
    m_cross = (row >= H2) & (col < H2)
    rsel = lax.broadcasted_iota(jnp.int32, (C, HG_DIM), 0) < H2
    ng = ng_ref[...]

    def chunk(c, carry):
        r0 = pl.multiple_of(c * C, C)
        rows = pl.ds(r0, C)
        fl = f_ref[rows, :].astype(F32)
        forget = lb + (1.0 - lb) * jax.nn.sigmoid(fl)
        lf = jnp.log(forget)
        kk_all = 1.0 - forget
        hi, mid, lo = _split3(lf)
        b_all = _dot(tri, hi) + _dot(tri, mid) + _dot(tri, lo)
        for h in range(HG_HEADS):
            ls = slice(h * HG_DIM, (h + 1) * HG_DIM)
            b = b_all[:, ls]
            kk = kk_all[:, ls]
            q = q_ref[rows, ls].astype(F32)
            v = i_ref[rows, ls]
            ref = jnp.where(rsel, b[H2 // 2 - 1:H2 // 2, :], b[H2 + H2 // 2 - 1:H2 + H2 // 2, :])
            d1 = b - ref
            q1 = (q * jnp.exp(d1)).astype(BF16)
            k1 = (kk * jnp.exp(-d1)).astype(BF16)
            e3 = jnp.exp(-jnp.abs(b - b[H2 - 1:H2, :]))
            q3 = (q * e3).astype(BF16)
            k3 = (kk * e3).astype(BF16)
            s1 = _dot_nt(q1, k1)
            s3 = _dot_nt(q3, k3)
            s = jnp.where(m_diag, s1, jnp.where(m_cross, s3, 0.0)).astype(BF16)
            st = state_ref[h]
            qs = (q * jnp.exp(b)).astype(BF16)
            o = _dot(s, v) + _dot_nt(qs, st.astype(BF16))
            b_last = b[C - 1:C, :]
            kd = (kk * jnp.exp(b_last - b)).astype(BF16)
            state_ref[h] = st * jnp.exp(b_last) + _dot_tn(v, kd)
            rms = lax.rsqrt(jnp.mean(o * o, axis=-1, keepdims=True) + RMS_EPS)
            gate = g_ref[rows, ls].astype(F32)
            o_ref[rows, ls] = (o * rms * ng * (gate * jax.nn.sigmoid(gate))).astype(o_ref.dtype)
        return carry

    lax.fori_loop(0, tb // C, chunk, 0)


def _hgrn2(proj, lb_logits, norm_g, B, S, layer, tb=512):
    T = B * S
    nsb = S // tb

    def spec(k):
        return pl.BlockSpec((tb, HG_W), lambda b, s, k=k: (b * nsb + s, COL_HG // HG_W + k))

    return pl.pallas_call(
        functools.partial(_hgrn2_kernel, layer=layer, tb=tb),
        out_shape=jax.ShapeDtypeStruct((T, HG_W), BF16),
        grid=(B, nsb),
        in_specs=[
            _const_spec(lb_logits.shape),
            _const_spec((1, HG_DIM)),
            spec(0), spec(1), spec(2), spec(3),
        ],
        out_specs=pl.BlockSpec((tb, HG_W), lambda b, s: (b * nsb + s, 0)),
        scratch_shapes=[pltpu.VMEM((HG_HEADS, HG_DIM, HG_DIM), F32)],
        compiler_params=_cparams(("parallel", "arbitrary")),
        name="hgrn2",
    )(lb_logits, norm_g, proj, proj, proj, proj)


def _softplus(z):
    return jnp.maximum(z, 0.0) + jnp.log(1.0 + jnp.exp(-jnp.abs(z)))


def _sb_kernel(q_ref, k_ref, v_ref, o_ref):
    i = pl.program_id(2)
    BLK = SB_BLOCK
    scale = SB_HEAD_DIM ** -0.5
    q = q_ref[...]
    lane = lax.broadcasted_iota(jnp.int32, (BLK, LANES), 1)
    row = lax.broadcasted_iota(jnp.int32, (BLK, BLK), 0)
    col = lax.broadcasted_iota(jnp.int32, (BLK, BLK), 1)
    qs = q * jnp.asarray(scale, q.dtype)
    qh = (jnp.where(lane < SB_HEAD_DIM, qs, jnp.zeros_like(qs)),
          jnp.where(lane >= SB_HEAD_DIM, qs, jnp.zeros_like(qs)))
    after = (row > col).astype(BF16)
    before = col < row

    def block(j, carry, diag):
        r0 = pl.multiple_of(j * BLK, BLK)
        kb = k_ref[pl.ds(r0, BLK), :]
        vb = v_ref[pl.ds(r0, BLK), :]
        new = []
        for h in range(2):
            acc, run = carry[h]
            z = _dot_nt(qh[h], kb)
            sp = _softplus(z)
            lk = -sp
            if diag:
                lk = jnp.where(before, lk, 0.0)
            hi = lk.astype(BF16)
            lo = (lk - hi.astype(F32)).astype(BF16)
            tail = run + _dot(hi, after) + _dot(lo, after)
            w = jnp.exp(z - sp + tail)
            if diag:
                w = jnp.where(before, w, 0.0)
            acc = acc + _dot(w.astype(BF16), vb)
            run = run + jnp.sum(lk, axis=-1, keepdims=True)
            new.append((acc, run))
        return tuple(new)

    zero = (jnp.zeros((BLK, LANES), F32), jnp.zeros((BLK, 1), F32))
    carry = block(i, (zero, zero), True)
    carry = lax.fori_loop(1, i + 1, lambda jj, c: block(i - jj, c, False), carry)
    o_ref[...] = jnp.where(lane < SB_HEAD_DIM, carry[0][0], carry[1][0]).astype(o_ref.dtype)


def _sb_attn(proj, B, S):
    T = B * S
    nq = S // SB_BLOCK
    hp = SB_W // LANES
    cq = COL_SB // LANES
    ck = (COL_SB + SB_W) // LANES
    cv = (COL_SB + 2 * SB_W) // LANES
    return pl.pallas_call(
        _sb_kernel,
        out_shape=jax.ShapeDtypeStruct((T, SB_W), BF16),
        grid=(B, hp, nq),
        in_specs=[
            pl.BlockSpec((SB_BLOCK, LANES), lambda b, p, i: (b * nq + i, cq + p)),
            pl.BlockSpec((S, LANES), lambda b, p, i: (b, ck + p)),
            pl.BlockSpec((S, LANES), lambda b, p, i: (b, cv + p)),
        ],
        out_specs=pl.BlockSpec((SB_BLOCK, LANES), lambda b, p, i: (b * nq + i, p)),
        compiler_params=_cparams(("parallel", "parallel", "arbitrary")),
        name="sb_attn",
    )(proj, proj, proj)


GATE_BLK = 512


def _mix_kernel(x_ref, gin_ref, bin_ref, oa_ref, ob_ref, ga0_ref, ga1_ref, gb0_ref, gb1_ref,
                wa_ref, wb_ref, wm_ref, g1_ref, b1_ref, o_ref):
    h0 = _layer_norm(x_ref[...], gin_ref[...], bin_ref[...])
    oa = oa_ref[...]
    ob = ob_ref[...]
    y = None
    for half, (ga_ref, gb_ref) in enumerate(((ga0_ref, gb0_ref), (ga1_ref, gb1_ref))):
        cs = slice(half * GATE_BLK, (half + 1) * GATE_BLK)
        merged = (jax.nn.sigmoid(ga_ref[...].astype(F32)) * _dot(oa, wa_ref[:, cs])
                  + jax.nn.sigmoid(gb_ref[...].astype(F32)) * _dot(ob, wb_ref[:, cs]))
        part = _dot(merged.astype(BF16), wm_ref[cs, :])
        y = part if y is None else y + part
    o_ref[...] = _layer_norm(DN_ALPHA * h0 + y, g1_ref[...], b1_ref[...])


def _mix(x2, gin, bin_, oa, ob, proj, wa, wb, wm, g1, b1, tm=512):
    T = x2.shape[0]
    c0 = COL_GATE // GATE_BLK

    def gate_spec(k):
        return pl.BlockSpec((tm, GATE_BLK), lambda i, k=k: (i, c0 + k))

    return pl.pallas_call(
        _mix_kernel,
        out_shape=jax.ShapeDtypeStruct((T, D_MODEL), F32),
        grid=(T // tm,),
        in_specs=[
            pl.BlockSpec((tm, D_MODEL), lambda i: (i, 0)),
            _const_spec((1, D_MODEL)), _const_spec((1, D_MODEL)),
            pl.BlockSpec((tm, HG_W), lambda i: (i, 0)),
            pl.BlockSpec((tm, SB_W), lambda i: (i, 0)),
            gate_spec(0), gate_spec(1), gate_spec(2), gate_spec(3),
            _const_spec((HG_W, D_MODEL)), _const_spec((SB_W, D_MODEL)), _const_spec((D_MODEL, D_MODEL)),
            _const_spec((1, D_MODEL)), _const_spec((1, D_MODEL)),
        ],
        out_specs=pl.BlockSpec((tm, D_MODEL), lambda i: (i, 0)),
        compiler_params=_cparams(("parallel",)),
        name="mix",
    )(x2, gin, bin_, oa, ob, proj, proj, proj, proj, wa, wb, wm, g1, b1)


def _xa_kv_kernel(m_ref, wk_ref, wv_ref, k_ref, v_ref):
    m = m_ref[0].astype(BF16)
    k_ref[0] = _dot(m, wk_ref[...]).astype(BF16)
    v_ref[0] = _dot(m, wv_ref[...]).astype(BF16)


def _xa_kv(mem, wk, wv):
    B, M, _ = mem.shape
    W = wk.shape[1]
    return pl.pallas_call(
        _xa_kv_kernel,
        out_shape=(jax.ShapeDtypeStruct((B, M, W), BF16), jax.ShapeDtypeStruct((B, M, W), BF16)),
        grid=(B,),
        in_specs=[pl.BlockSpec((1, M, D_MODEL), lambda b: (b, 0, 0)),
                  _const_spec(wk.shape), _const_spec(wv.shape)],
        out_specs=(pl.BlockSpec((1, M, W), lambda b: (b, 0, 0)),
                   pl.BlockSpec((1, M, W), lambda b: (b, 0, 0))),
        compiler_params=_cparams(("parallel",)),
        name="xa_kv",
    )(mem, wk, wv)


def _xattn_kernel(h_ref, k_ref, v_ref, wq_ref, wo_ref, g_ref, b_ref, o_ref):
    h = h_ref[...]
    q = _dot(h.astype(BF16), wq_ref[...])
    scale = XA_HEAD_DIM ** -0.5
    outs = []
    for hd in range(XA_HEADS):
        ls = slice(hd * XA_HEAD_DIM, (hd + 1) * XA_HEAD_DIM)
        s = _dot_nt(q[:, ls].astype(BF16), k_ref[0, :, ls]) * scale
        s = s - jnp.max(s, axis=-1, keepdims=True)
        p = jnp.exp(s)
        p = p / jnp.sum(p, axis=-1, keepdims=True)
        outs.append(_dot(p.astype(BF16), v_ref[0, :, ls]).astype(BF16))
    o = jnp.concatenate(outs, axis=-1)
    y = _dot(o, wo_ref[...])
    o_ref[...] = _layer_norm(DN_ALPHA * h + y, g_ref[...], b_ref[...])


def _xattn(h1, kx, vx, wq, wo, g2, b2, B, S, tm=512):
    T = B * S
    M = kx.shape[1]
    W = kx.shape[2]
    nsb = S // tm
    return pl.pallas_call(
        _xattn_kernel,
        out_shape=jax.ShapeDtypeStruct((T, D_MODEL), F32),
        grid=(B, nsb),
        in_specs=[
            pl.BlockSpec((tm, D_MODEL), lambda b, s: (b * nsb + s, 0)),
            pl.BlockSpec((1, M, W), lambda b, s: (b, 0, 0)),
            pl.BlockSpec((1, M, W), lambda b, s: (b, 0, 0)),
            _const_spec(wq.shape), _const_spec(wo.shape),
            _const_spec((1, D_MODEL)), _const_spec((1, D_MODEL)),
        ],
        out_specs=pl.BlockSpec((tm, D_MODEL), lambda b, s: (b * nsb + s, 0)),
        compiler_params=_cparams(("parallel", "arbitrary")),
        name="xattn",
    )(h1, kx, vx, wq, wo, g2, b2)


def _first_argmax(vals, lane, width):
    m = jnp.max(vals, axis=-1, keepdims=True)
    idx = jnp.min(jnp.where(vals == m, lane, width), axis=-1, keepdims=True)
    return m, idx


def _router(h, wg, bg, we, be):
    gl = jnp.dot(h, wg, preferred_element_type=F32, precision=lax.Precision.HIGHEST) + bg
    el = jnp.dot(h, we, preferred_element_type=F32, precision=lax.Precision.HIGHEST) + be
    tm = h.shape[0]
    lane_g = lax.broadcasted_iota(jnp.int32, (tm, N_GROUPS), 1)
    lane_e = lax.broadcasted_iota(jnp.int32, (tm, N_EXPERTS), 1)
    gm, g_idx = _first_argmax(gl, lane_g, N_GROUPS)
    g_top = 1.0 / jnp.sum(jnp.exp(gl - gm), axis=-1, keepdims=True)
    neg = jnp.float32(-jnp.inf)
    ml = jnp.where(lane_e // EXPERTS_PER_GROUP == g_idx, el, neg)
    m1, i1 = _first_argmax(ml, lane_e, N_EXPERTS)
    ml2 = jnp.where(lane_e == i1, neg, ml)
    m2, i2 = _first_argmax(ml2, lane_e, N_EXPERTS)
    e2 = jnp.exp(m2 - m1)
    w1 = g_top / (1.0 + e2)
    w2 = g_top * e2 / (1.0 + e2)
    return jnp.where(lane_e == i1, w1, 0.0) + jnp.where(lane_e == i2, w2, 0.0)


def _moe_kernel(h_ref, wg_ref, bg_ref, we_ref, be_ref, w1_ref, w3_ref, w2_ref, g_ref, b_ref,
                o_ref, acc_ref, comb_ref, hb_ref):
    e = pl.program_id(1)

    @pl.when(e == 0)
    def _():
        h = h_ref[...]
        hb_ref[...] = h.astype(BF16)
        comb_ref[...] = _router(h, wg_ref[...], bg_ref[...], we_ref[...], be_ref[...])
        acc_ref[...] = jnp.zeros_like(acc_ref)

    hb = hb_ref[...]
    a = _dot(hb, w1_ref[0])
    b = _dot(hb, w3_ref[0])
    hid = (a * jax.nn.sigmoid(a) * b).astype(BF16)
    y = _dot(hid, w2_ref[0])
    comb = comb_ref[...]
    lane_e = lax.broadcasted_iota(jnp.int32, comb.shape, 1)
    c = jnp.sum(jnp.where(lane_e == e, comb, 0.0), axis=-1, keepdims=True)
    acc_ref[...] += c * y

    @pl.when(e == N_EXPERTS - 1)
    def _():
        o_ref[...] = _layer_norm(DN_ALPHA * h_ref[...] + acc_ref[...], g_ref[...], b_ref[...])


def _moe(h2, wg, bg, we, be, w1, w3, w2, g3, b3, tm=1024):
    T = h2.shape[0]
    return pl.pallas_call(
        _moe_kernel,
        out_shape=jax.ShapeDtypeStruct((T, D_MODEL), F32),
        grid=(T // tm, N_EXPERTS),
        in_specs=[
            pl.BlockSpec((tm, D_MODEL), lambda i, e: (i, 0)),
            _const_spec(wg.shape), _const_spec(bg.shape), _const_spec(we.shape), _const_spec(be.shape),
            pl.BlockSpec((1, D_MODEL, EXPERT_FF), lambda i, e: (e, 0, 0)),
            pl.BlockSpec((1, D_MODEL, EXPERT_FF), lambda i, e: (e, 0, 0)),
            pl.BlockSpec((1, EXPERT_FF, D_MODEL), lambda i, e: (e, 0, 0)),
            _const_spec((1, D_MODEL)), _const_spec((1, D_MODEL)),
        ],
        out_specs=pl.BlockSpec((tm, D_MODEL), lambda i, e: (i, 0)),
        scratch_shapes=[pltpu.VMEM((tm, D_MODEL), F32), pltpu.VMEM((tm, N_EXPERTS), F32),
                        pltpu.VMEM((tm, D_MODEL), BF16)],
        compiler_params=_cparams(("parallel", "arbitrary")),
        name="moe",
    )(h2, wg, bg, we, be, w1, w3, w2, g3, b3)


def kernel(x, mem, ln_in_g, ln_in_b, w_in, hg_lb_logits, hg_norm_g, w_branch_a, w_branch_b, w_mix_out,
           ln1_g, ln1_b, xa_wq, xa_wk, xa_wv, xa_wo, ln2_g, ln2_b, router_wg, router_bg, router_we,
           router_be, moe_w1, moe_w3, moe_w2, ln3_g, ln3_b):
    B, S, D = x.shape
    T = B * S
    row = lambda a: a.reshape(1, -1).astype(F32)
    x2 = x.reshape(T, D)
    gin, bin_ = row(ln_in_g), row(ln_in_b)
    assert w_in.shape[0] == DEPTH
    l = 0
    proj = _in_proj(x2, gin, bin_, w_in[l].astype(BF16))
    oa = _hgrn2(proj, hg_lb_logits.astype(F32), row(hg_norm_g[l]), B, S, l)
    ob = _sb_attn(proj, B, S)
    h1 = _mix(x2, gin, bin_, oa, ob, proj, w_branch_a[l].astype(BF16), w_branch_b[l].astype(BF16),
              w_mix_out[l].astype(BF16), row(ln1_g[l]), row(ln1_b[l]))
    kx, vx = _xa_kv(mem, xa_wk[l].astype(BF16), xa_wv[l].astype(BF16))
    h2 = _xattn(h1, kx, vx, xa_wq[l].astype(BF16), xa_wo[l].astype(BF16), row(ln2_g[l]), row(ln2_b[l]), B, S)
    out = _moe(h2, router_wg[l].astype(F32), row(router_bg[l]), router_we[l].astype(F32), row(router_be[l]),
               moe_w1[l].astype(BF16), moe_w3[l].astype(BF16), moe_w2[l].astype(BF16),
               row(ln3_g[l]), row(ln3_b[l]))
    return out.reshape(B, S, D)
```

```python
import functools

import jax
import jax.numpy as jnp
from jax import lax
from jax.experimental import pallas as pl
from jax.experimental.pallas import tpu as pltpu

F32 = jnp.float32
BF16 = jnp.bfloat16

D_MODEL = 1024
HG_HEADS = 4
HG_DIM = 128
HG_W = HG_HEADS * HG_DIM
HG_CHUNK = 64
SB_HEADS = 8
SB_HEAD_DIM = 64
SB_W = SB_HEADS * SB_HEAD_DIM
SB_BLOCK = 128
XA_HEADS = 4
XA_HEAD_DIM = D_MODEL // XA_HEADS
N_GROUPS = 4
EXPERTS_PER_GROUP = 4
N_EXPERTS = N_GROUPS * EXPERTS_PER_GROUP
EXPERT_FF = 512
N_IN = HG_W * 4 + SB_W * 3 + D_MODEL * 2
COL_HG = 0
COL_SB = 4 * HG_W
COL_GATE = COL_SB + 3 * SB_W
LN_EPS = 1e-5
RMS_EPS = 1e-6
DEPTH = 1
DN_ALPHA = (2 * DEPTH) ** 0.25

LANES = 128
VMEM_LIMIT = 56 * 1024 * 1024


def _cparams(sem, flags=None):
    return pltpu.CompilerParams(dimension_semantics=sem, vmem_limit_bytes=VMEM_LIMIT, flags=flags)


def _layer_norm(x, g, b):
    mu = jnp.mean(x, axis=-1, keepdims=True)
    xc = x - mu
    var = jnp.mean(xc * xc, axis=-1, keepdims=True)
    return xc * lax.rsqrt(var + LN_EPS) * g + b


def _dot(a, b):
    return jnp.dot(a, b, preferred_element_type=F32)


def _dot_nt(a, b):
    return lax.dot_general(a, b, (((1,), (1,)), ((), ())), preferred_element_type=F32)


def _dot_tn(a, b):
    return lax.dot_general(a, b, (((0,), (0,)), ((), ())), preferred_element_type=F32)


def _split3(a):
    hi = a.astype(BF16)
    r1 = a - hi.astype(F32)
    mid = r1.astype(BF16)
    lo = (r1 - mid.astype(F32)).astype(BF16)
    return hi, mid, lo


def _const_spec(shape):
    nd = len(shape)
    return pl.BlockSpec(shape, lambda *_: (0,) * nd)


def _in_proj_kernel(x_ref, g_ref, b_ref, w_ref, o_ref, *, col_chunk):
    h = _layer_norm(x_ref[...], g_ref[...], b_ref[...]).astype(BF16)
    for c in range(N_IN // col_chunk):
        sl = slice(c * col_chunk, (c + 1) * col_chunk)
        o_ref[:, sl] = _dot(h, w_ref[:, sl]).astype(BF16)


def _in_proj(x2, g, b, w_bf, tm=512, col_chunk=512):
    T = x2.shape[0]
    return pl.pallas_call(
        functools.partial(_in_proj_kernel, col_chunk=col_chunk),
        out_shape=jax.ShapeDtypeStruct((T, N_IN), BF16),
        grid=(T // tm,),
        in_specs=[
            pl.BlockSpec((tm, D_MODEL), lambda i: (i, 0)),
            _const_spec((1, D_MODEL)),
            _const_spec((1, D_MODEL)),
            _const_spec((D_MODEL, N_IN)),
        ],
        out_specs=pl.BlockSpec((tm, N_IN), lambda i: (i, 0)),
        compiler_params=_cparams(("parallel",)),
        name="in_proj",
    )(x2, g, b, w_bf)


def _hgrn2_kernel(lbl_ref, ng_ref, q_ref, f_ref, i_ref, g_ref, o_ref, state_ref, *, layer, tb):
    C = HG_CHUNK
    H2 = C // 2

    @pl.when(pl.program_id(1) == 0)
    def _():
        state_ref[...] = jnp.zeros_like(state_ref)

    lbl = lbl_ref[...]
    ex = jnp.exp(lbl - jnp.max(lbl, axis=0, keepdims=True))
    lb = jnp.sum(ex[: layer + 1], axis=0, keepdims=True) / jnp.sum(ex, axis=0, keepdims=True)

    row = lax.broadcasted_iota(jnp.int32, (C, C), 0)
    col = lax.broadcasted_iota(jnp.int32, (C, C), 1)
    tri = (col <= row).astype(BF16)
    same_half = (row < H2) == (col < H2)
    m_diag = same_half & (col <= row)
    m_cross = (row >= H2) & (col < H2)
    rsel = lax.broadcasted_iota(jnp.int32, (C, HG_DIM), 0) < H2
    ng = ng_ref[...]

    def chunk(c, carry):
        r0 = pl.multiple_of(c * C, C)
        rows = pl.ds(r0, C)
        fl = f_ref[rows, :].astype(F32)
        forget = lb + (1.0 - lb) * jax.nn.sigmoid(fl)
        lf = jnp.log(forget)
        kk_all = 1.0 - forget
        hi, mid, lo = _split3(lf)
        b_all = _dot(tri, hi) + _dot(tri, mid) + _dot(tri, lo)
        for h in range(HG_HEADS):
            ls = slice(h * HG_DIM, (h + 1) * HG_DIM)
            b = b_all[:, ls]
            kk = kk_all[:, ls]
            q = q_ref[rows, ls].astype(F32)
            v = i_ref[rows, ls]
            ref = jnp.where(rsel, b[H2 // 2 - 1:H2 // 2, :], b[H2 + H2 // 2 - 1:H2 + H2 // 2, :])
            d1 = b - ref
            q1 = (q * jnp.exp(d1)).astype(BF16)
            k1 = (kk * jnp.exp(-d1)).astype(BF16)
            e3 = jnp.exp(-jnp.abs(b - b[H2 - 1:H2, :]))
            q3 = (q * e3).astype(BF16)
            k3 = (kk * e3).astype(BF16)
            s1 = _dot_nt(q1, k1)
            s3 = _dot_nt(q3, k3)
            s = jnp.where(m_diag, s1, jnp.where(m_cross, s3, 0.0)).astype(BF16)
            st = state_ref[h]
            qs = (q * jnp.exp(b)).astype(BF16)
            o = _dot(s, v) + _dot_nt(qs, st.astype(BF16))
            b_last = b[C - 1:C, :]
            kd = (kk * jnp.exp(b_last - b)).astype(BF16)
            state_ref[h] = st * jnp.exp(b_last) + _dot_tn(v, kd)
            rms = lax.rsqrt(jnp.mean(o * o, axis=-1, keepdims=True) + RMS_EPS)
            gate = g_ref[rows, ls].astype(F32)
            o_ref[rows, ls] = (o * rms * ng * (gate * jax.nn.sigmoid(gate))).astype(o_ref.dtype)
        return carry

    lax.fori_loop(0, tb // C, chunk, 0)


def _hgrn2(proj, lb_logits, norm_g, B, S, layer, tb=512):
    T = B * S
    nsb = S // tb

    def spec(k):
        return pl.BlockSpec((tb, HG_W), lambda b, s, k=k: (b * nsb + s, COL_HG // HG_W + k))

    return pl.pallas_call(
        functools.partial(_hgrn2_kernel, layer=layer, tb=tb),
        out_shape=jax.ShapeDtypeStruct((T, HG_W), BF16),
        grid=(B, nsb),
        in_specs=[
            _const_spec(lb_logits.shape),
            _const_spec((1, HG_DIM)),
            spec(0), spec(1), spec(2), spec(3),
        ],
        out_specs=pl.BlockSpec((tb, HG_W), lambda b, s: (b * nsb + s, 0)),
        scratch_shapes=[pltpu.VMEM((HG_HEADS, HG_DIM, HG_DIM), F32)],
        compiler_params=_cparams(("parallel", "arbitrary")),
        name="hgrn2",
    )(lb_logits, norm_g, proj, proj, proj, proj)


LOG2E = 1.4426950408889634


def _sb_kernel(q_ref, k_ref, v_ref, o_ref, acc_ref, run_ref, *, npairs, tq):
    i = pl.program_id(1)
    BLK = SB_BLOCK
    W2 = 2 * BLK
    ndiag = tq // BLK
    scale = SB_HEAD_DIM ** -0.5
    lane = lax.broadcasted_iota(jnp.int32, (BLK, LANES), 1)
    head0 = lane < SB_HEAD_DIM
    zrow = lax.broadcasted_iota(jnp.int32, (tq, W2), 0)
    zcol = lax.broadcasted_iota(jnp.int32, (tq, W2), 1) & (BLK - 1)
    wr = lax.broadcasted_iota(jnp.int32, (W2, W2), 0) & (BLK - 1)
    wc = lax.broadcasted_iota(jnp.int32, (W2, W2), 1)
    cw = ((wc >= BLK) | (wr > wc)).astype(BF16)
    qs = q_ref[...] * jnp.asarray(scale, q_ref.dtype)

    def masked_kv(ref, j, p):
        x = ref[pl.ds(pl.multiple_of(j * BLK, BLK), BLK), p * LANES:(p + 1) * LANES]
        zero = jnp.zeros_like(x)
        return jnp.concatenate([jnp.where(head0, x, zero), jnp.where(head0, zero, x)], axis=0)

    def step(js, ds):
        nb = len(js)
        zs = [[_dot_nt(qs[:, p * LANES:(p + 1) * LANES], masked_kv(k_ref, js[b], p)) * LOG2E
               for p in range(npairs)] for b in range(nb)]
        ws = []
        for b in range(nb):
            before = None if ds[b] is None else (zcol + ds[b] * BLK) < zrow
            lbs, hls = [], []
            for p in range(npairs):
                z = zs[b][p]
                l2 = jnp.log2(1.0 + jnp.exp2(-jnp.abs(z)))
                lb = jnp.minimum(z, 0.0) - l2
                lk = lb - z
                if before is not None:
                    lk = jnp.where(before, lk, 0.0)
                hi = lk.astype(BF16)
                lo = (lk - hi.astype(F32)).astype(BF16)
                lbs.append(lb)
                hls.append(jnp.concatenate([hi[:, :BLK], lo[:, :BLK]], axis=1))
                hls.append(jnp.concatenate([hi[:, BLK:], lo[:, BLK:]], axis=1))
            cs = _dot(jnp.concatenate(hls, axis=0), cw)
            wb = []
            for p in range(npairs):
                c0 = cs[(2 * p) * tq:(2 * p + 1) * tq]
                c1 = cs[(2 * p + 1) * tq:(2 * p + 2) * tq]
                cum = jnp.concatenate([c0[:, :BLK], c1[:, :BLK]], axis=1)
                rsum = jnp.concatenate([c0[:, BLK:], c1[:, BLK:]], axis=1)
                run = run_ref[p]
                w = jnp.exp2(lbs[p] + (run + cum))
                if before is not None:
                    w = jnp.where(before, w, 0.0)
                run_ref[p] = run + rsum
                wb.append(w.astype(BF16))
            ws.append(wb)
        for p in range(npairs):
            o = acc_ref[p]
            for b in range(nb):
                o = o + _dot(ws[b][p], masked_kv(v_ref, js[b], p))
            acc_ref[p] = o

    acc_ref[...] = jnp.zeros_like(acc_ref)
    run_ref[...] = jnp.zeros_like(run_ref)
    step([i * ndiag + d for d in reversed(range(ndiag))], list(reversed(range(ndiag))))

    def body(m, c):
        base = (i - 1 - m) * ndiag
        step([base + d for d in reversed(range(ndiag))], [None] * ndiag)
        return c

    lax.fori_loop(0, i, body, 0)
    for p in range(npairs):
        o_ref[:, p * LANES:(p + 1) * LANES] = acc_ref[p].astype(o_ref.dtype)


def _sb_attn(proj, B, S, npairs=SB_W // LANES, tq=256):
    T = B * S
    nq = S // tq
    wblk = npairs * LANES
    ng = SB_W // wblk
    cq = COL_SB // wblk
    ck = (COL_SB + SB_W) // wblk
    cv = (COL_SB + 2 * SB_W) // wblk
    return pl.pallas_call(
        functools.partial(_sb_kernel, npairs=npairs, tq=tq),
        out_shape=jax.ShapeDtypeStruct((T, SB_W), BF16),
        grid=(B * ng, nq),
        in_specs=[
            pl.BlockSpec((tq, wblk), lambda g, i: ((g // ng) * nq + i, cq + g % ng)),
            pl.BlockSpec((S, wblk), lambda g, i: (g // ng, ck + g % ng)),
            pl.BlockSpec((S, wblk), lambda g, i: (g // ng, cv + g % ng)),
        ],
        out_specs=pl.BlockSpec((tq, wblk), lambda g, i: ((g // ng) * nq + i, g % ng)),
        scratch_shapes=[pltpu.VMEM((npairs, tq, LANES), F32),
                        pltpu.VMEM((npairs, tq, 2 * SB_BLOCK), F32)],
        compiler_params=_cparams(("parallel", "arbitrary")),
        name="sb_attn",
    )(proj, proj, proj)


GATE_BLK = 512


def _mix_kernel(x_ref, gin_ref, bin_ref, oa_ref, ob_ref, ga0_ref, ga1_ref, gb0_ref, gb1_ref,
                wa_ref, wb_ref, wm_ref, g1_ref, b1_ref, o_ref):
    h0 = _layer_norm(x_ref[...], gin_ref[...], bin_ref[...])
    oa = oa_ref[...]
    ob = ob_ref[...]
    y = None
    for half, (ga_ref, gb_ref) in enumerate(((ga0_ref, gb0_ref), (ga1_ref, gb1_ref))):
        cs = slice(half * GATE_BLK, (half + 1) * GATE_BLK)
        merged = (jax.nn.sigmoid(ga_ref[...].astype(F32)) * _dot(oa, wa_ref[:, cs])
                  + jax.nn.sigmoid(gb_ref[...].astype(F32)) * _dot(ob, wb_ref[:, cs]))
        part = _dot(merged.astype(BF16), wm_ref[cs, :])
        y = part if y is None else y + part
    o_ref[...] = _layer_norm(DN_ALPHA * h0 + y, g1_ref[...], b1_ref[...])


def _mix(x2, gin, bin_, oa, ob, proj, wa, wb, wm, g1, b1, tm=512):
    T = x2.shape[0]
    c0 = COL_GATE // GATE_BLK

    def gate_spec(k):
        return pl.BlockSpec((tm, GATE_BLK), lambda i, k=k: (i, c0 + k))

    return pl.pallas_call(
        _mix_kernel,
        out_shape=jax.ShapeDtypeStruct((T, D_MODEL), F32),
        grid=(T // tm,),
        in_specs=[
            pl.BlockSpec((tm, D_MODEL), lambda i: (i, 0)),
            _const_spec((1, D_MODEL)), _const_spec((1, D_MODEL)),
            pl.BlockSpec((tm, HG_W), lambda i: (i, 0)),
            pl.BlockSpec((tm, SB_W), lambda i: (i, 0)),
            gate_spec(0), gate_spec(1), gate_spec(2), gate_spec(3),
            _const_spec((HG_W, D_MODEL)), _const_spec((SB_W, D_MODEL)), _const_spec((D_MODEL, D_MODEL)),
            _const_spec((1, D_MODEL)), _const_spec((1, D_MODEL)),
        ],
        out_specs=pl.BlockSpec((tm, D_MODEL), lambda i: (i, 0)),
        compiler_params=_cparams(("parallel",)),
        name="mix",
    )(x2, gin, bin_, oa, ob, proj, proj, proj, proj, wa, wb, wm, g1, b1)


def _xa_kv_kernel(m_ref, wk_ref, wv_ref, k_ref, v_ref):
    m = m_ref[0].astype(BF16)
    k_ref[0] = _dot(m, wk_ref[...]).astype(BF16)
    v_ref[0] = _dot(m, wv_ref[...]).astype(BF16)


def _xa_kv(mem, wk, wv):
    B, M, _ = mem.shape
    W = wk.shape[1]
    return pl.pallas_call(
        _xa_kv_kernel,
        out_shape=(jax.ShapeDtypeStruct((B, M, W), BF16), jax.ShapeDtypeStruct((B, M, W), BF16)),
        grid=(B,),
        in_specs=[pl.BlockSpec((1, M, D_MODEL), lambda b: (b, 0, 0)),
                  _const_spec(wk.shape), _const_spec(wv.shape)],
        out_specs=(pl.BlockSpec((1, M, W), lambda b: (b, 0, 0)),
                   pl.BlockSpec((1, M, W), lambda b: (b, 0, 0))),
        compiler_params=_cparams(("parallel",)),
        name="xa_kv",
    )(mem, wk, wv)


def _xattn_kernel(h_ref, k_ref, v_ref, wq_ref, wo_ref, g_ref, b_ref, o_ref):
    h = h_ref[...]
    q = _dot(h.astype(BF16), wq_ref[...])
    scale = XA_HEAD_DIM ** -0.5
    outs = []
    for hd in range(XA_HEADS):
        ls = slice(hd * XA_HEAD_DIM, (hd + 1) * XA_HEAD_DIM)
        s = _dot_nt(q[:, ls].astype(BF16), k_ref[0, :, ls]) * scale
        s = s - jnp.max(s, axis=-1, keepdims=True)
        p = jnp.exp(s)
        p = p / jnp.sum(p, axis=-1, keepdims=True)
        outs.append(_dot(p.astype(BF16), v_ref[0, :, ls]).astype(BF16))
    o = jnp.concatenate(outs, axis=-1)
    y = _dot(o, wo_ref[...])
    o_ref[...] = _layer_norm(DN_ALPHA * h + y, g_ref[...], b_ref[...])


def _xattn(h1, kx, vx, wq, wo, g2, b2, B, S, tm=512):
    T = B * S
    M = kx.shape[1]
    W = kx.shape[2]
    nsb = S // tm
    return pl.pallas_call(
        _xattn_kernel,
        out_shape=jax.ShapeDtypeStruct((T, D_MODEL), F32),
        grid=(B, nsb),
        in_specs=[
            pl.BlockSpec((tm, D_MODEL), lambda b, s: (b * nsb + s, 0)),
            pl.BlockSpec((1, M, W), lambda b, s: (b, 0, 0)),
            pl.BlockSpec((1, M, W), lambda b, s: (b, 0, 0)),
            _const_spec(wq.shape), _const_spec(wo.shape),
            _const_spec((1, D_MODEL)), _const_spec((1, D_MODEL)),
        ],
        out_specs=pl.BlockSpec((tm, D_MODEL), lambda b, s: (b * nsb + s, 0)),
        compiler_params=_cparams(("parallel", "arbitrary")),
        name="xattn",
    )(h1, kx, vx, wq, wo, g2, b2)


def _first_argmax(vals, lane, width):
    m = jnp.max(vals, axis=-1, keepdims=True)
    idx = jnp.min(jnp.where(vals == m, lane, width), axis=-1, keepdims=True)
    return m, idx


def _router(h, wg, bg, we, be):
    gl = jnp.dot(h, wg, preferred_element_type=F32, precision=lax.Precision.HIGHEST) + bg
    el = jnp.dot(h, we, preferred_element_type=F32, precision=lax.Precision.HIGHEST) + be
    tm = h.shape[0]
    lane_g = lax.broadcasted_iota(jnp.int32, (tm, N_GROUPS), 1)
    lane_e = lax.broadcasted_iota(jnp.int32, (tm, N_EXPERTS), 1)
    gm, g_idx = _first_argmax(gl, lane_g, N_GROUPS)
    g_top = 1.0 / jnp.sum(jnp.exp(gl - gm), axis=-1, keepdims=True)
    neg = jnp.float32(-jnp.inf)
    ml = jnp.where(lane_e // EXPERTS_PER_GROUP == g_idx, el, neg)
    m1, i1 = _first_argmax(ml, lane_e, N_EXPERTS)
    ml2 = jnp.where(lane_e == i1, neg, ml)
    m2, i2 = _first_argmax(ml2, lane_e, N_EXPERTS)
    e2 = jnp.exp(m2 - m1)
    w1 = g_top / (1.0 + e2)
    w2 = g_top * e2 / (1.0 + e2)
    return jnp.where(lane_e == i1, w1, 0.0) + jnp.where(lane_e == i2, w2, 0.0)


def _moe_kernel(h_ref, wg_ref, bg_ref, we_ref, be_ref, w1_ref, w3_ref, w2_ref, g_ref, b_ref,
                o_ref, acc_ref, comb_ref, hb_ref):
    e = pl.program_id(1)

    @pl.when(e == 0)
    def _():
        h = h_ref[...]
        hb_ref[...] = h.astype(BF16)
        comb_ref[...] = _router(h, wg_ref[...], bg_ref[...], we_ref[...], be_ref[...])
        acc_ref[...] = jnp.zeros_like(acc_ref)

    hb = hb_ref[...]
    a = _dot(hb, w1_ref[0])
    b = _dot(hb, w3_ref[0])
    hid = (a * jax.nn.sigmoid(a) * b).astype(BF16)
    y = _dot(hid, w2_ref[0])
    comb = comb_ref[...]
    lane_e = lax.broadcasted_iota(jnp.int32, comb.shape, 1)
    c = jnp.sum(jnp.where(lane_e == e, comb, 0.0), axis=-1, keepdims=True)
    acc_ref[...] += c * y

    @pl.when(e == N_EXPERTS - 1)
    def _():
        o_ref[...] = _layer_norm(DN_ALPHA * h_ref[...] + acc_ref[...], g_ref[...], b_ref[...])


def _moe(h2, wg, bg, we, be, w1, w3, w2, g3, b3, tm=1024):
    T = h2.shape[0]
    return pl.pallas_call(
        _moe_kernel,
        out_shape=jax.ShapeDtypeStruct((T, D_MODEL), F32),
        grid=(T // tm, N_EXPERTS),
        in_specs=[
            pl.BlockSpec((tm, D_MODEL), lambda i, e: (i, 0)),
            _const_spec(wg.shape), _const_spec(bg.shape), _const_spec(we.shape), _const_spec(be.shape),
            pl.BlockSpec((1, D_MODEL, EXPERT_FF), lambda i, e: (e, 0, 0)),
            pl.BlockSpec((1, D_MODEL, EXPERT_FF), lambda i, e: (e, 0, 0)),
            pl.BlockSpec((1, EXPERT_FF, D_MODEL), lambda i, e: (e, 0, 0)),
            _const_spec((1, D_MODEL)), _const_spec((1, D_MODEL)),
        ],
        out_specs=pl.BlockSpec((tm, D_MODEL), lambda i, e: (i, 0)),
        scratch_shapes=[pltpu.VMEM((tm, D_MODEL), F32), pltpu.VMEM((tm, N_EXPERTS), F32),
                        pltpu.VMEM((tm, D_MODEL), BF16)],
        compiler_params=_cparams(("parallel", "arbitrary")),
        name="moe",
    )(h2, wg, bg, we, be, w1, w3, w2, g3, b3)


def kernel(x, mem, ln_in_g, ln_in_b, w_in, hg_lb_logits, hg_norm_g, w_branch_a, w_branch_b, w_mix_out,
           ln1_g, ln1_b, xa_wq, xa_wk, xa_wv, xa_wo, ln2_g, ln2_b, router_wg, router_bg, router_we,
           router_be, moe_w1, moe_w3, moe_w2, ln3_g, ln3_b):
    B, S, D = x.shape
    T = B * S
    row = lambda a: a.reshape(1, -1).astype(F32)
    x2 = x.reshape(T, D)
    gin, bin_ = row(ln_in_g), row(ln_in_b)
    assert w_in.shape[0] == DEPTH
    l = 0
    proj = _in_proj(x2, gin, bin_, w_in[l].astype(BF16))
    oa = _hgrn2(proj, hg_lb_logits.astype(F32), row(hg_norm_g[l]), B, S, l)
    ob = _sb_attn(proj, B, S)
    h1 = _mix(x2, gin, bin_, oa, ob, proj, w_branch_a[l].astype(BF16), w_branch_b[l].astype(BF16),
              w_mix_out[l].astype(BF16), row(ln1_g[l]), row(ln1_b[l]))
    kx, vx = _xa_kv(mem, xa_wk[l].astype(BF16), xa_wv[l].astype(BF16))
    h2 = _xattn(h1, kx, vx, xa_wq[l].astype(BF16), xa_wo[l].astype(BF16), row(ln2_g[l]), row(ln2_b[l]), B, S)
    out = _moe(h2, router_wg[l].astype(F32), row(router_bg[l]), router_we[l].astype(F32), row(router_be[l]),
               moe_w1[l].astype(BF16), moe_w3[l].astype(BF16), moe_w2[l].astype(BF16),
               row(ln3_g[l]), row(ln3_b[l]))
    return out.reshape(B, S, D)
```

```python
import functools

import jax
import jax.numpy as jnp
from jax import lax
from jax.experimental import pallas as pl
from jax.experimental.pallas import tpu as pltpu

F32 = jnp.float32
BF16 = jnp.bfloat16

D_MODEL = 1024
HG_HEADS = 4
HG_DIM = 128
HG_W = HG_HEADS * HG_DIM
HG_CHUNK = 64
SB_HEADS = 8
SB_HEAD_DIM = 64
SB_W = SB_HEADS * SB_HEAD_DIM
SB_BLOCK = 128
XA_HEADS = 4
XA_HEAD_DIM = D_MODEL // XA_HEADS
N_GROUPS = 4
EXPERTS_PER_GROUP = 4
N_EXPERTS = N_GROUPS * EXPERTS_PER_GROUP
EXPERT_FF = 512
N_IN = HG_W * 4 + SB_W * 3 + D_MODEL * 2
COL_HG = 0
COL_SB = 4 * HG_W
COL_GATE = COL_SB + 3 * SB_W
LN_EPS = 1e-5
RMS_EPS = 1e-6
DEPTH = 1
DN_ALPHA = (2 * DEPTH) ** 0.25

LANES = 128
VMEM_LIMIT = 56 * 1024 * 1024


def _cparams(sem, flags=None):
    return pltpu.CompilerParams(dimension_semantics=sem, vmem_limit_bytes=VMEM_LIMIT, flags=flags)


def _layer_norm(x, g, b):
    mu = jnp.mean(x, axis=-1, keepdims=True)
    xc = x - mu
    var = jnp.mean(xc * xc, axis=-1, keepdims=True)
    return xc * lax.rsqrt(var + LN_EPS) * g + b


def _dot(a, b):
    return jnp.dot(a, b, preferred_element_type=F32)


def _dot_nt(a, b):
    return lax.dot_general(a, b, (((1,), (1,)), ((), ())), preferred_element_type=F32)


def _dot_tn(a, b):
    return lax.dot_general(a, b, (((0,), (0,)), ((), ())), preferred_element_type=F32)


def _split3(a):
    hi = a.astype(BF16)
    r1 = a - hi.astype(F32)
    mid = r1.astype(BF16)
    lo = (r1 - mid.astype(F32)).astype(BF16)
    return hi, mid, lo


def _const_spec(shape):
    nd = len(shape)
    return pl.BlockSpec(shape, lambda *_: (0,) * nd)


def _in_proj_kernel(x_ref, g_ref, b_ref, w_ref, o_ref, *, col_chunk):
    h = _layer_norm(x_ref[...], g_ref[...], b_ref[...]).astype(BF16)
    for c in range(N_IN // col_chunk):
        sl = slice(c * col_chunk, (c + 1) * col_chunk)
        o_ref[:, sl] = _dot(h, w_ref[:, sl]).astype(BF16)


def _in_proj(x2, g, b, w_bf, tm=512, col_chunk=512):
    T = x2.shape[0]
    return pl.pallas_call(
        functools.partial(_in_proj_kernel, col_chunk=col_chunk),
        out_shape=jax.ShapeDtypeStruct((T, N_IN), BF16),
        grid=(T // tm,),
        in_specs=[
            pl.BlockSpec((tm, D_MODEL), lambda i: (i, 0)),
            _const_spec((1, D_MODEL)),
            _const_spec((1, D_MODEL)),
            _const_spec((D_MODEL, N_IN)),
        ],
        out_specs=pl.BlockSpec((tm, N_IN), lambda i: (i, 0)),
        compiler_params=_cparams(("parallel",)),
        name="in_proj",
    )(x2, g, b, w_bf)


def _hgrn2_kernel(lbl_ref, ng_ref, q_ref, f_ref, i_ref, g_ref, o_ref, state_ref, *, layer, tb):
    C = HG_CHUNK
    H2 = C // 2

    @pl.when(pl.program_id(1) == 0)
    def _():
        state_ref[...] = jnp.zeros_like(state_ref)

    lbl = lbl_ref[...]
    ex = jnp.exp(lbl - jnp.max(lbl, axis=0, keepdims=True))
    lb = jnp.sum(ex[: layer + 1], axis=0, keepdims=True) / jnp.sum(ex, axis=0, keepdims=True)

    row = lax.broadcasted_iota(jnp.int32, (C, C), 0)
    col = lax.broadcasted_iota(jnp.int32, (C, C), 1)
    tri = (col <= row).astype(BF16)
    same_half = (row < H2) == (col < H2)
    m_diag = same_half & (col <= row)
    m_cross = (row >= H2) & (col < H2)
    rsel = lax.broadcasted_iota(jnp.int32, (C, HG_DIM), 0) < H2
    ng = ng_ref[...]

    def chunk(c, carry):
        r0 = pl.multiple_of(c * C, C)
        rows = pl.ds(r0, C)
        fl = f_ref[rows, :].astype(F32)
        forget = lb + (1.0 - lb) * jax.nn.sigmoid(fl)
        lf = jnp.log(forget)
        kk_all = 1.0 - forget
        hi, mid, lo = _split3(lf)
        b_all = _dot(tri, hi) + _dot(tri, mid) + _dot(tri, lo)
        for h in range(HG_HEADS):
            ls = slice(h * HG_DIM, (h + 1) * HG_DIM)
            b = b_all[:, ls]
            kk = kk_all[:, ls]
            q = q_ref[rows, ls].astype(F32)
            v = i_ref[rows, ls]
            ref = jnp.where(rsel, b[H2 // 2 - 1:H2 // 2, :], b[H2 + H2 // 2 - 1:H2 + H2 // 2, :])
            d1 = b - ref
            q1 = (q * jnp.exp(d1)).astype(BF16)
            k1 = (kk * jnp.exp(-d1)).astype(BF16)
            e3 = jnp.exp(-jnp.abs(b - b[H2 - 1:H2, :]))
            q3 = (q * e3).astype(BF16)
            k3 = (kk * e3).astype(BF16)
            s1 = _dot_nt(q1, k1)
            s3 = _dot_nt(q3, k3)
            s = jnp.where(m_diag, s1, jnp.where(m_cross, s3, 0.0)).astype(BF16)
            st = state_ref[h]
            qs = (q * jnp.exp(b)).astype(BF16)
            o = _dot(s, v) + _dot_nt(qs, st.astype(BF16))
            b_last = b[C - 1:C, :]
            kd = (kk * jnp.exp(b_last - b)).astype(BF16)
            state_ref[h] = st * jnp.exp(b_last) + _dot_tn(v, kd)
            rms = lax.rsqrt(jnp.mean(o * o, axis=-1, keepdims=True) + RMS_EPS)
            gate = g_ref[rows, ls].astype(F32)
            o_ref[rows, ls] = (o * rms * ng * (gate * jax.nn.sigmoid(gate))).astype(o_ref.dtype)
        return carry

    lax.fori_loop(0, tb // C, chunk, 0)


def _hgrn2(proj, lb_logits, norm_g, B, S, layer, tb=512):
    T = B * S
    nsb = S // tb

    def spec(k):
        return pl.BlockSpec((tb, HG_W), lambda b, s, k=k: (b * nsb + s, COL_HG // HG_W + k))

    return pl.pallas_call(
        functools.partial(_hgrn2_kernel, layer=layer, tb=tb),
        out_shape=jax.ShapeDtypeStruct((T, HG_W), BF16),
        grid=(B, nsb),
        in_specs=[
            _const_spec(lb_logits.shape),
            _const_spec((1, HG_DIM)),
            spec(0), spec(1), spec(2), spec(3),
        ],
        out_specs=pl.BlockSpec((tb, HG_W), lambda b, s: (b * nsb + s, 0)),
        scratch_shapes=[pltpu.VMEM((HG_HEADS, HG_DIM, HG_DIM), F32)],
        compiler_params=_cparams(("parallel", "arbitrary")),
        name="hgrn2",
    )(lb_logits, norm_g, proj, proj, proj, proj)


LOG2E = 1.4426950408889634


def _sb_kernel(q_ref, k_ref, v_ref, o_ref, acc_ref, run_ref, *, npairs, tq):
    i = pl.program_id(1)
    BLK = SB_BLOCK
    W2 = 2 * BLK
    ndiag = tq // BLK
    scale = SB_HEAD_DIM ** -0.5
    lane = lax.broadcasted_iota(jnp.int32, (BLK, LANES), 1)
    head0 = lane < SB_HEAD_DIM
    zrow = lax.broadcasted_iota(jnp.int32, (tq, W2), 0)
    zcol = lax.broadcasted_iota(jnp.int32, (tq, W2), 1) & (BLK - 1)
    wr = lax.broadcasted_iota(jnp.int32, (W2, W2), 0) & (BLK - 1)
    wc = lax.broadcasted_iota(jnp.int32, (W2, W2), 1)
    cw = ((wc >= BLK) | (wr > wc)).astype(BF16)
    qs = q_ref[...] * jnp.asarray(scale, q_ref.dtype)

    def masked_kv(ref, j, p):
        x = ref[pl.ds(pl.multiple_of(j * BLK, BLK), BLK), p * LANES:(p + 1) * LANES]
        zero = jnp.zeros_like(x)
        return jnp.concatenate([jnp.where(head0, x, zero), jnp.where(head0, zero, x)], axis=0)

    def step(js, ds):
        nb = len(js)
        zs = [[_dot_nt(qs[:, p * LANES:(p + 1) * LANES], masked_kv(k_ref, js[b], p)) * LOG2E
               for p in range(npairs)] for b in range(nb)]
        ws = []
        for b in range(nb):
            before = None if ds[b] is None else (zcol + ds[b] * BLK) < zrow
            lbs, hls = [], []
            for p in range(npairs):
                z = zs[b][p]
                l2 = jnp.log2(1.0 + jnp.exp2(-jnp.abs(z)))
                lb = jnp.minimum(z, 0.0) - l2
                lk = lb - z
                if before is not None:
                    lk = jnp.where(before, lk, 0.0)
                hi = lk.astype(BF16)
                lo = (lk - hi.astype(F32)).astype(BF16)
                lbs.append(lb)
                hls.append(jnp.concatenate([hi[:, :BLK], lo[:, :BLK]], axis=1))
                hls.append(jnp.concatenate([hi[:, BLK:], lo[:, BLK:]], axis=1))
            cs = _dot(jnp.concatenate(hls, axis=0), cw)
            wb = []
            for p in range(npairs):
                c0 = cs[(2 * p) * tq:(2 * p + 1) * tq]
                c1 = cs[(2 * p + 1) * tq:(2 * p + 2) * tq]
                cum = jnp.concatenate([c0[:, :BLK], c1[:, :BLK]], axis=1)
                rsum = jnp.concatenate([c0[:, BLK:], c1[:, BLK:]], axis=1)
                run = run_ref[p]
                w = jnp.exp2(lbs[p] + (run + cum))
                if before is not None:
                    w = jnp.where(before, w, 0.0)
                run_ref[p] = run + rsum
                wb.append(w.astype(BF16))
            ws.append(wb)
        for p in range(npairs):
            o = acc_ref[p]
            for b in range(nb):
                o = o + _dot(ws[b][p], masked_kv(v_ref, js[b], p))
            acc_ref[p] = o

    acc_ref[...] = jnp.zeros_like(acc_ref)
    run_ref[...] = jnp.zeros_like(run_ref)
    step([i * ndiag + d for d in reversed(range(ndiag))], list(reversed(range(ndiag))))

    def body(m, c):
        base = (i - 1 - m) * ndiag
        step([base + d for d in reversed(range(ndiag))], [None] * ndiag)
        return c

    lax.fori_loop(0, i, body, 0)
    for p in range(npairs):
        o_ref[:, p * LANES:(p + 1) * LANES] = acc_ref[p].astype(o_ref.dtype)


def _sb_attn(proj, B, S, npairs=SB_W // LANES, tq=256):
    T = B * S
    nq = S // tq
    wblk = npairs * LANES
    ng = SB_W // wblk
    cq = COL_SB // wblk
    ck = (COL_SB + SB_W) // wblk
    cv = (COL_SB + 2 * SB_W) // wblk
    return pl.pallas_call(
        functools.partial(_sb_kernel, npairs=npairs, tq=tq),
        out_shape=jax.ShapeDtypeStruct((T, SB_W), BF16),
        grid=(B * ng, nq),
        in_specs=[
            pl.BlockSpec((tq, wblk), lambda g, i: ((g // ng) * nq + i, cq + g % ng)),
            pl.BlockSpec((S, wblk), lambda g, i: (g // ng, ck + g % ng)),
            pl.BlockSpec((S, wblk), lambda g, i: (g // ng, cv + g % ng)),
        ],
        out_specs=pl.BlockSpec((tq, wblk), lambda g, i: ((g // ng) * nq + i, g % ng)),
        scratch_shapes=[pltpu.VMEM((npairs, tq, LANES), F32),
                        pltpu.VMEM((npairs, tq, 2 * SB_BLOCK), F32)],
        compiler_params=_cparams(("parallel", "arbitrary")),
        name="sb_attn",
    )(proj, proj, proj)


GATE_BLK = 512


def _mix_kernel(x_ref, gin_ref, bin_ref, oa_ref, ob_ref, ga0_ref, ga1_ref, gb0_ref, gb1_ref,
                wa_ref, wb_ref, wm_ref, g1_ref, b1_ref, o_ref):
    h0 = _layer_norm(x_ref[...], gin_ref[...], bin_ref[...])
    oa = oa_ref[...]
    ob = ob_ref[...]
    y = None
    for half, (ga_ref, gb_ref) in enumerate(((ga0_ref, gb0_ref), (ga1_ref, gb1_ref))):
        cs = slice(half * GATE_BLK, (half + 1) * GATE_BLK)
        merged = (jax.nn.sigmoid(ga_ref[...].astype(F32)) * _dot(oa, wa_ref[:, cs])
                  + jax.nn.sigmoid(gb_ref[...].astype(F32)) * _dot(ob, wb_ref[:, cs]))
        part = _dot(merged.astype(BF16), wm_ref[cs, :])
        y = part if y is None else y + part
    o_ref[...] = _layer_norm(DN_ALPHA * h0 + y, g1_ref[...], b1_ref[...])


def _mix(x2, gin, bin_, oa, ob, proj, wa, wb, wm, g1, b1, tm=512):
    T = x2.shape[0]
    c0 = COL_GATE // GATE_BLK

    def gate_spec(k):
        return pl.BlockSpec((tm, GATE_BLK), lambda i, k=k: (i, c0 + k))

    return pl.pallas_call(
        _mix_kernel,
        out_shape=jax.ShapeDtypeStruct((T, D_MODEL), F32),
        grid=(T // tm,),
        in_specs=[
            pl.BlockSpec((tm, D_MODEL), lambda i: (i, 0)),
            _const_spec((1, D_MODEL)), _const_spec((1, D_MODEL)),
            pl.BlockSpec((tm, HG_W), lambda i: (i, 0)),
            pl.BlockSpec((tm, SB_W), lambda i: (i, 0)),
            gate_spec(0), gate_spec(1), gate_spec(2), gate_spec(3),
            _const_spec((HG_W, D_MODEL)), _const_spec((SB_W, D_MODEL)), _const_spec((D_MODEL, D_MODEL)),
            _const_spec((1, D_MODEL)), _const_spec((1, D_MODEL)),
        ],
        out_specs=pl.BlockSpec((tm, D_MODEL), lambda i: (i, 0)),
        compiler_params=_cparams(("parallel",)),
        name="mix",
    )(x2, gin, bin_, oa, ob, proj, proj, proj, proj, wa, wb, wm, g1, b1)


def _xa_kv_kernel(m_ref, wk_ref, wv_ref, k_ref, v_ref):
    m = m_ref[0].astype(BF16)
    k_ref[0] = _dot(m, wk_ref[...]).astype(BF16)
    v_ref[0] = _dot(m, wv_ref[...]).astype(BF16)


def _xa_kv(mem, wk, wv):
    B, M, _ = mem.shape
    W = wk.shape[1]
    return pl.pallas_call(
        _xa_kv_kernel,
        out_shape=(jax.ShapeDtypeStruct((B, M, W), BF16), jax.ShapeDtypeStruct((B, M, W), BF16)),
        grid=(B,),
        in_specs=[pl.BlockSpec((1, M, D_MODEL), lambda b: (b, 0, 0)),
                  _const_spec(wk.shape), _const_spec(wv.shape)],
        out_specs=(pl.BlockSpec((1, M, W), lambda b: (b, 0, 0)),
                   pl.BlockSpec((1, M, W), lambda b: (b, 0, 0))),
        compiler_params=_cparams(("parallel",)),
        name="xa_kv",
    )(mem, wk, wv)


def _xattn_kernel(h_ref, k_ref, v_ref, wq_ref, wo_ref, g_ref, b_ref, o_ref):
    h = h_ref[...]
    q = _dot(h.astype(BF16), wq_ref[...])
    scale = XA_HEAD_DIM ** -0.5
    outs = []
    for hd in range(XA_HEADS):
        ls = slice(hd * XA_HEAD_DIM, (hd + 1) * XA_HEAD_DIM)
        s = _dot_nt(q[:, ls].astype(BF16), k_ref[0, :, ls]) * scale
        s = s - jnp.max(s, axis=-1, keepdims=True)
        p = jnp.exp(s)
        p = p / jnp.sum(p, axis=-1, keepdims=True)
        outs.append(_dot(p.astype(BF16), v_ref[0, :, ls]).astype(BF16))
    o = jnp.concatenate(outs, axis=-1)
    y = _dot(o, wo_ref[...])
    o_ref[...] = _layer_norm(DN_ALPHA * h + y, g_ref[...], b_ref[...])


def _xattn(h1, kx, vx, wq, wo, g2, b2, B, S, tm=512):
    T = B * S
    M = kx.shape[1]
    W = kx.shape[2]
    nsb = S // tm
    return pl.pallas_call(
        _xattn_kernel,
        out_shape=jax.ShapeDtypeStruct((T, D_MODEL), F32),
        grid=(B, nsb),
        in_specs=[
            pl.BlockSpec((tm, D_MODEL), lambda b, s: (b * nsb + s, 0)),
            pl.BlockSpec((1, M, W), lambda b, s: (b, 0, 0)),
            pl.BlockSpec((1, M, W), lambda b, s: (b, 0, 0)),
            _const_spec(wq.shape), _const_spec(wo.shape),
            _const_spec((1, D_MODEL)), _const_spec((1, D_MODEL)),
        ],
        out_specs=pl.BlockSpec((tm, D_MODEL), lambda b, s: (b * nsb + s, 0)),
        compiler_params=_cparams(("parallel", "arbitrary")),
        name="xattn",
    )(h1, kx, vx, wq, wo, g2, b2)


MOE_TM = 1024
SEG_ALIGN = 16
MOE_M1 = 160
MOE_M2 = 64
MOE_PCH = 256
MOE_QCH = 512
SUBLANES = 8


def _moe_slots(tm):
    return -(-(2 * tm + (N_EXPERTS - 1) * SEG_ALIGN + MOE_M1 + MOE_M2) // MOE_QCH) * MOE_QCH

GROUP_ROWS = 16


def _hdot(a, b):
    return jnp.dot(a, b, preferred_element_type=F32, precision=lax.Precision.HIGHEST)


def _split2(a):
    hi = a.astype(BF16)
    return hi, (a - hi.astype(F32)).astype(BF16)


def _first_argmax0(vals, row, height):
    m = jnp.max(vals, axis=0, keepdims=True)
    idx = jnp.min(jnp.where(vals == m, row, height), axis=0, keepdims=True)
    return m, idx


def _select_rows(shape, rows):
    r = lax.broadcasted_iota(jnp.int32, shape, 0)
    out = jnp.zeros(shape, F32)
    for k, v in enumerate(rows):
        out = jnp.where(r == k, v, out)
    return out


def _route_kernel(h_ref, wt_ref, bg_ref, be_ref, rinfo_ref, cinfo_ref, meta_ref):
    tm = h_ref.shape[0]
    h = h_ref[...]
    neg = jnp.float32(-jnp.inf)
    rg = lax.broadcasted_iota(jnp.int32, (GROUP_ROWS, tm), 0)
    re = lax.broadcasted_iota(jnp.int32, (N_EXPERTS, tm), 0)
    nw = GROUP_ROWS + N_EXPERTS
    hh, hl = _split2(h)
    wh, wl = _split2(wt_ref[...])
    lg = _dot_nt(jnp.concatenate([wh, wl], axis=0), hh)
    lg = lg[:nw] + (lg[nw:] + _dot_nt(wh, hl))
    gl = jnp.where(rg < N_GROUPS, lg[:GROUP_ROWS] + bg_ref[...], neg)
    el = lg[GROUP_ROWS:] + be_ref[...]
    gm, g_idx = _first_argmax0(gl, rg, GROUP_ROWS)
    g_top = 1.0 / jnp.sum(jnp.exp(gl - gm), axis=0, keepdims=True)
    ml = jnp.where(re // EXPERTS_PER_GROUP == g_idx, el, neg)
    m1, i1 = _first_argmax0(ml, re, N_EXPERTS)
    ml2 = jnp.where(re == i1, neg, ml)
    m2, i2 = _first_argmax0(ml2, re, N_EXPERTS)
    e2 = jnp.exp(m2 - m1)
    gate0 = g_top / (1.0 + e2)
    gate1 = g_top * e2 / (1.0 + e2)

    oh0 = jnp.where(re == i1, 1.0, 0.0)
    oh1 = jnp.where(re == i2, 1.0, 0.0)
    cnt0 = jnp.sum(oh0, axis=1, keepdims=True)
    cnt1 = jnp.sum(oh1, axis=1, keepdims=True)
    cntp = jnp.ceil((cnt0 + cnt1) * (1.0 / SEG_ALIGN)) * SEG_ALIGN
    er = lax.broadcasted_iota(jnp.int32, (N_EXPERTS, N_EXPERTS), 0)
    ec = lax.broadcasted_iota(jnp.int32, (N_EXPERTS, N_EXPERTS), 1)
    seg = _hdot(jnp.where(ec < er, 1.0, 0.0), jnp.broadcast_to(cntp, (N_EXPERTS, LANES)))[:, 0:1]
    tr = lax.broadcasted_iota(jnp.int32, (tm, tm), 0)
    tc = lax.broadcasted_iota(jnp.int32, (tm, tm), 1)
    earlier = jnp.where(tr < tc, 1.0, 0.0).astype(BF16)
    cum = _dot(jnp.concatenate([oh0, oh1], axis=0).astype(BF16), earlier)
    slot0 = jnp.sum(oh0 * (seg + cum[:N_EXPERTS]), axis=0, keepdims=True)
    slot1 = jnp.sum(oh1 * (seg + cnt0 + cum[N_EXPERTS:]), axis=0, keepdims=True)

    info = (slot0, slot1, gate0, gate1)
    rinfo_ref[0] = _select_rows((SUBLANES, tm), info)
    cinfo_ref[...] = _select_rows((LANES, tm), info).T

    ohs = jnp.concatenate([oh0 + oh1, jnp.zeros((LANES - N_EXPERTS, tm), F32)], axis=0).astype(BF16)
    cnt_row = _dot_nt(jnp.ones((SUBLANES, tm), BF16), ohs)
    cntp_row = jnp.ceil(cnt_row * (1.0 / SEG_ALIGN)) * SEG_ALIGN
    lr = lax.broadcasted_iota(jnp.int32, (LANES, LANES), 0)
    lc = lax.broadcasted_iota(jnp.int32, (LANES, LANES), 1)
    seg_row = _hdot(cntp_row, jnp.where(lr < lc, 1.0, 0.0))
    r8 = lax.broadcasted_iota(jnp.int32, (SUBLANES, LANES), 0)
    meta_ref[0] = jnp.where(r8 == 0, seg_row, jnp.where(r8 == 1, cnt_row, 0.0)).astype(jnp.int32)


def _route(h2, wt, bg, be, tm=MOE_TM):
    T = h2.shape[0]
    nt = T // tm
    return pl.pallas_call(
        _route_kernel,
        out_shape=(jax.ShapeDtypeStruct((nt, SUBLANES, tm), F32),
                   jax.ShapeDtypeStruct((T, LANES), F32),
                   jax.ShapeDtypeStruct((nt, SUBLANES, LANES), jnp.int32)),
        grid=(nt,),
        in_specs=[pl.BlockSpec((tm, D_MODEL), lambda i: (i, 0)),
                  _const_spec(wt.shape), _const_spec(bg.shape), _const_spec(be.shape)],
        out_specs=(pl.BlockSpec((1, SUBLANES, tm), lambda i: (i, 0, 0)),
                   pl.BlockSpec((tm, LANES), lambda i: (i, 0)),
                   pl.BlockSpec((1, SUBLANES, LANES), lambda i: (i, 0, 0))),
        compiler_params=_cparams(("parallel",)),
        name="moe_route",
    )(h2, wt, bg, be)


def _moe_kernel(meta_ref, h_ref, rinfo_ref, cinfo_ref, w1_ref, w3_ref, w2_ref, g_ref, b_ref,
                o_ref, xs_ref, ys_ref):
    t = pl.program_id(0)
    e = pl.program_id(1)
    tm = h_ref.shape[0]
    slots = xs_ref.shape[0]

    @pl.when(e == 0)
    def _():
        hb = h_ref[...].astype(BF16)
        info = rinfo_ref[0]
        slot0 = info[0:1].astype(jnp.int32)
        slot1 = info[1:2].astype(jnp.int32)
        for c in range(slots // MOE_PCH):
            s = lax.broadcasted_iota(jnp.int32, (MOE_PCH, tm), 0) + c * MOE_PCH
            pick = jnp.where((s == slot0) | (s == slot1), 1.0, 0.0).astype(BF16)
            xs_ref[c * MOE_PCH:(c + 1) * MOE_PCH, :] = _dot(pick, hb).astype(BF16)
        ys_ref[...] = jnp.zeros_like(ys_ref)

    start = pl.multiple_of(meta_ref[(2 * t) * N_EXPERTS + e], SEG_ALIGN)
    cnt = meta_ref[(2 * t + 1) * N_EXPERTS + e]

    def ffn(r0, m):
        x = xs_ref[pl.ds(r0, m), :]
        a = _dot(x, w1_ref[0])
        b = _dot(x, w3_ref[0])
        hid = (a * jax.nn.sigmoid(a) * b).astype(BF16)
        ys_ref[pl.ds(r0, m), :] = _dot(hid, w2_ref[0]).astype(BF16)

    @pl.when(cnt > 0)
    def _():
        ffn(start, MOE_M1)

        def extra(k, c):
            ffn(pl.multiple_of(start + MOE_M1 + k * MOE_M2, SEG_ALIGN), MOE_M2)
            return c

        lax.fori_loop(0, (jnp.maximum(cnt - MOE_M1, 0) + MOE_M2 - 1) // MOE_M2, extra, 0)

    @pl.when(e == N_EXPERTS - 1)
    def _():
        ci = cinfo_ref[...]
        slot0 = ci[:, 0:1].astype(jnp.int32)
        slot1 = ci[:, 1:2].astype(jnp.int32)
        gate0 = ci[:, 2:3]
        gate1 = ci[:, 3:4]
        y = None
        for c in range(slots // MOE_QCH):
            s = lax.broadcasted_iota(jnp.int32, (tm, MOE_QCH), 1) + c * MOE_QCH
            q = (jnp.where(s == slot0, gate0, 0.0) + jnp.where(s == slot1, gate1, 0.0)).astype(BF16)
            part = _dot(q, ys_ref[c * MOE_QCH:(c + 1) * MOE_QCH, :])
            y = part if y is None else y + part
        o_ref[...] = _layer_norm(DN_ALPHA * h_ref[...] + y, g_ref[...], b_ref[...])


def _moe(h2, rinfo, cinfo, meta, w1, w3, w2, g3, b3, tm=MOE_TM):
    T = h2.shape[0]
    return pl.pallas_call(
        _moe_kernel,
        out_shape=jax.ShapeDtypeStruct((T, D_MODEL), F32),
        grid_spec=pltpu.PrefetchScalarGridSpec(
            num_scalar_prefetch=1,
            grid=(T // tm, N_EXPERTS),
            in_specs=[
                pl.BlockSpec((tm, D_MODEL), lambda i, e, m: (i, 0)),
                pl.BlockSpec((1, SUBLANES, tm), lambda i, e, m: (i, 0, 0)),
                pl.BlockSpec((tm, LANES), lambda i, e, m: (i, 0)),
                pl.BlockSpec((1, D_MODEL, EXPERT_FF), lambda i, e, m: (e, 0, 0)),
                pl.BlockSpec((1, D_MODEL, EXPERT_FF), lambda i, e, m: (e, 0, 0)),
                pl.BlockSpec((1, EXPERT_FF, D_MODEL), lambda i, e, m: (e, 0, 0)),
                pl.BlockSpec((1, D_MODEL), lambda i, e, m: (0, 0)),
                pl.BlockSpec((1, D_MODEL), lambda i, e, m: (0, 0)),
            ],
            out_specs=pl.BlockSpec((tm, D_MODEL), lambda i, e, m: (i, 0)),
            scratch_shapes=[pltpu.VMEM((_moe_slots(tm), D_MODEL), BF16),
                            pltpu.VMEM((_moe_slots(tm), D_MODEL), BF16)],
        ),
        compiler_params=_cparams(("parallel", "arbitrary")),
        name="moe",
    )(meta, h2, rinfo, cinfo, w1, w3, w2, g3, b3)


def _moe_layer(h2, wg, bg, we, be, w1, w3, w2, g3, b3, tm=MOE_TM):
    pad = GROUP_ROWS - N_GROUPS
    wt = jnp.concatenate([jnp.pad(wg.astype(F32).T, ((0, pad), (0, 0))), we.astype(F32).T], axis=0)
    bgc = jnp.pad(bg.astype(F32).reshape(-1, 1), ((0, pad), (0, 0)))
    rinfo, cinfo, meta = _route(h2, wt, bgc, be.astype(F32).reshape(-1, 1), tm=tm)
    meta = meta[:, :2, :N_EXPERTS].reshape(-1)
    return _moe(h2, rinfo, cinfo, meta, w1, w3, w2, g3, b3, tm=tm)


def kernel(x, mem, ln_in_g, ln_in_b, w_in, hg_lb_logits, hg_norm_g, w_branch_a, w_branch_b, w_mix_out,
           ln1_g, ln1_b, xa_wq, xa_wk, xa_wv, xa_wo, ln2_g, ln2_b, router_wg, router_bg, router_we,
           router_be, moe_w1, moe_w3, moe_w2, ln3_g, ln3_b):
    B, S, D = x.shape
    T = B * S
    row = lambda a: a.reshape(1, -1).astype(F32)
    x2 = x.reshape(T, D)
    gin, bin_ = row(ln_in_g), row(ln_in_b)
    assert w_in.shape[0] == DEPTH
    l = 0
    proj = _in_proj(x2, gin, bin_, w_in[l].astype(BF16))
    oa = _hgrn2(proj, hg_lb_logits.astype(F32), row(hg_norm_g[l]), B, S, l)
    ob = _sb_attn(proj, B, S)
    h1 = _mix(x2, gin, bin_, oa, ob, proj, w_branch_a[l].astype(BF16), w_branch_b[l].astype(BF16),
              w_mix_out[l].astype(BF16), row(ln1_g[l]), row(ln1_b[l]))
    kx, vx = _xa_kv(mem, xa_wk[l].astype(BF16), xa_wv[l].astype(BF16))
    h2 = _xattn(h1, kx, vx, xa_wq[l].astype(BF16), xa_wo[l].astype(BF16), row(ln2_g[l]), row(ln2_b[l]), B, S)
    out = _moe_layer(h2, router_wg[l], router_bg[l], router_we[l], router_be[l],
                     moe_w1[l].astype(BF16), moe_w3[l].astype(BF16), moe_w2[l].astype(BF16),
                     row(ln3_g[l]), row(ln3_b[l]))
    return out.reshape(B, S, D)
```

```python
import functools

import jax
import jax.numpy as jnp
from jax import lax
from jax.experimental import pallas as pl
from jax.experimental.pallas import tpu as pltpu

F32 = jnp.float32
BF16 = jnp.bfloat16

D_MODEL = 1024
HG_HEADS = 4
HG_DIM = 128
HG_W = HG_HEADS * HG_DIM
HG_CHUNK = 64
SB_HEADS = 8
SB_HEAD_DIM = 64
SB_W = SB_HEADS * SB_HEAD_DIM
SB_BLOCK = 128
XA_HEADS = 4
XA_HEAD_DIM = D_MODEL // XA_HEADS
N_GROUPS = 4
EXPERTS_PER_GROUP = 4
N_EXPERTS = N_GROUPS * EXPERTS_PER_GROUP
EXPERT_FF = 512
N_IN = HG_W * 4 + SB_W * 3 + D_MODEL * 2
COL_HG = 0
COL_SB = 4 * HG_W
COL_GATE = COL_SB + 3 * SB_W
LN_EPS = 1e-5
RMS_EPS = 1e-6
DEPTH = 1
DN_ALPHA = (2 * DEPTH) ** 0.25

LANES = 128
VMEM_LIMIT = 56 * 1024 * 1024


def _cparams(sem, flags=None):
    return pltpu.CompilerParams(dimension_semantics=sem, vmem_limit_bytes=VMEM_LIMIT, flags=flags)


def _layer_norm(x, g, b):
    mu = jnp.mean(x, axis=-1, keepdims=True)
    xc = x - mu
    var = jnp.mean(xc * xc, axis=-1, keepdims=True)
    return xc * lax.rsqrt(var + LN_EPS) * g + b


def _dot(a, b):
    return jnp.dot(a, b, preferred_element_type=F32)


def _dot_nt(a, b):
    return lax.dot_general(a, b, (((1,), (1,)), ((), ())), preferred_element_type=F32)


def _dot_tn(a, b):
    return lax.dot_general(a, b, (((0,), (0,)), ((), ())), preferred_element_type=F32)


def _split3(a):
    hi = a.astype(BF16)
    r1 = a - hi.astype(F32)
    mid = r1.astype(BF16)
    lo = (r1 - mid.astype(F32)).astype(BF16)
    return hi, mid, lo


def _const_spec(shape):
    nd = len(shape)
    return pl.BlockSpec(shape, lambda *_: (0,) * nd)


def _in_proj_kernel(x_ref, g_ref, b_ref, w_ref, o_ref, *, col_chunk):
    h = _layer_norm(x_ref[...], g_ref[...], b_ref[...]).astype(BF16)
    for c in range(N_IN // col_chunk):
        sl = slice(c * col_chunk, (c + 1) * col_chunk)
        o_ref[:, sl] = _dot(h, w_ref[:, sl]).astype(BF16)


def _in_proj(x2, g, b, w_bf, tm=512, col_chunk=512):
    T = x2.shape[0]
    return pl.pallas_call(
        functools.partial(_in_proj_kernel, col_chunk=col_chunk),
        out_shape=jax.ShapeDtypeStruct((T, N_IN), BF16),
        grid=(T // tm,),
        in_specs=[
            pl.BlockSpec((tm, D_MODEL), lambda i: (i, 0)),
            _const_spec((1, D_MODEL)),
            _const_spec((1, D_MODEL)),
            _const_spec((D_MODEL, N_IN)),
        ],
        out_specs=pl.BlockSpec((tm, N_IN), lambda i: (i, 0)),
        compiler_params=_cparams(("parallel",)),
        name="in_proj",
    )(x2, g, b, w_bf)


def _hgrn2_kernel(lbl_ref, ng_ref, q_ref, f_ref, i_ref, g_ref, o_ref, state_ref, *, layer, tb):
    C = HG_CHUNK
    H2 = C // 2

    @pl.when(pl.program_id(1) == 0)
    def _():
        state_ref[...] = jnp.zeros_like(state_ref)

    lbl = lbl_ref[...]
    ex = jnp.exp(lbl - jnp.max(lbl, axis=0, keepdims=True))
    lb = jnp.sum(ex[: layer + 1], axis=0, keepdims=True) / jnp.sum(ex, axis=0, keepdims=True)

    row = lax.broadcasted_iota(jnp.int32, (C, C), 0)
    col = lax.broadcasted_iota(jnp.int32, (C, C), 1)
    tri = (col <= row).astype(BF16)
    same_half = (row < H2) == (col < H2)
    m_diag = same_half & (col <= row)
    m_cross = (row >= H2) & (col < H2)
    rsel = lax.broadcasted_iota(jnp.int32, (C, HG_DIM), 0) < H2
    ng = ng_ref[...]

    def chunk(c, carry):
        r0 = pl.multiple_of(c * C, C)
        rows = pl.ds(r0, C)
        fl = f_ref[rows, :].astype(F32)
        forget = lb + (1.0 - lb) * jax.nn.sigmoid(fl)
        lf = jnp.log(forget)
        kk_all = 1.0 - forget
        hi, mid, lo = _split3(lf)
        b_all = _dot(tri, hi) + _dot(tri, mid) + _dot(tri, lo)
        for h in range(HG_HEADS):
            ls = slice(h * HG_DIM, (h + 1) * HG_DIM)
            b = b_all[:, ls]
            kk = kk_all[:, ls]
            q = q_ref[rows, ls].astype(F32)
            v = i_ref[rows, ls]
            ref = jnp.where(rsel, b[H2 // 2 - 1:H2 // 2, :], b[H2 + H2 // 2 - 1:H2 + H2 // 2, :])
            d1 = b - ref
            q1 = (q * jnp.exp(d1)).astype(BF16)
            k1 = (kk * jnp.exp(-d1)).astype(BF16)
            e3 = jnp.exp(-jnp.abs(b - b[H2 - 1:H2, :]))
            q3 = (q * e3).astype(BF16)
            k3 = (kk * e3).astype(BF16)
            s1 = _dot_nt(q1, k1)
            s3 = _dot_nt(q3, k3)
            s = jnp.where(m_diag, s1, jnp.where(m_cross, s3, 0.0)).astype(BF16)
            st = state_ref[h]
            qs = (q * jnp.exp(b)).astype(BF16)
            o = _dot(s, v) + _dot_nt(qs, st.astype(BF16))
            b_last = b[C - 1:C, :]
            kd = (kk * jnp.exp(b_last - b)).astype(BF16)
            state_ref[h] = st * jnp.exp(b_last) + _dot_tn(v, kd)
            rms = lax.rsqrt(jnp.mean(o * o, axis=-1, keepdims=True) + RMS_EPS)
            gate = g_ref[rows, ls].astype(F32)
            o_ref[rows, ls] = (o * rms * ng * (gate * jax.nn.sigmoid(gate))).astype(o_ref.dtype)
        return carry

    lax.fori_loop(0, tb // C, chunk, 0)


def _hgrn2(proj, lb_logits, norm_g, B, S, layer, tb=512):
    T = B * S
    nsb = S // tb

    def spec(k):
        return pl.BlockSpec((tb, HG_W), lambda b, s, k=k: (b * nsb + s, COL_HG // HG_W + k))

    return pl.pallas_call(
        functools.partial(_hgrn2_kernel, layer=layer, tb=tb),
        out_shape=jax.ShapeDtypeStruct((T, HG_W), BF16),
        grid=(B, nsb),
        in_specs=[
            _const_spec(lb_logits.shape),
            _const_spec((1, HG_DIM)),
            spec(0), spec(1), spec(2), spec(3),
        ],
        out_specs=pl.BlockSpec((tb, HG_W), lambda b, s: (b * nsb + s, 0)),
        scratch_shapes=[pltpu.VMEM((HG_HEADS, HG_DIM, HG_DIM), F32)],
        compiler_params=_cparams(("parallel", "arbitrary")),
        name="hgrn2",
    )(lb_logits, norm_g, proj, proj, proj, proj)


LOG2E = 1.4426950408889634


def _sb_kernel(q_ref, k_ref, v_ref, o_ref, acc_ref, run_ref, *, npairs, tq):
    i = pl.program_id(1)
    BLK = SB_BLOCK
    W2 = 2 * BLK
    ndiag = tq // BLK
    scale = SB_HEAD_DIM ** -0.5
    lane = lax.broadcasted_iota(jnp.int32, (BLK, LANES), 1)
    head0 = lane < SB_HEAD_DIM
    zrow = lax.broadcasted_iota(jnp.int32, (tq, W2), 0)
    zcol = lax.broadcasted_iota(jnp.int32, (tq, W2), 1) & (BLK - 1)
    wr = lax.broadcasted_iota(jnp.int32, (W2, W2), 0) & (BLK - 1)
    wc = lax.broadcasted_iota(jnp.int32, (W2, W2), 1)
    cw = ((wc >= BLK) | (wr > wc)).astype(BF16)
    qs = q_ref[...] * jnp.asarray(scale, q_ref.dtype)

    def masked_kv(ref, j, p):
        x = ref[pl.ds(pl.multiple_of(j * BLK, BLK), BLK), p * LANES:(p + 1) * LANES]
        zero = jnp.zeros_like(x)
        return jnp.concatenate([jnp.where(head0, x, zero), jnp.where(head0, zero, x)], axis=0)

    def step(js, ds):
        nb = len(js)
        zs = [[_dot_nt(qs[:, p * LANES:(p + 1) * LANES], masked_kv(k_ref, js[b], p)) * LOG2E
               for p in range(npairs)] for b in range(nb)]
        ws = []
        for b in range(nb):
            before = None if ds[b] is None else (zcol + ds[b] * BLK) < zrow
            lbs, hls = [], []
            for p in range(npairs):
                z = zs[b][p]
                l2 = jnp.log2(1.0 + jnp.exp2(-jnp.abs(z)))
                lb = jnp.minimum(z, 0.0) - l2
                lk = lb - z
                if before is not None:
                    lk = jnp.where(before, lk, 0.0)
                hi = lk.astype(BF16)
                lo = (lk - hi.astype(F32)).astype(BF16)
                lbs.append(lb)
                hls.append(jnp.concatenate([hi[:, :BLK], lo[:, :BLK]], axis=1))
                hls.append(jnp.concatenate([hi[:, BLK:], lo[:, BLK:]], axis=1))
            cs = _dot(jnp.concatenate(hls, axis=0), cw)
            wb = []
            for p in range(npairs):
                c0 = cs[(2 * p) * tq:(2 * p + 1) * tq]
                c1 = cs[(2 * p + 1) * tq:(2 * p + 2) * tq]
                cum = jnp.concatenate([c0[:, :BLK], c1[:, :BLK]], axis=1)
                rsum = jnp.concatenate([c0[:, BLK:], c1[:, BLK:]], axis=1)
                run = run_ref[p]
                w = jnp.exp2(lbs[p] + (run + cum))
                if before is not None:
                    w = jnp.where(before, w, 0.0)
                run_ref[p] = run + rsum
                wb.append(w.astype(BF16))
            ws.append(wb)
        for p in range(npairs):
            o = acc_ref[p]
            for b in range(nb):
                o = o + _dot(ws[b][p], masked_kv(v_ref, js[b], p))
            acc_ref[p] = o

    acc_ref[...] = jnp.zeros_like(acc_ref)
    run_ref[...] = jnp.zeros_like(run_ref)
    step([i * ndiag + d for d in reversed(range(ndiag))], list(reversed(range(ndiag))))

    def body(m, c):
        base = (i - 1 - m) * ndiag
        step([base + d for d in reversed(range(ndiag))], [None] * ndiag)
        return c

    lax.fori_loop(0, i, body, 0)
    for p in range(npairs):
        o_ref[:, p * LANES:(p + 1) * LANES] = acc_ref[p].astype(o_ref.dtype)


def _sb_attn(proj, B, S, npairs=SB_W // LANES, tq=256):
    T = B * S
    nq = S // tq
    wblk = npairs * LANES
    ng = SB_W // wblk
    cq = COL_SB // wblk
    ck = (COL_SB + SB_W) // wblk
    cv = (COL_SB + 2 * SB_W) // wblk
    return pl.pallas_call(
        functools.partial(_sb_kernel, npairs=npairs, tq=tq),
        out_shape=jax.ShapeDtypeStruct((T, SB_W), BF16),
        grid=(B * ng, nq),
        in_specs=[
            pl.BlockSpec((tq, wblk), lambda g, i: ((g // ng) * nq + i, cq + g % ng)),
            pl.BlockSpec((S, wblk), lambda g, i: (g // ng, ck + g % ng)),
            pl.BlockSpec((S, wblk), lambda g, i: (g // ng, cv + g % ng)),
        ],
        out_specs=pl.BlockSpec((tq, wblk), lambda g, i: ((g // ng) * nq + i, g % ng)),
        scratch_shapes=[pltpu.VMEM((npairs, tq, LANES), F32),
                        pltpu.VMEM((npairs, tq, 2 * SB_BLOCK), F32)],
        compiler_params=_cparams(("parallel", "arbitrary")),
        name="sb_attn",
    )(proj, proj, proj)


GATE_BLK = 512


def _mix_kernel(x_ref, gin_ref, bin_ref, oa_ref, ob_ref, ga0_ref, ga1_ref, gb0_ref, gb1_ref,
                wa_ref, wb_ref, wm_ref, g1_ref, b1_ref, o_ref):
    h0 = _layer_norm(x_ref[...], gin_ref[...], bin_ref[...])
    oa = oa_ref[...]
    ob = ob_ref[...]
    y = None
    for half, (ga_ref, gb_ref) in enumerate(((ga0_ref, gb0_ref), (ga1_ref, gb1_ref))):
        cs = slice(half * GATE_BLK, (half + 1) * GATE_BLK)
        merged = (jax.nn.sigmoid(ga_ref[...].astype(F32)) * _dot(oa, wa_ref[:, cs])
                  + jax.nn.sigmoid(gb_ref[...].astype(F32)) * _dot(ob, wb_ref[:, cs]))
        part = _dot(merged.astype(BF16), wm_ref[cs, :])
        y = part if y is None else y + part
    o_ref[...] = _layer_norm(DN_ALPHA * h0 + y, g1_ref[...], b1_ref[...])


def _mix(x2, gin, bin_, oa, ob, proj, wa, wb, wm, g1, b1, tm=512):
    T = x2.shape[0]
    c0 = COL_GATE // GATE_BLK

    def gate_spec(k):
        return pl.BlockSpec((tm, GATE_BLK), lambda i, k=k: (i, c0 + k))

    return pl.pallas_call(
        _mix_kernel,
        out_shape=jax.ShapeDtypeStruct((T, D_MODEL), F32),
        grid=(T // tm,),
        in_specs=[
            pl.BlockSpec((tm, D_MODEL), lambda i: (i, 0)),
            _const_spec((1, D_MODEL)), _const_spec((1, D_MODEL)),
            pl.BlockSpec((tm, HG_W), lambda i: (i, 0)),
            pl.BlockSpec((tm, SB_W), lambda i: (i, 0)),
            gate_spec(0), gate_spec(1), gate_spec(2), gate_spec(3),
            _const_spec((HG_W, D_MODEL)), _const_spec((SB_W, D_MODEL)), _const_spec((D_MODEL, D_MODEL)),
            _const_spec((1, D_MODEL)), _const_spec((1, D_MODEL)),
        ],
        out_specs=pl.BlockSpec((tm, D_MODEL), lambda i: (i, 0)),
        compiler_params=_cparams(("parallel",)),
        name="mix",
    )(x2, gin, bin_, oa, ob, proj, proj, proj, proj, wa, wb, wm, g1, b1)


def _xa_kv_kernel(m_ref, wk_ref, wv_ref, k_ref, v_ref):
    m = m_ref[0].astype(BF16)
    k_ref[0] = _dot(m, wk_ref[...]).astype(BF16)
    v_ref[0] = _dot(m, wv_ref[...]).astype(BF16)


def _xa_kv(mem, wk, wv):
    B, M, _ = mem.shape
    W = wk.shape[1]
    return pl.pallas_call(
        _xa_kv_kernel,
        out_shape=(jax.ShapeDtypeStruct((B, M, W), BF16), jax.ShapeDtypeStruct((B, M, W), BF16)),
        grid=(B,),
        in_specs=[pl.BlockSpec((1, M, D_MODEL), lambda b: (b, 0, 0)),
                  _const_spec(wk.shape), _const_spec(wv.shape)],
        out_specs=(pl.BlockSpec((1, M, W), lambda b: (b, 0, 0)),
                   pl.BlockSpec((1, M, W), lambda b: (b, 0, 0))),
        compiler_params=_cparams(("parallel",)),
        name="xa_kv",
    )(mem, wk, wv)


def _xattn_kernel(h_ref, k_ref, v_ref, wq_ref, wo_ref, g_ref, b_ref, o_ref):
    h = h_ref[...]
    q = _dot(h.astype(BF16), wq_ref[...])
    scale = XA_HEAD_DIM ** -0.5
    outs = []
    for hd in range(XA_HEADS):
        ls = slice(hd * XA_HEAD_DIM, (hd + 1) * XA_HEAD_DIM)
        s = _dot_nt(q[:, ls].astype(BF16), k_ref[0, :, ls]) * scale
        s = s - jnp.max(s, axis=-1, keepdims=True)
        p = jnp.exp(s)
        p = p / jnp.sum(p, axis=-1, keepdims=True)
        outs.append(_dot(p.astype(BF16), v_ref[0, :, ls]).astype(BF16))
    o = jnp.concatenate(outs, axis=-1)
    y = _dot(o, wo_ref[...])
    o_ref[...] = _layer_norm(DN_ALPHA * h + y, g_ref[...], b_ref[...])


def _xattn(h1, kx, vx, wq, wo, g2, b2, B, S, tm=512):
    T = B * S
    M = kx.shape[1]
    W = kx.shape[2]
    nsb = S // tm
    return pl.pallas_call(
        _xattn_kernel,
        out_shape=jax.ShapeDtypeStruct((T, D_MODEL), F32),
        grid=(B, nsb),
        in_specs=[
            pl.BlockSpec((tm, D_MODEL), lambda b, s: (b * nsb + s, 0)),
            pl.BlockSpec((1, M, W), lambda b, s: (b, 0, 0)),
            pl.BlockSpec((1, M, W), lambda b, s: (b, 0, 0)),
            _const_spec(wq.shape), _const_spec(wo.shape),
            _const_spec((1, D_MODEL)), _const_spec((1, D_MODEL)),
        ],
        out_specs=pl.BlockSpec((tm, D_MODEL), lambda b, s: (b * nsb + s, 0)),
        compiler_params=_cparams(("parallel", "arbitrary")),
        name="xattn",
    )(h1, kx, vx, wq, wo, g2, b2)


MOE_TM = 1024
MOE_M1 = 160
MOE_M2 = 64
SUBLANES = 8
GROUP_ROWS = 16


def _hdot(a, b):
    return jnp.dot(a, b, preferred_element_type=F32, precision=lax.Precision.HIGHEST)


def _split2(a):
    hi = a.astype(BF16)
    return hi, (a - hi.astype(F32)).astype(BF16)


def _first_argmax0(vals, row, height):
    m = jnp.max(vals, axis=0, keepdims=True)
    idx = jnp.min(jnp.where(vals == m, row, height), axis=0, keepdims=True)
    return m, idx


def _select_rows(shape, rows):
    r = lax.broadcasted_iota(jnp.int32, shape, 0)
    out = jnp.zeros(shape, F32)
    for k, v in enumerate(rows):
        out = jnp.where(r == k, v, out)
    return out


def _route_kernel(h_ref, wt_ref, bg_ref, be_ref, rinfo_ref, cinfo_ref, meta_ref):
    tm = h_ref.shape[0]
    h = h_ref[...]
    neg = jnp.float32(-jnp.inf)
    rg = lax.broadcasted_iota(jnp.int32, (GROUP_ROWS, tm), 0)
    re = lax.broadcasted_iota(jnp.int32, (N_EXPERTS, tm), 0)
    nw = GROUP_ROWS + N_EXPERTS
    hh, hl = _split2(h)
    wh, wl = _split2(wt_ref[...])
    lg = _dot_nt(jnp.concatenate([wh, wl], axis=0), hh)
    lg = lg[:nw] + (lg[nw:] + _dot_nt(wh, hl))
    gl = jnp.where(rg < N_GROUPS, lg[:GROUP_ROWS] + bg_ref[...], neg)
    el = lg[GROUP_ROWS:] + be_ref[...]
    gm, g_idx = _first_argmax0(gl, rg, GROUP_ROWS)
    g_top = 1.0 / jnp.sum(jnp.exp(gl - gm), axis=0, keepdims=True)
    ml = jnp.where(re // EXPERTS_PER_GROUP == g_idx, el, neg)
    m1, i1 = _first_argmax0(ml, re, N_EXPERTS)
    ml2 = jnp.where(re == i1, neg, ml)
    m2, i2 = _first_argmax0(ml2, re, N_EXPERTS)
    e2 = jnp.exp(m2 - m1)
    gate0 = g_top / (1.0 + e2)
    gate1 = g_top * e2 / (1.0 + e2)

    oh0 = jnp.where(re == i1, 1.0, 0.0)
    oh1 = jnp.where(re == i2, 1.0, 0.0)
    cnt0 = jnp.sum(oh0, axis=1, keepdims=True)
    cnt1 = jnp.sum(oh1, axis=1, keepdims=True)
    er =lax.broadcasted_iota(jnp.int32, (N_EXPERTS, N_EXPERTS), 0)
    ec = lax.broadcasted_iota(jnp.int32, (N_EXPERTS, N_EXPERTS), 1)
    seg = _hdot(jnp.where(ec < er, 1.0, 0.0), jnp.broadcast_to(cnt0 + cnt1, (N_EXPERTS, LANES)))[:, 0:1]
    tr = lax.broadcasted_iota(jnp.int32, (tm, tm), 0)
    tc = lax.broadcasted_iota(jnp.int32, (tm, tm), 1)
    earlier = jnp.where(tr < tc, 1.0, 0.0).astype(BF16)
    cum = _dot(jnp.concatenate([oh0, oh1], axis=0).astype(BF16), earlier)
    slot0 = jnp.sum(oh0 * (seg + cum[:N_EXPERTS]), axis=0, keepdims=True)
    slot1 = jnp.sum(oh1 * (seg + cnt0 + cum[N_EXPERTS:]), axis=0, keepdims=True)

    info = (slot0, slot1, gate0, gate1)
    rinfo_ref[0] = _select_rows((SUBLANES, tm), info)
    cinfo_ref[...] = _select_rows((LANES, tm), info).T

    ohs = jnp.concatenate([oh0 + oh1, jnp.zeros((LANES - N_EXPERTS, tm), F32)], axis=0).astype(BF16)
    cnt_row = _dot_nt(jnp.ones((SUBLANES, tm), BF16), ohs)
    lr = lax.broadcasted_iota(jnp.int32, (LANES, LANES), 0)
    lc = lax.broadcasted_iota(jnp.int32, (LANES, LANES), 1)
    seg_row = _hdot(cnt_row, jnp.where(lr < lc, 1.0, 0.0))
    r8 = lax.broadcasted_iota(jnp.int32, (SUBLANES, LANES), 0)
    meta_ref[0] = jnp.where(r8 == 0, seg_row, jnp.where(r8 == 1, cnt_row, 0.0)).astype(jnp.int32)


def _route(h2, wt, bg, be, tm=MOE_TM):
    T = h2.shape[0]
    nt = T // tm
    return pl.pallas_call(
        _route_kernel,
        out_shape=(jax.ShapeDtypeStruct((nt, SUBLANES, tm), F32),
                   jax.ShapeDtypeStruct((T, LANES), F32),
                   jax.ShapeDtypeStruct((nt, SUBLANES, LANES), jnp.int32)),
        grid=(nt,),
        in_specs=[pl.BlockSpec((tm, D_MODEL), lambda i: (i, 0)),
                  _const_spec(wt.shape), _const_spec(bg.shape), _const_spec(be.shape)],
        out_specs=(pl.BlockSpec((1, SUBLANES, tm), lambda i: (i, 0, 0)),
                   pl.BlockSpec((tm, LANES), lambda i: (i, 0)),
                   pl.BlockSpec((1, SUBLANES, LANES), lambda i: (i, 0, 0))),
        compiler_params=_cparams(("parallel",)),
        name="moe_route",
    )(h2, wt, bg, be)


def _moe_kernel(meta_ref, h_ref, rinfo_ref, cinfo_ref, w1_ref, w3_ref, w2_ref, g_ref, b_ref,
                o_ref, hb_ref, acc_ref):
    t = pl.program_id(0)
    e = pl.program_id(1)
    tm = h_ref.shape[0]

    @pl.when(e == 0)
    def _():
        hb_ref[...] = h_ref[...].astype(BF16)
        acc_ref[...] = jnp.zeros_like(acc_ref)

    start = meta_ref[(2 * t) * N_EXPERTS + e]
    end = start + meta_ref[(2 * t + 1) * N_EXPERTS + e]

    def chunk(r0, m):
        info = rinfo_ref[0]
        s = lax.broadcasted_iota(jnp.int32, (m, tm), 0) + r0
        pick = (s == info[0:1].astype(jnp.int32)) | (s == info[1:2].astype(jnp.int32))
        x = _dot(jnp.where(pick, 1.0, 0.0).astype(BF16), hb_ref[...]).astype(BF16)
        a = _dot(x, w1_ref[0])
        b = _dot(x, w3_ref[0])
        hid = (a * jax.nn.sigmoid(a) * b).astype(BF16)
        y = _dot(hid, w2_ref[0]).astype(BF16)
        ci = cinfo_ref[...]
        sc = lax.broadcasted_iota(jnp.int32, (tm, m), 1) + r0
        live = sc < end
        q = (jnp.where((sc == ci[:, 0:1].astype(jnp.int32)) & live, ci[:, 2:3], 0.0)
             + jnp.where((sc == ci[:, 1:2].astype(jnp.int32)) & live, ci[:, 3:4], 0.0))
        acc_ref[...] += _dot(q.astype(BF16), y)

    @pl.when(end > start)
    def _():
        chunk(start, MOE_M1)

        def extra(k, c):
            chunk(start + MOE_M1 + k * MOE_M2, MOE_M2)
            return c

        lax.fori_loop(0, (jnp.maximum(end - start - MOE_M1, 0) + MOE_M2 - 1) // MOE_M2, extra, 0)

    @pl.when(e == N_EXPERTS - 1)
    def _():
        o_ref[...] = _layer_norm(DN_ALPHA * h_ref[...] + acc_ref[...], g_ref[...], b_ref[...])


def _moe(h2, rinfo, cinfo, meta, w1, w3, w2, g3, b3, tm=MOE_TM):
    T = h2.shape[0]
    return pl.pallas_call(
        _moe_kernel,
        out_shape=jax.ShapeDtypeStruct((T, D_MODEL), F32),
        grid_spec=pltpu.PrefetchScalarGridSpec(
            num_scalar_prefetch=1,
            grid=(T // tm, N_EXPERTS),
            in_specs=[
                pl.BlockSpec((tm, D_MODEL), lambda i, e, m: (i, 0)),
                pl.BlockSpec((1, SUBLANES, tm), lambda i, e, m: (i, 0, 0)),
                pl.BlockSpec((tm, LANES), lambda i, e, m: (i, 0)),
                pl.BlockSpec((1, D_MODEL, EXPERT_FF), lambda i, e, m: (e, 0, 0)),
                pl.BlockSpec((1, D_MODEL, EXPERT_FF), lambda i, e, m: (e, 0, 0)),
                pl.BlockSpec((1, EXPERT_FF, D_MODEL), lambda i, e, m: (e, 0, 0)),
                pl.BlockSpec((1, D_MODEL), lambda i, e, m: (0, 0)),
                pl.BlockSpec((1, D_MODEL), lambda i, e, m: (0, 0)),
            ],
            out_specs=pl.BlockSpec((tm, D_MODEL), lambda i, e, m: (i, 0)),
            scratch_shapes=[pltpu.VMEM((tm, D_MODEL), BF16), pltpu.VMEM((tm, D_MODEL), F32)],
        ),
        compiler_params=_cparams(("parallel", "arbitrary")),
        name="moe",
    )(meta, h2, rinfo, cinfo, w1, w3, w2, g3, b3)


def _moe_layer(h2, wg, bg, we, be, w1, w3, w2, g3, b3, tm=MOE_TM):
    pad = GROUP_ROWS - N_GROUPS
    wt = jnp.concatenate([jnp.pad(wg.astype(F32).T, ((0, pad), (0, 0))), we.astype(F32).T], axis=0)
    bgc = jnp.pad(bg.astype(F32).reshape(-1, 1), ((0, pad), (0, 0)))
    rinfo, cinfo, meta = _route(h2, wt, bgc, be.astype(F32).reshape(-1, 1), tm=tm)
    meta = meta[:, :2, :N_EXPERTS].reshape(-1)
    return _moe(h2, rinfo, cinfo, meta, w1, w3, w2, g3, b3, tm=tm)


def kernel(x, mem, ln_in_g, ln_in_b, w_in, hg_lb_logits, hg_norm_g, w_branch_a, w_branch_b, w_mix_out,
           ln1_g, ln1_b, xa_wq, xa_wk, xa_wv, xa_wo, ln2_g, ln2_b, router_wg, router_bg, router_we,
           router_be, moe_w1, moe_w3, moe_w2, ln3_g, ln3_b):
    B, S, D = x.shape
    T = B * S
    row = lambda a: a.reshape(1, -1).astype(F32)
    x2 = x.reshape(T, D)
    gin, bin_ = row(ln_in_g), row(ln_in_b)
    assert w_in.shape[0] == DEPTH
    l = 0
    proj = _in_proj(x2, gin, bin_, w_in[l].astype(BF16))
    oa = _hgrn2(proj, hg_lb_logits.astype(F32), row(hg_norm_g[l]), B, S, l)
    ob = _sb_attn(proj, B, S)
    h1 = _mix(x2, gin, bin_, oa, ob, proj, w_branch_a[l].astype(BF16), w_branch_b[l].astype(BF16),
              w_mix_out[l].astype(BF16), row(ln1_g[l]), row(ln1_b[l]))
    kx, vx = _xa_kv(mem, xa_wk[l].astype(BF16), xa_wv[l].astype(BF16))
    h2 = _xattn(h1, kx, vx, xa_wq[l].astype(BF16), xa_wo[l].astype(BF16), row(ln2_g[l]), row(ln2_b[l]), B, S)
    out = _moe_layer(h2, router_wg[l], router_bg[l], router_we[l], router_be[l],
                     moe_w1[l].astype(BF16), moe_w3[l].astype(BF16), moe_w2[l].astype(BF16),
                     row(ln3_g[l]), row(ln3_b[l]))
    return out.reshape(B, S, D)
```

```python
import functools

import jax
import jax.numpy as jnp
from jax import lax
from jax.experimental import pallas as pl
from jax.experimental.pallas import tpu as pltpu

F32 = jnp.float32
BF16 = jnp.bfloat16

D_MODEL = 1024
HG_HEADS = 4
HG_DIM = 128
HG_W = HG_HEADS * HG_DIM
HG_CHUNK = 64
HG_LOCAL_CHUNKS = 4
SB_HEADS = 8
SB_HEAD_DIM = 64
SB_W = SB_HEADS * SB_HEAD_DIM
SB_BLOCK = 128
XA_HEADS = 4
XA_HEAD_DIM = D_MODEL // XA_HEADS
N_GROUPS = 4
EXPERTS_PER_GROUP = 4
N_EXPERTS = N_GROUPS * EXPERTS_PER_GROUP
EXPERT_FF = 512
N_IN = HG_W * 4 + SB_W * 3 + D_MODEL * 2
COL_HG = 0
COL_SB = 4 * HG_W
COL_GATE = COL_SB + 3 * SB_W
LN_EPS = 1e-5
RMS_EPS = 1e-6
DEPTH = 1
DN_ALPHA = (2 * DEPTH) ** 0.25

LANES = 128
VMEM_LIMIT = 56 * 1024 * 1024


def _cparams(sem, flags=None):
    return pltpu.CompilerParams(dimension_semantics=sem, vmem_limit_bytes=VMEM_LIMIT, flags=flags)


def _layer_norm(x, g, b):
    mu = jnp.mean(x, axis=-1, keepdims=True)
    xc = x - mu
    var = jnp.mean(xc * xc, axis=-1, keepdims=True)
    return xc * lax.rsqrt(var + LN_EPS) * g + b


def _dot(a, b):
    return jnp.dot(a, b, preferred_element_type=F32)


def _dot_nt(a, b):
    return lax.dot_general(a, b, (((1,), (1,)), ((), ())), preferred_element_type=F32)


def _dot_tn(a, b):
    return lax.dot_general(a, b, (((0,), (0,)), ((), ())), preferred_element_type=F32)


def _split3(a):
    hi = a.astype(BF16)
    r1 = a - hi.astype(F32)
    mid = r1.astype(BF16)
    lo = (r1 - mid.astype(F32)).astype(BF16)
    return hi, mid, lo


def _const_spec(shape):
    nd = len(shape)
    return pl.BlockSpec(shape, lambda *_: (0,) * nd)


def _in_proj_kernel(x_ref, g_ref, b_ref, w_ref, o_ref, *, col_chunk):
    h = _layer_norm(x_ref[...], g_ref[...], b_ref[...]).astype(BF16)
    for c in range(N_IN // col_chunk):
        sl = slice(c * col_chunk, (c + 1) * col_chunk)
        o_ref[:, sl] = _dot(h, w_ref[:, sl]).astype(BF16)


def _in_proj(x2, g, b, w_bf, tm=512, col_chunk=512):
    T = x2.shape[0]
    return pl.pallas_call(
        functools.partial(_in_proj_kernel, col_chunk=col_chunk),
        out_shape=jax.ShapeDtypeStruct((T, N_IN), BF16),
        grid=(T // tm,),
        in_specs=[
            pl.BlockSpec((tm, D_MODEL), lambda i: (i, 0)),
            _const_spec((1, D_MODEL)),
            _const_spec((1, D_MODEL)),
            _const_spec((D_MODEL, N_IN)),
        ],
        out_specs=pl.BlockSpec((tm, N_IN), lambda i: (i, 0)),
        compiler_params=_cparams(("parallel",)),
        name="in_proj",
    )(x2, g, b, w_bf)


def _hgrn2_kernel(lbl_ref, ng_ref, q_ref, f_ref, i_ref, g_ref, o_ref,
                  state_ref, intra_ref, qs_ref, dl_ref, u_ref, *, layer, tb):
    C = HG_CHUNK
    H2 = C // 2

    @pl.when(pl.program_id(1) == 0)
    def _():
        state_ref[...] = jnp.zeros_like(state_ref)

    lbl = lbl_ref[...]
    ex = jnp.exp(lbl - jnp.max(lbl, axis=0, keepdims=True))
    lb = jnp.sum(ex[: layer + 1], axis=0, keepdims=True) / jnp.sum(ex, axis=0, keepdims=True)

    R = HG_LOCAL_CHUNKS * C
    row = lax.broadcasted_iota(jnp.int32, (R, R), 0)
    col = lax.broadcasted_iota(jnp.int32, (R, R), 1)
    same_chunk = (row // C) == (col // C)
    tri = (same_chunk & (col <= row)).astype(BF16)
    m_diag = ((row // H2) == (col // H2)) & (col <= row)
    m_cross = same_chunk & ((row % C) >= H2) & ((col % C) < H2)
    rrow = lax.broadcasted_iota(jnp.int32, (R, HG_W), 0)
    ng = ng_ref[...]
    heads = [slice(h * HG_DIM, (h + 1) * HG_DIM) for h in range(HG_HEADS)]

    def row_of_period(x, offset, period):
        out = jnp.broadcast_to(x[offset:offset + 1, :], x.shape)
        for k in range(1, R // period):
            out = jnp.where(rrow >= k * period, x[k * period + offset:k * period + offset + 1, :], out)
        return out

    def local(blk):
        rows = slice(blk * R, (blk + 1) * R)
        forget = lb + (1.0 - lb) * jax.nn.sigmoid(f_ref[rows, :].astype(F32))
        kk = 1.0 - forget
        hi, mid, lo = _split3(jnp.log(forget))
        b = _dot(tri, hi) + _dot(tri, mid) + _dot(tri, lo)
        q = q_ref[rows, :].astype(F32)
        d1 = b - row_of_period(b, H2 // 2 - 1, H2)
        q1 = (q * jnp.exp(d1)).astype(BF16)
        k1 = (kk * jnp.exp(-d1)).astype(BF16)
        e3 = jnp.exp(-jnp.abs(b - row_of_period(b, H2 - 1, C)))
        q3 = (q * e3).astype(BF16)
        k3 = (kk * e3).astype(BF16)
        kd = (kk * jnp.exp(row_of_period(b, C - 1, C) - b)).astype(BF16)
        qs_ref[rows, :] = (q * jnp.exp(b)).astype(BF16)
        for k in range(HG_LOCAL_CHUNKS):
            dl_ref[blk * HG_LOCAL_CHUNKS + k] = jnp.exp(b[k * C + C - 1:(k + 1) * C, :])
        s1 = [_dot_nt(q1[:, ls], k1[:, ls]) for ls in heads]
        s3 = [_dot_nt(q3[:, ls], k3[:, ls]) for ls in heads]
        s = [jnp.where(m_diag, a1, jnp.where(m_cross, a3, 0.0)).astype(BF16) for a1, a3 in zip(s1, s3)]
        for h, ls in enumerate(heads):
            v = i_ref[rows, ls]
            intra_ref[rows, ls] = _dot(s[h], v)
            for k in range(HG_LOCAL_CHUNKS):
                cs = slice(k * C, (k + 1) * C)
                u_ref[blk * HG_LOCAL_CHUNKS + k, h] = _dot_tn(v[cs], kd[cs, ls])

    def carry_state(c, carry):
        rows = pl.ds(pl.multiple_of(c * C, C), C)
        dl = dl_ref[c]
        for h, ls in enumerate(heads):
            st = state_ref[h]
            o = intra_ref[rows, ls] + _dot_nt(qs_ref[rows, ls], st.astype(BF16))
            state_ref[h] = st * dl[:, ls] + u_ref[c, h]
            rms = lax.rsqrt(jnp.mean(o * o, axis=-1, keepdims=True) + RMS_EPS)
            gate = g_ref[rows, ls].astype(F32)
            o_ref[rows, ls] = (o * rms * ng * (gate * jax.nn.sigmoid(gate))).astype(o_ref.dtype)
        return carry

    for blk in range(tb // R):
        local(blk)
    lax.fori_loop(0, tb // C, carry_state, 0, unroll=2)


def _hgrn2(proj, lb_logits, norm_g, B, S, layer, tb=512):
    T = B * S
    nsb = S // tb

    def spec(k):
        return pl.BlockSpec((tb, HG_W), lambda b, s, k=k: (b * nsb + s, COL_HG // HG_W + k))

    return pl.pallas_call(
        functools.partial(_hgrn2_kernel, layer=layer, tb=tb),
        out_shape=jax.ShapeDtypeStruct((T, HG_W), BF16),
        grid=(B, nsb),
        in_specs=[
            _const_spec(lb_logits.shape),
            _const_spec((1, HG_DIM)),
            spec(0), spec(1), spec(2), spec(3),
        ],
        out_specs=pl.BlockSpec((tb, HG_W), lambda b, s: (b * nsb + s, 0)),
        scratch_shapes=[pltpu.VMEM((HG_HEADS, HG_DIM, HG_DIM), F32),
                        pltpu.VMEM((tb, HG_W), F32),
                        pltpu.VMEM((tb, HG_W), BF16),
                        pltpu.VMEM((tb // HG_CHUNK, 1, HG_W), F32),
                        pltpu.VMEM((tb // HG_CHUNK, HG_HEADS, HG_DIM, HG_DIM), F32)],
        compiler_params=_cparams(("parallel", "arbitrary")),
        name="hgrn2",
    )(lb_logits, norm_g, proj, proj, proj, proj)


LOG2E = 1.4426950408889634


def _sb_kernel(q_ref, k_ref, v_ref, o_ref, acc_ref, run_ref, *, npairs, tq):
    i = pl.program_id(1)
    BLK = SB_BLOCK
    W2 = 2 * BLK
    ndiag = tq // BLK
    scale = SB_HEAD_DIM ** -0.5
    lane = lax.broadcasted_iota(jnp.int32, (BLK, LANES), 1)
    head0 = lane < SB_HEAD_DIM
    zrow = lax.broadcasted_iota(jnp.int32, (tq, W2), 0)
    zcol = lax.broadcasted_iota(jnp.int32, (tq, W2), 1) & (BLK - 1)
    wr = lax.broadcasted_iota(jnp.int32, (W2, W2), 0) & (BLK - 1)
    wc = lax.broadcasted_iota(jnp.int32, (W2, W2), 1)
    cw = ((wc >= BLK) | (wr > wc)).astype(BF16)
    qs = q_ref[...] * jnp.asarray(scale, q_ref.dtype)

    def masked_kv(ref, j, p):
        x = ref[pl.ds(pl.multiple_of(j * BLK, BLK), BLK), p * LANES:(p + 1) * LANES]
        zero = jnp.zeros_like(x)
        return jnp.concatenate([jnp.where(head0, x, zero), jnp.where(head0, zero, x)], axis=0)

    def step(js, ds):
        nb = len(js)
        zs = [[_dot_nt(qs[:, p * LANES:(p + 1) * LANES], masked_kv(k_ref, js[b], p)) * LOG2E
               for p in range(npairs)] for b in range(nb)]
        ws = []
        for b in range(nb):
            before = None if ds[b] is None else (zcol + ds[b] * BLK) < zrow
            lbs, hls = [], []
            for p in range(npairs):
                z = zs[b][p]
                l2 = jnp.log2(1.0 + jnp.exp2(-jnp.abs(z)))
                lb = jnp.minimum(z, 0.0) - l2
                lk = lb - z
                if before is not None:
                    lk = jnp.where(before, lk, 0.0)
                hi = lk.astype(BF16)
                lo = (lk - hi.astype(F32)).astype(BF16)
                lbs.append(lb)
                hls.append(jnp.concatenate([hi[:, :BLK], lo[:, :BLK]], axis=1))
                hls.append(jnp.concatenate([hi[:, BLK:], lo[:, BLK:]], axis=1))
            cs = _dot(jnp.concatenate(hls, axis=0), cw)
            wb = []
            for p in range(npairs):
                c0 = cs[(2 * p) * tq:(2 * p + 1) * tq]
                c1 = cs[(2 * p + 1) * tq:(2 * p + 2) * tq]
                cum = jnp.concatenate([c0[:, :BLK], c1[:, :BLK]], axis=1)
                rsum = jnp.concatenate([c0[:, BLK:], c1[:, BLK:]], axis=1)
                run = run_ref[p]
                w = jnp.exp2(lbs[p] + (run + cum))
                if before is not None:
                    w = jnp.where(before, w, 0.0)
                run_ref[p] = run + rsum
                wb.append(w.astype(BF16))
            ws.append(wb)
        for p in range(npairs):
            o = acc_ref[p]
            for b in range(nb):
                o = o + _dot(ws[b][p], masked_kv(v_ref, js[b], p))
            acc_ref[p] = o

    acc_ref[...] = jnp.zeros_like(acc_ref)
    run_ref[...] = jnp.zeros_like(run_ref)
    step([i * ndiag + d for d in reversed(range(ndiag))], list(reversed(range(ndiag))))

    def body(m, c):
        base = (i - 1 - m) * ndiag
        step([base + d for d in reversed(range(ndiag))], [None] * ndiag)
        return c

    lax.fori_loop(0, i, body, 0)
    for p in range(npairs):
        o_ref[:, p * LANES:(p + 1) * LANES] = acc_ref[p].astype(o_ref.dtype)


def _sb_attn(proj, B, S, npairs=SB_W // LANES, tq=256):
    T = B * S
    nq = S // tq
    wblk = npairs * LANES
    ng = SB_W // wblk
    cq = COL_SB // wblk
    ck = (COL_SB + SB_W) // wblk
    cv = (COL_SB + 2 * SB_W) // wblk
    return pl.pallas_call(
        functools.partial(_sb_kernel, npairs=npairs, tq=tq),
        out_shape=jax.ShapeDtypeStruct((T, SB_W), BF16),
        grid=(B * ng, nq),
        in_specs=[
            pl.BlockSpec((tq, wblk), lambda g, i: ((g // ng) * nq + i, cq + g % ng)),
            pl.BlockSpec((S, wblk), lambda g, i: (g // ng, ck + g % ng)),
            pl.BlockSpec((S, wblk), lambda g, i: (g // ng, cv + g % ng)),
        ],
        out_specs=pl.BlockSpec((tq, wblk), lambda g, i: ((g // ng) * nq + i, g % ng)),
        scratch_shapes=[pltpu.VMEM((npairs, tq, LANES), F32),
                        pltpu.VMEM((npairs, tq, 2 * SB_BLOCK), F32)],
        compiler_params=_cparams(("parallel", "arbitrary")),
        name="sb_attn",
    )(proj, proj, proj)


GATE_BLK = 512


def _mix_kernel(x_ref, gin_ref, bin_ref, oa_ref, ob_ref, ga0_ref, ga1_ref, gb0_ref, gb1_ref,
                wa_ref, wb_ref, wm_ref, g1_ref, b1_ref, o_ref):
    h0 = _layer_norm(x_ref[...], gin_ref[...], bin_ref[...])
    oa = oa_ref[...]
    ob = ob_ref[...]
    y = None
    for half, (ga_ref, gb_ref) in enumerate(((ga0_ref, gb0_ref), (ga1_ref, gb1_ref))):
        cs = slice(half * GATE_BLK, (half + 1) * GATE_BLK)
        merged = (jax.nn.sigmoid(ga_ref[...].astype(F32)) * _dot(oa, wa_ref[:, cs])
                  + jax.nn.sigmoid(gb_ref[...].astype(F32)) * _dot(ob, wb_ref[:, cs]))
        part = _dot(merged.astype(BF16), wm_ref[cs, :])
        y = part if y is None else y + part
    o_ref[...] = _layer_norm(DN_ALPHA * h0 + y, g1_ref[...], b1_ref[...])


def _mix(x2, gin, bin_, oa, ob, proj, wa, wb, wm, g1, b1, tm=512):
    T = x2.shape[0]
    c0 = COL_GATE // GATE_BLK

    def gate_spec(k):
        return pl.BlockSpec((tm, GATE_BLK), lambda i, k=k: (i, c0 + k))

    return pl.pallas_call(
        _mix_kernel,
        out_shape=jax.ShapeDtypeStruct((T, D_MODEL), F32),
        grid=(T // tm,),
        in_specs=[
            pl.BlockSpec((tm, D_MODEL), lambda i: (i, 0)),
            _const_spec((1, D_MODEL)), _const_spec((1, D_MODEL)),
            pl.BlockSpec((tm, HG_W), lambda i: (i, 0)),
            pl.BlockSpec((tm, SB_W), lambda i: (i, 0)),
            gate_spec(0), gate_spec(1), gate_spec(2), gate_spec(3),
            _const_spec((HG_W, D_MODEL)), _const_spec((SB_W, D_MODEL)), _const_spec((D_MODEL, D_MODEL)),
            _const_spec((1, D_MODEL)), _const_spec((1, D_MODEL)),
        ],
        out_specs=pl.BlockSpec((tm, D_MODEL), lambda i: (i, 0)),
        compiler_params=_cparams(("parallel",)),
        name="mix",
    )(x2, gin, bin_, oa, ob, proj, proj, proj, proj, wa, wb, wm, g1, b1)


def _xa_kv_kernel(m_ref, wk_ref, wv_ref, k_ref, v_ref):
    m = m_ref[0].astype(BF16)
    k_ref[0] = _dot(m, wk_ref[...]).astype(BF16)
    v_ref[0] = _dot(m, wv_ref[...]).astype(BF16)


def _xa_kv(mem, wk, wv):
    B, M, _ = mem.shape
    W = wk.shape[1]
    return pl.pallas_call(
        _xa_kv_kernel,
        out_shape=(jax.ShapeDtypeStruct((B, M, W), BF16), jax.ShapeDtypeStruct((B, M, W), BF16)),
        grid=(B,),
        in_specs=[pl.BlockSpec((1, M, D_MODEL), lambda b: (b, 0, 0)),
                  _const_spec(wk.shape), _const_spec(wv.shape)],
        out_specs=(pl.BlockSpec((1, M, W), lambda b: (b, 0, 0)),
                   pl.BlockSpec((1, M, W), lambda b: (b, 0, 0))),
        compiler_params=_cparams(("parallel",)),
        name="xa_kv",
    )(mem, wk, wv)


def _xattn_kernel(h_ref, k_ref, v_ref, wq_ref, wo_ref, g_ref, b_ref, o_ref):
    h = h_ref[...]
    q = _dot(h.astype(BF16), wq_ref[...])
    scale = XA_HEAD_DIM ** -0.5
    outs = []
    for hd in range(XA_HEADS):
        ls = slice(hd * XA_HEAD_DIM, (hd + 1) * XA_HEAD_DIM)
        s = _dot_nt(q[:, ls].astype(BF16), k_ref[0, :, ls]) * scale
        s = s - jnp.max(s, axis=-1, keepdims=True)
        p = jnp.exp(s)
        p = p / jnp.sum(p, axis=-1, keepdims=True)
        outs.append(_dot(p.astype(BF16), v_ref[0, :, ls]).astype(BF16))
    o = jnp.concatenate(outs, axis=-1)
    y = _dot(o, wo_ref[...])
    o_ref[...] = _layer_norm(DN_ALPHA * h + y, g_ref[...], b_ref[...])


def _xattn(h1, kx, vx, wq, wo, g2, b2, B, S, tm=512):
    T = B * S
    M = kx.shape[1]
    W = kx.shape[2]
    nsb = S // tm
    return pl.pallas_call(
        _xattn_kernel,
        out_shape=jax.ShapeDtypeStruct((T, D_MODEL), F32),
        grid=(B, nsb),
        in_specs=[
            pl.BlockSpec((tm, D_MODEL), lambda b, s: (b * nsb + s, 0)),
            pl.BlockSpec((1, M, W), lambda b, s: (b, 0, 0)),
            pl.BlockSpec((1, M, W), lambda b, s: (b, 0, 0)),
            _const_spec(wq.shape), _const_spec(wo.shape),
            _const_spec((1, D_MODEL)), _const_spec((1, D_MODEL)),
        ],
        out_specs=pl.BlockSpec((tm, D_MODEL), lambda b, s: (b * nsb + s, 0)),
        compiler_params=_cparams(("parallel", "arbitrary")),
        name="xattn",
    )(h1, kx, vx, wq, wo, g2, b2)


MOE_TM = 1024
MOE_M1 = 160
MOE_M2 = 64
MOE_EPS = 2
SUBLANES = 8
GROUP_ROWS = 16


def _hdot(a, b):
    return jnp.dot(a, b, preferred_element_type=F32, precision=lax.Precision.HIGHEST)


def _split2(a):
    hi = a.astype(BF16)
    return hi, (a - hi.astype(F32)).astype(BF16)


def _first_argmax0(vals, row, height):
    m = jnp.max(vals, axis=0, keepdims=True)
    idx = jnp.min(jnp.where(vals == m, row, height), axis=0, keepdims=True)
    return m, idx


def _select_rows(shape, rows):
    r = lax.broadcasted_iota(jnp.int32, shape, 0)
    out = jnp.zeros(shape, F32)
    for k, v in enumerate(rows):
        out = jnp.where(r == k, v, out)
    return out


def _route_kernel(h_ref, wt_ref, bg_ref, be_ref, rinfo_ref, cinfo_ref, meta_ref):
    tm = h_ref.shape[0]
    h = h_ref[...]
    neg = jnp.float32(-jnp.inf)
    rg = lax.broadcasted_iota(jnp.int32, (GROUP_ROWS, tm), 0)
    re = lax.broadcasted_iota(jnp.int32, (N_EXPERTS, tm), 0)
    nw = GROUP_ROWS + N_EXPERTS
    hh, hl = _split2(h)
    wh, wl = _split2(wt_ref[...])
    lg = _dot_nt(jnp.concatenate([wh, wl], axis=0), hh)
    lg = lg[:nw] + (lg[nw:] + _dot_nt(wh, hl))
    gl = jnp.where(rg < N_GROUPS, lg[:GROUP_ROWS] + bg_ref[...], neg)
    el = lg[GROUP_ROWS:] + be_ref[...]
    gm, g_idx = _first_argmax0(gl, rg, GROUP_ROWS)
    g_top = 1.0 / jnp.sum(jnp.exp(gl - gm), axis=0, keepdims=True)
    ml = jnp.where(re // EXPERTS_PER_GROUP == g_idx, el, neg)
    m1, i1 = _first_argmax0(ml, re, N_EXPERTS)
    ml2 = jnp.where(re == i1, neg, ml)
    m2, i2 = _first_argmax0(ml2, re, N_EXPERTS)
    e2 = jnp.exp(m2 - m1)
    gate0 = g_top / (1.0 + e2)
    gate1 = g_top * e2 / (1.0 + e2)

    oh0 = jnp.where(re == i1, 1.0, 0.0)
    oh1 = jnp.where(re == i2, 1.0, 0.0)
    cnt0 = jnp.sum(oh0, axis=1, keepdims=True)
    cnt1 = jnp.sum(oh1, axis=1, keepdims=True)
    er =lax.broadcasted_iota(jnp.int32, (N_EXPERTS, N_EXPERTS), 0)
    ec = lax.broadcasted_iota(jnp.int32, (N_EXPERTS, N_EXPERTS), 1)
    seg = _hdot(jnp.where(ec < er, 1.0, 0.0), jnp.broadcast_to(cnt0 + cnt1, (N_EXPERTS, LANES)))[:, 0:1]
    tr = lax.broadcasted_iota(jnp.int32, (tm, tm), 0)
    tc = lax.broadcasted_iota(jnp.int32, (tm, tm), 1)
    earlier = jnp.where(tr < tc, 1.0, 0.0).astype(BF16)
    cum = _dot(jnp.concatenate([oh0, oh1], axis=0).astype(BF16), earlier)
    slot0 = jnp.sum(oh0 * (seg + cum[:N_EXPERTS]), axis=0, keepdims=True)
    slot1 = jnp.sum(oh1 * (seg + cnt0 + cum[N_EXPERTS:]), axis=0, keepdims=True)

    info = (slot0, slot1, gate0, gate1)
    rinfo_ref[0] = _select_rows((SUBLANES, tm), info)
    cinfo_ref[...] = _select_rows((LANES, tm), info).T

    ohs = jnp.concatenate([oh0 + oh1, jnp.zeros((LANES - N_EXPERTS, tm), F32)], axis=0).astype(BF16)
    cnt_row = _dot_nt(jnp.ones((SUBLANES, tm), BF16), ohs)
    lr = lax.broadcasted_iota(jnp.int32, (LANES, LANES), 0)
    lc = lax.broadcasted_iota(jnp.int32, (LANES, LANES), 1)
    seg_row = _hdot(cnt_row, jnp.where(lr < lc, 1.0, 0.0))
    r8 = lax.broadcasted_iota(jnp.int32, (SUBLANES, LANES), 0)
    meta_ref[0] = jnp.where(r8 == 0, seg_row, jnp.where(r8 == 1, cnt_row, 0.0)).astype(jnp.int32)


def _route(h2, wt, bg, be, tm=MOE_TM):
    T = h2.shape[0]
    nt = T // tm
    return pl.pallas_call(
        _route_kernel,
        out_shape=(jax.ShapeDtypeStruct((nt, SUBLANES, tm), F32),
                   jax.ShapeDtypeStruct((T, LANES), F32),
                   jax.ShapeDtypeStruct((nt, SUBLANES, LANES), jnp.int32)),
        grid=(nt,),
        in_specs=[pl.BlockSpec((tm, D_MODEL), lambda i: (i, 0)),
                  _const_spec(wt.shape), _const_spec(bg.shape), _const_spec(be.shape)],
        out_specs=(pl.BlockSpec((1, SUBLANES, tm), lambda i: (i, 0, 0)),
                   pl.BlockSpec((tm, LANES), lambda i: (i, 0)),
                   pl.BlockSpec((1, SUBLANES, LANES), lambda i: (i, 0, 0))),
        compiler_params=_cparams(("parallel",)),
        name="moe_route",
    )(h2, wt, bg, be)


def _moe_kernel(meta_ref, h_ref, rinfo_ref, cinfo_ref, w1_ref, w3_ref, w2_ref, g_ref, b_ref,
                o_ref, hb_ref, acc_ref):
    t = pl.program_id(0)
    eg = pl.program_id(1)
    tm = h_ref.shape[0]

    @pl.when(eg == 0)
    def _():
        hb_ref[...] = h_ref[...].astype(BF16)
        acc_ref[...] = jnp.zeros_like(acc_ref)

    def chunk(j, r0, m, end):
        info = rinfo_ref[0]
        s = lax.broadcasted_iota(jnp.int32, (m, tm), 0) + r0
        pick = (s == info[0:1].astype(jnp.int32)) | (s == info[1:2].astype(jnp.int32))
        x = _dot(jnp.where(pick, 1.0, 0.0).astype(BF16), hb_ref[...]).astype(BF16)
        a = _dot(x, w1_ref[j])
        b = _dot(x, w3_ref[j])
        hid = (a * jax.nn.sigmoid(a) * b).astype(BF16)
        y = _dot(hid, w2_ref[j]).astype(BF16)
        ci = cinfo_ref[...]
        sc = lax.broadcasted_iota(jnp.int32, (tm, m), 1) + r0
        live = sc < end
        q = (jnp.where((sc == ci[:, 0:1].astype(jnp.int32)) & live, ci[:, 2:3], 0.0)
             + jnp.where((sc == ci[:, 1:2].astype(jnp.int32)) & live, ci[:, 3:4], 0.0))
        acc_ref[...] += _dot(q.astype(BF16), y)

    for j in range(MOE_EPS):
        e = eg * MOE_EPS + j
        start = meta_ref[(2 * t) * N_EXPERTS + e]
        end = start + meta_ref[(2 * t + 1) * N_EXPERTS + e]

        @pl.when(end > start)
        def _(j=j, start=start, end=end):
            chunk(j, start, MOE_M1, end)

            def extra(k, c):
                chunk(j, start + MOE_M1 + k * MOE_M2, MOE_M2, end)
                return c

            lax.fori_loop(0, (jnp.maximum(end - start - MOE_M1, 0) + MOE_M2 - 1) // MOE_M2, extra, 0)

    @pl.when(eg == N_EXPERTS // MOE_EPS - 1)
    def _():
        o_ref[...] = _layer_norm(DN_ALPHA * h_ref[...] + acc_ref[...], g_ref[...], b_ref[...])


def _moe(h2, rinfo, cinfo, meta, w1, w3, w2, g3, b3, tm=MOE_TM):
    T = h2.shape[0]
    return pl.pallas_call(
        _moe_kernel,
        out_shape=jax.ShapeDtypeStruct((T, D_MODEL), F32),
        grid_spec=pltpu.PrefetchScalarGridSpec(
            num_scalar_prefetch=1,
            grid=(T // tm, N_EXPERTS // MOE_EPS),
            in_specs=[
                pl.BlockSpec((tm, D_MODEL), lambda i, e, m: (i, 0)),
                pl.BlockSpec((1, SUBLANES, tm), lambda i, e, m: (i, 0, 0)),
                pl.BlockSpec((tm, LANES), lambda i, e, m: (i, 0)),
                pl.BlockSpec((MOE_EPS, D_MODEL, EXPERT_FF), lambda i, e, m: (e, 0, 0)),
                pl.BlockSpec((MOE_EPS, D_MODEL, EXPERT_FF), lambda i, e, m: (e, 0, 0)),
                pl.BlockSpec((MOE_EPS, EXPERT_FF, D_MODEL), lambda i, e, m: (e, 0, 0)),
                pl.BlockSpec((1, D_MODEL), lambda i, e, m: (0, 0)),
                pl.BlockSpec((1, D_MODEL), lambda i, e, m: (0, 0)),
            ],
            out_specs=pl.BlockSpec((tm, D_MODEL), lambda i, e, m: (i, 0)),
            scratch_shapes=[pltpu.VMEM((tm, D_MODEL), BF16), pltpu.VMEM((tm, D_MODEL), F32)],
        ),
        compiler_params=_cparams(("parallel", "arbitrary")),
        name="moe",
    )(meta, h2, rinfo, cinfo, w1, w3, w2, g3, b3)


def _moe_layer(h2, wg, bg, we, be, w1, w3, w2, g3, b3, tm=MOE_TM):
    pad = GROUP_ROWS - N_GROUPS
    wt = jnp.concatenate([jnp.pad(wg.astype(F32).T, ((0, pad), (0, 0))), we.astype(F32).T], axis=0)
    bgc = jnp.pad(bg.astype(F32).reshape(-1, 1), ((0, pad), (0, 0)))
    rinfo, cinfo, meta = _route(h2, wt, bgc, be.astype(F32).reshape(-1, 1), tm=tm)
    meta = meta[:, :2, :N_EXPERTS].reshape(-1)
    return _moe(h2, rinfo, cinfo, meta, w1, w3, w2, g3, b3, tm=tm)


def kernel(x, mem, ln_in_g, ln_in_b, w_in, hg_lb_logits, hg_norm_g, w_branch_a, w_branch_b, w_mix_out,
           ln1_g, ln1_b, xa_wq, xa_wk, xa_wv, xa_wo, ln2_g, ln2_b, router_wg, router_bg, router_we,
           router_be, moe_w1, moe_w3, moe_w2, ln3_g, ln3_b):
    B, S, D = x.shape
    T = B * S
    row = lambda a: a.reshape(1, -1).astype(F32)
    x2 = x.reshape(T, D)
    gin, bin_ = row(ln_in_g), row(ln_in_b)
    assert w_in.shape[0] == DEPTH
    l = 0
    proj = _in_proj(x2, gin, bin_, w_in[l].astype(BF16))
    oa = _hgrn2(proj, hg_lb_logits.astype(F32), row(hg_norm_g[l]), B, S, l)
    ob = _sb_attn(proj, B, S)
    h1 = _mix(x2, gin, bin_, oa, ob, proj, w_branch_a[l].astype(BF16), w_branch_b[l].astype(BF16),
              w_mix_out[l].astype(BF16), row(ln1_g[l]), row(ln1_b[l]))
    kx, vx = _xa_kv(mem, xa_wk[l].astype(BF16), xa_wv[l].astype(BF16))
    h2 = _xattn(h1, kx, vx, xa_wq[l].astype(BF16), xa_wo[l].astype(BF16), row(ln2_g[l]), row(ln2_b[l]), B, S)
    out = _moe_layer(h2, router_wg[l], router_bg[l], router_we[l], router_be[l],
                     moe_w1[l].astype(BF16), moe_w3[l].astype(BF16), moe_w2[l].astype(BF16),
                     row(ln3_g[l]), row(ln3_b[l]))
    return out.reshape(B, S, D)
```

```python
import functools

import jax
import jax.numpy as jnp
from jax import lax
from jax.experimental import pallas as pl
from jax.experimental.pallas import tpu as pltpu

F32 = jnp.float32
BF16 = jnp.bfloat16

D_MODEL = 1024
HG_HEADS = 4
HG_DIM = 128
HG_W = HG_HEADS * HG_DIM
HG_CHUNK = 64
HG_LOCAL_CHUNKS = 4
SB_HEADS = 8
SB_HEAD_DIM = 64
SB_W = SB_HEADS * SB_HEAD_DIM
SB_BLOCK = 128
XA_HEADS = 4
XA_HEAD_DIM = D_MODEL // XA_HEADS
N_GROUPS = 4
EXPERTS_PER_GROUP = 4
N_EXPERTS = N_GROUPS * EXPERTS_PER_GROUP
EXPERT_FF = 512
N_IN = HG_W * 4 + SB_W * 3 + D_MODEL * 2
COL_HG = 0
COL_SB = 4 * HG_W
COL_GATE = COL_SB + 3 * SB_W
LN_EPS = 1e-5
RMS_EPS = 1e-6
DEPTH = 1
DN_ALPHA = (2 * DEPTH) ** 0.25

LANES = 128
VMEM_LIMIT = 56 * 1024 * 1024


def _cparams(sem, flags=None):
    return pltpu.CompilerParams(dimension_semantics=sem, vmem_limit_bytes=VMEM_LIMIT, flags=flags)


def _layer_norm(x, g, b):
    mu = jnp.mean(x, axis=-1, keepdims=True)
    xc = x - mu
    var = jnp.mean(xc * xc, axis=-1, keepdims=True)
    return xc * lax.rsqrt(var + LN_EPS) * g + b


def _dot(a, b):
    return jnp.dot(a, b, preferred_element_type=F32)


def _dot_nt(a, b):
    return lax.dot_general(a, b, (((1,), (1,)), ((), ())), preferred_element_type=F32)


def _dot_tn(a, b):
    return lax.dot_general(a, b, (((0,), (0,)), ((), ())), preferred_element_type=F32)


def _split3(a):
    hi = a.astype(BF16)
    r1 = a - hi.astype(F32)
    mid = r1.astype(BF16)
    lo = (r1 - mid.astype(F32)).astype(BF16)
    return hi, mid, lo


def _const_spec(shape):
    nd = len(shape)
    return pl.BlockSpec(shape, lambda *_: (0,) * nd)


def _in_proj_kernel(x_ref, g_ref, b_ref, w_ref, o_ref, *, col_chunk):
    h = _layer_norm(x_ref[...], g_ref[...], b_ref[...]).astype(BF16)
    for c in range(N_IN // col_chunk):
        sl = slice(c * col_chunk, (c + 1) * col_chunk)
        o_ref[:, sl] = _dot(h, w_ref[:, sl]).astype(BF16)


def _in_proj(x2, g, b, w_bf, tm=512, col_chunk=512):
    T = x2.shape[0]
    return pl.pallas_call(
        functools.partial(_in_proj_kernel, col_chunk=col_chunk),
        out_shape=jax.ShapeDtypeStruct((T, N_IN), BF16),
        grid=(T // tm,),
        in_specs=[
            pl.BlockSpec((tm, D_MODEL), lambda i: (i, 0)),
            _const_spec((1, D_MODEL)),
            _const_spec((1, D_MODEL)),
            _const_spec((D_MODEL, N_IN)),
        ],
        out_specs=pl.BlockSpec((tm, N_IN), lambda i: (i, 0)),
        compiler_params=_cparams(("parallel",)),
        name="in_proj",
    )(x2, g, b, w_bf)


def _hgrn2_kernel(lbl_ref, ng_ref, q_ref, f_ref, i_ref, g_ref, o_ref,
                  state_ref, intra_ref, qs_ref, dl_ref, u_ref, *, layer, tb):
    C = HG_CHUNK
    H2 = C // 2

    @pl.when(pl.program_id(1) == 0)
    def _():
        state_ref[...] = jnp.zeros_like(state_ref)

    lbl = lbl_ref[...]
    ex = jnp.exp(lbl - jnp.max(lbl, axis=0, keepdims=True))
    lb = jnp.sum(ex[: layer + 1], axis=0, keepdims=True) / jnp.sum(ex, axis=0, keepdims=True)

    R = HG_LOCAL_CHUNKS * C
    row = lax.broadcasted_iota(jnp.int32, (R, R), 0)
    col = lax.broadcasted_iota(jnp.int32, (R, R), 1)
    same_chunk = (row // C) == (col // C)
    tri = (same_chunk & (col <= row)).astype(BF16)
    m_diag = ((row // H2) == (col // H2)) & (col <= row)
    m_cross = same_chunk & ((row % C) >= H2) & ((col % C) < H2)
    rrow = lax.broadcasted_iota(jnp.int32, (R, HG_W), 0)
    ng = ng_ref[...]
    heads = [slice(h * HG_DIM, (h + 1) * HG_DIM) for h in range(HG_HEADS)]

    def row_of_period(x, offset, period):
        out = jnp.broadcast_to(x[offset:offset + 1, :], x.shape)
        for k in range(1, R // period):
            out = jnp.where(rrow >= k * period, x[k * period + offset:k * period + offset + 1, :], out)
        return out

    def local(blk):
        rows = slice(blk * R, (blk + 1) * R)
        forget = lb + (1.0 - lb) * jax.nn.sigmoid(f_ref[rows, :].astype(F32))
        kk = 1.0 - forget
        hi, mid, lo = _split3(jnp.log(forget))
        b = _dot(tri, hi) + _dot(tri, mid) + _dot(tri, lo)
        q = q_ref[rows, :].astype(F32)
        d1 = b - row_of_period(b, H2 // 2 - 1, H2)
        q1 = (q * jnp.exp(d1)).astype(BF16)
        k1 = (kk * jnp.exp(-d1)).astype(BF16)
        e3 = jnp.exp(-jnp.abs(b - row_of_period(b, H2 - 1, C)))
        q3 = (q * e3).astype(BF16)
        k3 = (kk * e3).astype(BF16)
        kd = (kk * jnp.exp(row_of_period(b, C - 1, C) - b)).astype(BF16)
        qs_ref[rows, :] = (q * jnp.exp(b)).astype(BF16)
        for k in range(HG_LOCAL_CHUNKS):
            dl_ref[blk * HG_LOCAL_CHUNKS + k] = jnp.exp(b[k * C + C - 1:(k + 1) * C, :])
        s1 = [_dot_nt(q1[:, ls], k1[:, ls]) for ls in heads]
        s3 = [_dot_nt(q3[:, ls], k3[:, ls]) for ls in heads]
        s = [jnp.where(m_diag, a1, jnp.where(m_cross, a3, 0.0)).astype(BF16) for a1, a3 in zip(s1, s3)]
        for h, ls in enumerate(heads):
            v = i_ref[rows, ls]
            intra_ref[rows, ls] = _dot(s[h], v)
            for k in range(HG_LOCAL_CHUNKS):
                cs = slice(k * C, (k + 1) * C)
                u_ref[blk * HG_LOCAL_CHUNKS + k, h] = _dot_tn(v[cs], kd[cs, ls])

    def carry_state(c, carry):
        rows = pl.ds(pl.multiple_of(c * C, C), C)
        dl = dl_ref[c]
        for h, ls in enumerate(heads):
            st = state_ref[h]
            o = intra_ref[rows, ls] + _dot_nt(qs_ref[rows, ls], st.astype(BF16))
            state_ref[h] = st * dl[:, ls] + u_ref[c, h]
            rms = lax.rsqrt(jnp.mean(o * o, axis=-1, keepdims=True) + RMS_EPS)
            gate = g_ref[rows, ls].astype(F32)
            o_ref[rows, ls] = (o * rms * ng * (gate * jax.nn.sigmoid(gate))).astype(o_ref.dtype)
        return carry

    for blk in range(tb // R):
        local(blk)
    lax.fori_loop(0, tb // C, carry_state, 0, unroll=2)


def _hgrn2(proj, lb_logits, norm_g, B, S, layer, tb=512):
    T = B * S
    nsb = S // tb

    def spec(k):
        return pl.BlockSpec((tb, HG_W), lambda b, s, k=k: (b * nsb + s, COL_HG // HG_W + k))

    return pl.pallas_call(
        functools.partial(_hgrn2_kernel, layer=layer, tb=tb),
        out_shape=jax.ShapeDtypeStruct((T, HG_W), BF16),
        grid=(B, nsb),
        in_specs=[
            _const_spec(lb_logits.shape),
            _const_spec((1, HG_DIM)),
            spec(0), spec(1), spec(2), spec(3),
        ],
        out_specs=pl.BlockSpec((tb, HG_W), lambda b, s: (b * nsb + s, 0)),
        scratch_shapes=[pltpu.VMEM((HG_HEADS, HG_DIM, HG_DIM), F32),
                        pltpu.VMEM((tb, HG_W), F32),
                        pltpu.VMEM((tb, HG_W), BF16),
                        pltpu.VMEM((tb // HG_CHUNK, 1, HG_W), F32),
                        pltpu.VMEM((tb // HG_CHUNK, HG_HEADS, HG_DIM, HG_DIM), F32)],
        compiler_params=_cparams(("parallel", "arbitrary")),
        name="hgrn2",
    )(lb_logits, norm_g, proj, proj, proj, proj)


LOG2E = 1.4426950408889634
SB_DEAD = -160.0


def _sb_kernel(q_ref, k_ref, v_ref, o_ref, acc_ref, run_ref, *, npairs, tq):
    i = pl.program_id(1)
    BLK = SB_BLOCK
    W2 = 2 * BLK
    ndiag = tq // BLK
    scale = SB_HEAD_DIM ** -0.5
    lane = lax.broadcasted_iota(jnp.int32, (BLK, LANES), 1)
    head0 = lane < SB_HEAD_DIM
    zrow = lax.broadcasted_iota(jnp.int32, (tq, W2), 0)
    zcol = lax.broadcasted_iota(jnp.int32, (tq, W2), 1) & (BLK - 1)
    wr = lax.broadcasted_iota(jnp.int32, (W2, W2), 0) & (BLK - 1)
    wc = lax.broadcasted_iota(jnp.int32, (W2, W2), 1)
    cw = ((wc >= BLK) | (wr > wc)).astype(BF16)
    qs = q_ref[...] * jnp.asarray(scale, q_ref.dtype)

    def masked_kv(ref, j, p):
        x = ref[pl.ds(pl.multiple_of(j * BLK, BLK), BLK), p * LANES:(p + 1) * LANES]
        zero = jnp.zeros_like(x)
        return jnp.concatenate([jnp.where(head0, x, zero), jnp.where(head0, zero, x)], axis=0)

    def step(js, ds):
        nb = len(js)
        zs = [[_dot_nt(qs[:, p * LANES:(p + 1) * LANES], masked_kv(k_ref, js[b], p)) * LOG2E
               for p in range(npairs)] for b in range(nb)]
        ws = []
        for b in range(nb):
            before = None if ds[b] is None else (zcol + ds[b] * BLK) < zrow
            lbs, hls = [], []
            for p in range(npairs):
                z = zs[b][p]
                l2 = jnp.log2(1.0 + jnp.exp2(-jnp.abs(z)))
                lb = jnp.minimum(z, 0.0) - l2
                lk = lb - z
                if before is not None:
                    lk = jnp.where(before, lk, 0.0)
                hi = lk.astype(BF16)
                lo = (lk - hi.astype(F32)).astype(BF16)
                lbs.append(lb)
                hls.append(jnp.concatenate([hi[:, :BLK], lo[:, :BLK]], axis=1))
                hls.append(jnp.concatenate([hi[:, BLK:], lo[:, BLK:]], axis=1))
            cs = _dot(jnp.concatenate(hls, axis=0), cw)
            wb = []
            for p in range(npairs):
                c0 = cs[(2 * p) * tq:(2 * p + 1) * tq]
                c1 = cs[(2 * p + 1) * tq:(2 * p + 2) * tq]
                cum = jnp.concatenate([c0[:, :BLK], c1[:, :BLK]], axis=1)
                rsum = jnp.concatenate([c0[:, BLK:], c1[:, BLK:]], axis=1)
                run = run_ref[p]
                w = jnp.exp2(lbs[p] + (run + cum))
                if before is not None:
                    w = jnp.where(before, w, 0.0)
                run_ref[p] = run + rsum
                wb.append(w.astype(BF16))
            ws.append(wb)
        for p in range(npairs):
            o = acc_ref[p]
            for b in range(nb):
                o = o + _dot(ws[b][p], masked_kv(v_ref, js[b], p))
            acc_ref[p] = o

    acc_ref[...] = jnp.zeros_like(acc_ref)
    run_ref[...] = jnp.zeros_like(run_ref)
    step([i * ndiag + d for d in reversed(range(ndiag))], list(reversed(range(ndiag))))

    def live():
        return jnp.max(run_ref[...]) > SB_DEAD

    def cond(c):
        return jnp.logical_and(c[0] < i, c[1])

    def body(c):
        base = (i - 1 - c[0]) * ndiag
        step([base + d for d in reversed(range(ndiag))], [None] * ndiag)
        return c[0] + 1, live()

    lax.while_loop(cond, body, (jnp.int32(0), live()))
    for p in range(npairs):
        o_ref[:, p * LANES:(p + 1) * LANES] = acc_ref[p].astype(o_ref.dtype)


def _sb_attn(proj, B, S, npairs=SB_W // LANES, tq=256):
    T = B * S
    nq = S // tq
    wblk = npairs * LANES
    ng = SB_W // wblk
    cq = COL_SB // wblk
    ck = (COL_SB + SB_W) // wblk
    cv = (COL_SB + 2 * SB_W) // wblk
    return pl.pallas_call(
        functools.partial(_sb_kernel, npairs=npairs, tq=tq),
        out_shape=jax.ShapeDtypeStruct((T, SB_W), BF16),
        grid=(B * ng, nq),
        in_specs=[
            pl.BlockSpec((tq, wblk), lambda g, i: ((g // ng) * nq + i, cq + g % ng)),
            pl.BlockSpec((S, wblk), lambda g, i: (g // ng, ck + g % ng)),
            pl.BlockSpec((S, wblk), lambda g, i: (g // ng, cv + g % ng)),
        ],
        out_specs=pl.BlockSpec((tq, wblk), lambda g, i: ((g // ng) * nq + i, g % ng)),
        scratch_shapes=[pltpu.VMEM((npairs, tq, LANES), F32),
                        pltpu.VMEM((npairs, tq, 2 * SB_BLOCK), F32)],
        compiler_params=_cparams(("parallel", "arbitrary")),
        name="sb_attn",
    )(proj, proj, proj)


GATE_BLK = 512


def _mix_kernel(x_ref, gin_ref, bin_ref, oa_ref, ob_ref, ga0_ref, ga1_ref, gb0_ref, gb1_ref,
                wa_ref, wb_ref, wm_ref, g1_ref, b1_ref, o_ref):
    h0 = _layer_norm(x_ref[...], gin_ref[...], bin_ref[...])
    oa = oa_ref[...]
    ob = ob_ref[...]
    y = None
    for half, (ga_ref, gb_ref) in enumerate(((ga0_ref, gb0_ref), (ga1_ref, gb1_ref))):
        cs = slice(half * GATE_BLK, (half + 1) * GATE_BLK)
        merged = (jax.nn.sigmoid(ga_ref[...].astype(F32)) * _dot(oa, wa_ref[:, cs])
                  + jax.nn.sigmoid(gb_ref[...].astype(F32)) * _dot(ob, wb_ref[:, cs]))
        part = _dot(merged.astype(BF16), wm_ref[cs, :])
        y = part if y is None else y + part
    o_ref[...] = _layer_norm(DN_ALPHA * h0 + y, g1_ref[...], b1_ref[...])


def _mix(x2, gin, bin_, oa, ob, proj, wa, wb, wm, g1, b1, tm=512):
    T = x2.shape[0]
    c0 = COL_GATE // GATE_BLK

    def gate_spec(k):
        return pl.BlockSpec((tm, GATE_BLK), lambda i, k=k: (i, c0 + k))

    return pl.pallas_call(
        _mix_kernel,
        out_shape=jax.ShapeDtypeStruct((T, D_MODEL), F32),
        grid=(T // tm,),
        in_specs=[
            pl.BlockSpec((tm, D_MODEL), lambda i: (i, 0)),
            _const_spec((1, D_MODEL)), _const_spec((1, D_MODEL)),
            pl.BlockSpec((tm, HG_W), lambda i: (i, 0)),
            pl.BlockSpec((tm, SB_W), lambda i: (i, 0)),
            gate_spec(0), gate_spec(1), gate_spec(2), gate_spec(3),
            _const_spec((HG_W, D_MODEL)), _const_spec((SB_W, D_MODEL)), _const_spec((D_MODEL, D_MODEL)),
            _const_spec((1, D_MODEL)), _const_spec((1, D_MODEL)),
        ],
        out_specs=pl.BlockSpec((tm, D_MODEL), lambda i: (i, 0)),
        compiler_params=_cparams(("parallel",)),
        name="mix",
    )(x2, gin, bin_, oa, ob, proj, proj, proj, proj, wa, wb, wm, g1, b1)


def _xa_kv_kernel(m_ref, wk_ref, wv_ref, k_ref, v_ref):
    m = m_ref[0].astype(BF16)
    k_ref[0] = _dot(m, wk_ref[...]).astype(BF16)
    v_ref[0] = _dot(m, wv_ref[...]).astype(BF16)


def _xa_kv(mem, wk, wv):
    B, M, _ = mem.shape
    W = wk.shape[1]
    return pl.pallas_call(
        _xa_kv_kernel,
        out_shape=(jax.ShapeDtypeStruct((B, M, W), BF16), jax.ShapeDtypeStruct((B, M, W), BF16)),
        grid=(B,),
        in_specs=[pl.BlockSpec((1, M, D_MODEL), lambda b: (b, 0, 0)),
                  _const_spec(wk.shape), _const_spec(wv.shape)],
        out_specs=(pl.BlockSpec((1, M, W), lambda b: (b, 0, 0)),
                   pl.BlockSpec((1, M, W), lambda b: (b, 0, 0))),
        compiler_params=_cparams(("parallel",)),
        name="xa_kv",
    )(mem, wk, wv)


def _xattn_kernel(h_ref, k_ref, v_ref, wq_ref, wo_ref, g_ref, b_ref, o_ref):
    h = h_ref[...]
    q = _dot(h.astype(BF16), wq_ref[...])
    scale = XA_HEAD_DIM ** -0.5
    outs = []
    for hd in range(XA_HEADS):
        ls = slice(hd * XA_HEAD_DIM, (hd + 1) * XA_HEAD_DIM)
        s = _dot_nt(q[:, ls].astype(BF16), k_ref[0, :, ls]) * scale
        s = s - jnp.max(s, axis=-1, keepdims=True)
        p = jnp.exp(s)
        p = p / jnp.sum(p, axis=-1, keepdims=True)
        outs.append(_dot(p.astype(BF16), v_ref[0, :, ls]).astype(BF16))
    o = jnp.concatenate(outs, axis=-1)
    y = _dot(o, wo_ref[...])
    o_ref[...] = _layer_norm(DN_ALPHA * h + y, g_ref[...], b_ref[...])


def _xattn(h1, kx, vx, wq, wo, g2, b2, B, S, tm=512):
    T = B * S
    M = kx.shape[1]
    W = kx.shape[2]
    nsb = S // tm
    return pl.pallas_call(
        _xattn_kernel,
        out_shape=jax.ShapeDtypeStruct((T, D_MODEL), F32),
        grid=(B, nsb),
        in_specs=[
            pl.BlockSpec((tm, D_MODEL), lambda b, s: (b * nsb + s, 0)),
            pl.BlockSpec((1, M, W), lambda b, s: (b, 0, 0)),
            pl.BlockSpec((1, M, W), lambda b, s: (b, 0, 0)),
            _const_spec(wq.shape), _const_spec(wo.shape),
            _const_spec((1, D_MODEL)), _const_spec((1, D_MODEL)),
        ],
        out_specs=pl.BlockSpec((tm, D_MODEL), lambda b, s: (b * nsb + s, 0)),
        compiler_params=_cparams(("parallel", "arbitrary")),
        name="xattn",
    )(h1, kx, vx, wq, wo, g2, b2)


MOE_TM = 1024
MOE_M1 = 160
MOE_M2 = 64
MOE_EPS = 2
SUBLANES = 8
GROUP_ROWS = 16


def _hdot(a, b):
    return jnp.dot(a, b, preferred_element_type=F32, precision=lax.Precision.HIGHEST)


def _split2(a):
    hi = a.astype(BF16)
    return hi, (a - hi.astype(F32)).astype(BF16)


def _first_argmax0(vals, row, height):
    m = jnp.max(vals, axis=0, keepdims=True)
    idx = jnp.min(jnp.where(vals == m, row, height), axis=0, keepdims=True)
    return m, idx


def _select_rows(shape, rows):
    r = lax.broadcasted_iota(jnp.int32, shape, 0)
    out = jnp.zeros(shape, F32)
    for k, v in enumerate(rows):
        out = jnp.where(r == k, v, out)
    return out


def _route_kernel(h_ref, wt_ref, bg_ref, be_ref, rinfo_ref, cinfo_ref, meta_ref):
    tm = h_ref.shape[0]
    h = h_ref[...]
    neg = jnp.float32(-jnp.inf)
    rg = lax.broadcasted_iota(jnp.int32, (GROUP_ROWS, tm), 0)
    re = lax.broadcasted_iota(jnp.int32, (N_EXPERTS, tm), 0)
    nw = GROUP_ROWS + N_EXPERTS
    hh, hl = _split2(h)
    wh, wl = _split2(wt_ref[...])
    lg = _dot_nt(jnp.concatenate([wh, wl], axis=0), hh)
    lg = lg[:nw] + (lg[nw:] + _dot_nt(wh, hl))
    gl = jnp.where(rg < N_GROUPS, lg[:GROUP_ROWS] + bg_ref[...], neg)
    el = lg[GROUP_ROWS:] + be_ref[...]
    gm, g_idx = _first_argmax0(gl, rg, GROUP_ROWS)
    g_top = 1.0 / jnp.sum(jnp.exp(gl - gm), axis=0, keepdims=True)
    ml = jnp.where(re // EXPERTS_PER_GROUP == g_idx, el, neg)
    m1, i1 = _first_argmax0(ml, re, N_EXPERTS)
    ml2 = jnp.where(re == i1, neg, ml)
    m2, i2 = _first_argmax0(ml2, re, N_EXPERTS)
    e2 = jnp.exp(m2 - m1)
    gate0 = g_top / (1.0 + e2)
    gate1 = g_top * e2 / (1.0 + e2)

    oh0 = jnp.where(re == i1, 1.0, 0.0)
    oh1 = jnp.where(re == i2, 1.0, 0.0)
    cnt0 = jnp.sum(oh0, axis=1, keepdims=True)
    cnt1 = jnp.sum(oh1, axis=1, keepdims=True)
    er =lax.broadcasted_iota(jnp.int32, (N_EXPERTS, N_EXPERTS), 0)
    ec = lax.broadcasted_iota(jnp.int32, (N_EXPERTS, N_EXPERTS), 1)
    seg = _hdot(jnp.where(ec < er, 1.0, 0.0), jnp.broadcast_to(cnt0 + cnt1, (N_EXPERTS, LANES)))[:, 0:1]
    tr = lax.broadcasted_iota(jnp.int32, (tm, tm), 0)
    tc = lax.broadcasted_iota(jnp.int32, (tm, tm), 1)
    earlier = jnp.where(tr < tc, 1.0, 0.0).astype(BF16)
    cum = _dot(jnp.concatenate([oh0, oh1], axis=0).astype(BF16), earlier)
    slot0 = jnp.sum(oh0 * (seg + cum[:N_EXPERTS]), axis=0, keepdims=True)
    slot1 = jnp.sum(oh1 * (seg + cnt0 + cum[N_EXPERTS:]), axis=0, keepdims=True)

    info = (slot0, slot1, gate0, gate1)
    rinfo_ref[0] = _select_rows((SUBLANES, tm), info)
    cinfo_ref[...] = _select_rows((LANES, tm), info).T

    ohs = jnp.concatenate([oh0 + oh1, jnp.zeros((LANES - N_EXPERTS, tm), F32)], axis=0).astype(BF16)
    cnt_row = _dot_nt(jnp.ones((SUBLANES, tm), BF16), ohs)
    lr = lax.broadcasted_iota(jnp.int32, (LANES, LANES), 0)
    lc = lax.broadcasted_iota(jnp.int32, (LANES, LANES), 1)
    seg_row = _hdot(cnt_row, jnp.where(lr < lc, 1.0, 0.0))
    r8 = lax.broadcasted_iota(jnp.int32, (SUBLANES, LANES), 0)
    meta_ref[0] = jnp.where(r8 == 0, seg_row, jnp.where(r8 == 1, cnt_row, 0.0)).astype(jnp.int32)


def _route(h2, wt, bg, be, tm=MOE_TM):
    T = h2.shape[0]
    nt = T // tm
    return pl.pallas_call(
        _route_kernel,
        out_shape=(jax.ShapeDtypeStruct((nt, SUBLANES, tm), F32),
                   jax.ShapeDtypeStruct((T, LANES), F32),
                   jax.ShapeDtypeStruct((nt, SUBLANES, LANES), jnp.int32)),
        grid=(nt,),
        in_specs=[pl.BlockSpec((tm, D_MODEL), lambda i: (i, 0)),
                  _const_spec(wt.shape), _const_spec(bg.shape), _const_spec(be.shape)],
        out_specs=(pl.BlockSpec((1, SUBLANES, tm), lambda i: (i, 0, 0)),
                   pl.BlockSpec((tm, LANES), lambda i: (i, 0)),
                   pl.BlockSpec((1, SUBLANES, LANES), lambda i: (i, 0, 0))),
        compiler_params=_cparams(("parallel",)),
        name="moe_route",
    )(h2, wt, bg, be)


def _moe_kernel(meta_ref, h_ref, rinfo_ref, cinfo_ref, w1_ref, w3_ref, w2_ref, g_ref, b_ref,
                o_ref, hb_ref, acc_ref):
    t = pl.program_id(0)
    eg = pl.program_id(1)
    tm = h_ref.shape[0]

    @pl.when(eg == 0)
    def _():
        hb_ref[...] = h_ref[...].astype(BF16)
        acc_ref[...] = jnp.zeros_like(acc_ref)

    def chunk(j, r0, m, end):
        info = rinfo_ref[0]
        s = lax.broadcasted_iota(jnp.int32, (m, tm), 0) + r0
        pick = (s == info[0:1].astype(jnp.int32)) | (s == info[1:2].astype(jnp.int32))
        x = _dot(jnp.where(pick, 1.0, 0.0).astype(BF16), hb_ref[...]).astype(BF16)
        a = _dot(x, w1_ref[j])
        b = _dot(x, w3_ref[j])
        hid = (a * jax.nn.sigmoid(a) * b).astype(BF16)
        y = _dot(hid, w2_ref[j]).astype(BF16)
        ci = cinfo_ref[...]
        sc = lax.broadcasted_iota(jnp.int32, (tm, m), 1) + r0
        live = sc < end
        q = (jnp.where((sc == ci[:, 0:1].astype(jnp.int32)) & live, ci[:, 2:3], 0.0)
             + jnp.where((sc == ci[:, 1:2].astype(jnp.int32)) & live, ci[:, 3:4], 0.0))
        acc_ref[...] += _dot(q.astype(BF16), y)

    for j in range(MOE_EPS):
        e = eg * MOE_EPS + j
        start = meta_ref[(2 * t) * N_EXPERTS + e]
        end = start + meta_ref[(2 * t + 1) * N_EXPERTS + e]

        @pl.when(end > start)
        def _(j=j, start=start, end=end):
            chunk(j, start, MOE_M1, end)

            def extra(k, c):
                chunk(j, start + MOE_M1 + k * MOE_M2, MOE_M2, end)
                return c

            lax.fori_loop(0, (jnp.maximum(end - start - MOE_M1, 0) + MOE_M2 - 1) // MOE_M2, extra, 0)

    @pl.when(eg == N_EXPERTS // MOE_EPS - 1)
    def _():
        o_ref[...] = _layer_norm(DN_ALPHA * h_ref[...] + acc_ref[...], g_ref[...], b_ref[...])


def _moe(h2, rinfo, cinfo, meta, w1, w3, w2, g3, b3, tm=MOE_TM):
    T = h2.shape[0]
    return pl.pallas_call(
        _moe_kernel,
        out_shape=jax.ShapeDtypeStruct((T, D_MODEL), F32),
        grid_spec=pltpu.PrefetchScalarGridSpec(
            num_scalar_prefetch=1,
            grid=(T // tm, N_EXPERTS // MOE_EPS),
            in_specs=[
                pl.BlockSpec((tm, D_MODEL), lambda i, e, m: (i, 0)),
                pl.BlockSpec((1, SUBLANES, tm), lambda i, e, m: (i, 0, 0)),
                pl.BlockSpec((tm, LANES), lambda i, e, m: (i, 0)),
                pl.BlockSpec((MOE_EPS, D_MODEL, EXPERT_FF), lambda i, e, m: (e, 0, 0)),
                pl.BlockSpec((MOE_EPS, D_MODEL, EXPERT_FF), lambda i, e, m: (e, 0, 0)),
                pl.BlockSpec((MOE_EPS, EXPERT_FF, D_MODEL), lambda i, e, m: (e, 0, 0)),
                pl.BlockSpec((1, D_MODEL), lambda i, e, m: (0, 0)),
                pl.BlockSpec((1, D_MODEL), lambda i, e, m: (0, 0)),
            ],
            out_specs=pl.BlockSpec((tm, D_MODEL), lambda i, e, m: (i, 0)),
            scratch_shapes=[pltpu.VMEM((tm, D_MODEL), BF16), pltpu.VMEM((tm, D_MODEL), F32)],
        ),
        compiler_params=_cparams(("parallel", "arbitrary")),
        name="moe",
    )(meta, h2, rinfo, cinfo, w1, w3, w2, g3, b3)


def _moe_layer(h2, wg, bg, we, be, w1, w3, w2, g3, b3, tm=MOE_TM):
    pad = GROUP_ROWS - N_GROUPS
    wt = jnp.concatenate([jnp.pad(wg.astype(F32).T, ((0, pad), (0, 0))), we.astype(F32).T], axis=0)
    bgc = jnp.pad(bg.astype(F32).reshape(-1, 1), ((0, pad), (0, 0)))
    rinfo, cinfo, meta = _route(h2, wt, bgc, be.astype(F32).reshape(-1, 1), tm=tm)
    meta = meta[:, :2, :N_EXPERTS].reshape(-1)
    return _moe(h2, rinfo, cinfo, meta, w1, w3, w2, g3, b3, tm=tm)


def kernel(x, mem, ln_in_g, ln_in_b, w_in, hg_lb_logits, hg_norm_g, w_branch_a, w_branch_b, w_mix_out,
           ln1_g, ln1_b, xa_wq, xa_wk, xa_wv, xa_wo, ln2_g, ln2_b, router_wg, router_bg, router_we,
           router_be, moe_w1, moe_w3, moe_w2, ln3_g, ln3_b):
    B, S, D = x.shape
    T = B * S
    row = lambda a: a.reshape(1, -1).astype(F32)
    x2 = x.reshape(T, D)
    gin, bin_ = row(ln_in_g), row(ln_in_b)
    assert w_in.shape[0] == DEPTH
    l = 0
    proj = _in_proj(x2, gin, bin_, w_in[l].astype(BF16))
    oa = _hgrn2(proj, hg_lb_logits.astype(F32), row(hg_norm_g[l]), B, S, l)
    ob = _sb_attn(proj, B, S)
    h1 = _mix(x2, gin, bin_, oa, ob, proj, w_branch_a[l].astype(BF16), w_branch_b[l].astype(BF16),
              w_mix_out[l].astype(BF16), row(ln1_g[l]), row(ln1_b[l]))
    kx, vx = _xa_kv(mem, xa_wk[l].astype(BF16), xa_wv[l].astype(BF16))
    h2 = _xattn(h1, kx, vx, xa_wq[l].astype(BF16), xa_wo[l].astype(BF16), row(ln2_g[l]), row(ln2_b[l]), B, S)
    out = _moe_layer(h2, router_wg[l], router_bg[l], router_we[l], router_be[l],
                     moe_w1[l].astype(BF16), moe_w3[l].astype(BF16), moe_w2[l].astype(BF16),
                     row(ln3_g[l]), row(ln3_b[l]))
    return out.reshape(B, S, D)
```

```python
import functools

import jax
import jax.numpy as jnp
from jax import lax
from jax.experimental import pallas as pl
from jax.experimental.pallas import tpu as pltpu

F32 = jnp.float32
BF16 = jnp.bfloat16

D_MODEL = 1024
HG_HEADS = 4
HG_DIM = 128
HG_W = HG_HEADS * HG_DIM
HG_CHUNK = 64
HG_LOCAL_CHUNKS = 4
SB_HEADS = 8
SB_HEAD_DIM = 64
SB_W = SB_HEADS * SB_HEAD_DIM
SB_BLOCK = 128
XA_HEADS = 4
XA_HEAD_DIM = D_MODEL // XA_HEADS
N_GROUPS = 4
EXPERTS_PER_GROUP = 4
N_EXPERTS = N_GROUPS * EXPERTS_PER_GROUP
EXPERT_FF = 512
N_IN = HG_W * 4 + SB_W * 3 + D_MODEL * 2
COL_HG = 0
COL_SB = 4 * HG_W
COL_GATE = COL_SB + 3 * SB_W
LN_EPS = 1e-5
RMS_EPS = 1e-6
DEPTH = 1
DN_ALPHA = (2 * DEPTH) ** 0.25

LANES = 128
VMEM_LIMIT = 56 * 1024 * 1024


def _cparams(sem, flags=None):
    return pltpu.CompilerParams(dimension_semantics=sem, vmem_limit_bytes=VMEM_LIMIT, flags=flags)


def _layer_norm(x, g, b):
    mu = jnp.mean(x, axis=-1, keepdims=True)
    xc = x - mu
    var = jnp.mean(xc * xc, axis=-1, keepdims=True)
    return xc * lax.rsqrt(var + LN_EPS) * g + b


def _dot(a, b):
    return jnp.dot(a, b, preferred_element_type=F32)


def _dot_nt(a, b):
    return lax.dot_general(a, b, (((1,), (1,)), ((), ())), preferred_element_type=F32)


def _dot_tn(a, b):
    return lax.dot_general(a, b, (((0,), (0,)), ((), ())), preferred_element_type=F32)


def _split3(a):
    hi = a.astype(BF16)
    r1 = a - hi.astype(F32)
    mid = r1.astype(BF16)
    lo = (r1 - mid.astype(F32)).astype(BF16)
    return hi, mid, lo


def _const_spec(shape):
    nd = len(shape)
    return pl.BlockSpec(shape, lambda *_: (0,) * nd)


def _in_proj_kernel(x_ref, g_ref, b_ref, w_ref, o_ref, *, col_chunk):
    h = _layer_norm(x_ref[...], g_ref[...], b_ref[...]).astype(BF16)
    for c in range(N_IN // col_chunk):
        sl = slice(c * col_chunk, (c + 1) * col_chunk)
        o_ref[:, sl] = _dot(h, w_ref[:, sl]).astype(BF16)


def _in_proj(x2, g, b, w_bf, tm=512, col_chunk=512):
    T = x2.shape[0]
    return pl.pallas_call(
        functools.partial(_in_proj_kernel, col_chunk=col_chunk),
        out_shape=jax.ShapeDtypeStruct((T, N_IN), BF16),
        grid=(T // tm,),
        in_specs=[
            pl.BlockSpec((tm, D_MODEL), lambda i: (i, 0)),
            _const_spec((1, D_MODEL)),
            _const_spec((1, D_MODEL)),
            _const_spec((D_MODEL, N_IN)),
        ],
        out_specs=pl.BlockSpec((tm, N_IN), lambda i: (i, 0)),
        compiler_params=_cparams(("parallel",)),
        name="in_proj",
    )(x2, g, b, w_bf)


def _hgrn2_kernel(lbl_ref, ng_ref, q_ref, f_ref, i_ref, g_ref, o_ref,
                  state_ref, intra_ref, qs_ref, dl_ref, u_ref, *, layer, tb):
    C = HG_CHUNK
    H2 = C // 2

    @pl.when(pl.program_id(1) == 0)
    def _():
        state_ref[...] = jnp.zeros_like(state_ref)

    lbl = lbl_ref[...]
    ex = jnp.exp(lbl - jnp.max(lbl, axis=0, keepdims=True))
    lb = jnp.sum(ex[: layer + 1], axis=0, keepdims=True) / jnp.sum(ex, axis=0, keepdims=True)

    R = HG_LOCAL_CHUNKS * C
    row = lax.broadcasted_iota(jnp.int32, (R, R), 0)
    col = lax.broadcasted_iota(jnp.int32, (R, R), 1)
    same_chunk = (row // C) == (col // C)
    tri = (same_chunk & (col <= row)).astype(BF16)
    m_diag = ((row // H2) == (col // H2)) & (col <= row)
    m_cross = same_chunk & ((row % C) >= H2) & ((col % C) < H2)
    rrow = lax.broadcasted_iota(jnp.int32, (R, HG_W), 0)
    ng = ng_ref[...]
    heads = [slice(h * HG_DIM, (h + 1) * HG_DIM) for h in range(HG_HEADS)]

    def row_of_period(x, offset, period):
        out = jnp.broadcast_to(x[offset:offset + 1, :], x.shape)
        for k in range(1, R // period):
            out = jnp.where(rrow >= k * period, x[k * period + offset:k * period + offset + 1, :], out)
        return out

    def local(blk):
        rows = slice(blk * R, (blk + 1) * R)
        forget = lb + (1.0 - lb) * jax.nn.sigmoid(f_ref[rows, :].astype(F32))
        kk = 1.0 - forget
        hi, mid, lo = _split3(jnp.log(forget))
        b = _dot(tri, hi) + _dot(tri, mid) + _dot(tri, lo)
        q = q_ref[rows, :].astype(F32)
        d1 = b - row_of_period(b, H2 // 2 - 1, H2)
        q1 = (q * jnp.exp(d1)).astype(BF16)
        k1 = (kk * jnp.exp(-d1)).astype(BF16)
        e3 = jnp.exp(-jnp.abs(b - row_of_period(b, H2 - 1, C)))
        q3 = (q * e3).astype(BF16)
        k3 = (kk * e3).astype(BF16)
        kd = (kk * jnp.exp(row_of_period(b, C - 1, C) - b)).astype(BF16)
        qs_ref[rows, :] = (q * jnp.exp(b)).astype(BF16)
        for k in range(HG_LOCAL_CHUNKS):
            dl_ref[blk * HG_LOCAL_CHUNKS + k] = jnp.exp(b[k * C + C - 1:(k + 1) * C, :])
        s1 = [_dot_nt(q1[:, ls], k1[:, ls]) for ls in heads]
        s3 = [_dot_nt(q3[:, ls], k3[:, ls]) for ls in heads]
        s = [jnp.where(m_diag, a1, jnp.where(m_cross, a3, 0.0)).astype(BF16) for a1, a3 in zip(s1, s3)]
        for h, ls in enumerate(heads):
            v = i_ref[rows, ls]
            intra_ref[rows, ls] = _dot(s[h], v)
            for k in range(HG_LOCAL_CHUNKS):
                cs = slice(k * C, (k + 1) * C)
                u_ref[blk * HG_LOCAL_CHUNKS + k, h] = _dot_tn(v[cs], kd[cs, ls])

    def carry_state(c, carry):
        rows = pl.ds(pl.multiple_of(c * C, C), C)
        dl = dl_ref[c]
        for h, ls in enumerate(heads):
            st = state_ref[h]
            o = intra_ref[rows, ls] + _dot_nt(qs_ref[rows, ls], st.astype(BF16))
            state_ref[h] = st * dl[:, ls] + u_ref[c, h]
            rms = lax.rsqrt(jnp.mean(o * o, axis=-1, keepdims=True) + RMS_EPS)
            gate = g_ref[rows, ls].astype(F32)
            o_ref[rows, ls] = (o * rms * ng * (gate * jax.nn.sigmoid(gate))).astype(o_ref.dtype)
        return carry

    for blk in range(tb // R):
        local(blk)
    lax.fori_loop(0, tb // C, carry_state, 0, unroll=2)


def _hgrn2(proj, lb_logits, norm_g, B, S, layer, tb=512):
    T = B * S
    nsb = S // tb

    def spec(k):
        return pl.BlockSpec((tb, HG_W), lambda b, s, k=k: (b * nsb + s, COL_HG // HG_W + k))

    return pl.pallas_call(
        functools.partial(_hgrn2_kernel, layer=layer, tb=tb),
        out_shape=jax.ShapeDtypeStruct((T, HG_W), BF16),
        grid=(B, nsb),
        in_specs=[
            _const_spec(lb_logits.shape),
            _const_spec((1, HG_DIM)),
            spec(0), spec(1), spec(2), spec(3),
        ],
        out_specs=pl.BlockSpec((tb, HG_W), lambda b, s: (b * nsb + s, 0)),
        scratch_shapes=[pltpu.VMEM((HG_HEADS, HG_DIM, HG_DIM), F32),
                        pltpu.VMEM((tb, HG_W), F32),
                        pltpu.VMEM((tb, HG_W), BF16),
                        pltpu.VMEM((tb // HG_CHUNK, 1, HG_W), F32),
                        pltpu.VMEM((tb // HG_CHUNK, HG_HEADS, HG_DIM, HG_DIM), F32)],
        compiler_params=_cparams(("parallel", "arbitrary")),
        name="hgrn2",
    )(lb_logits, norm_g, proj, proj, proj, proj)


LOG2E = 1.4426950408889634
SB_DEAD = -160.0


def _sb_kernel(q_ref, k_ref, v_ref, o_ref, acc_ref, run_ref, *, npairs, tq):
    i = pl.program_id(1)
    BLK = SB_BLOCK
    W2 = 2 * BLK
    ndiag = tq // BLK
    scale = SB_HEAD_DIM ** -0.5
    lane = lax.broadcasted_iota(jnp.int32, (BLK, LANES), 1)
    head0 = lane < SB_HEAD_DIM
    zrow = lax.broadcasted_iota(jnp.int32, (tq, W2), 0)
    zcol = lax.broadcasted_iota(jnp.int32, (tq, W2), 1) & (BLK - 1)
    wr = lax.broadcasted_iota(jnp.int32, (W2, W2), 0) & (BLK - 1)
    wc = lax.broadcasted_iota(jnp.int32, (W2, W2), 1)
    cw = ((wc >= BLK) | (wr > wc)).astype(BF16)
    qs = q_ref[...] * jnp.asarray(scale, q_ref.dtype)

    def masked_kv(ref, j, p):
        x = ref[pl.ds(pl.multiple_of(j * BLK, BLK), BLK), p * LANES:(p + 1) * LANES]
        zero = jnp.zeros_like(x)
        return jnp.concatenate([jnp.where(head0, x, zero), jnp.where(head0, zero, x)], axis=0)

    def step(js, ds):
        nb = len(js)
        zs = [[_dot_nt(qs[:, p * LANES:(p + 1) * LANES], masked_kv(k_ref, js[b], p)) * LOG2E
               for p in range(npairs)] for b in range(nb)]
        ws = []
        for b in range(nb):
            before = None if ds[b] is None else (zcol + ds[b] * BLK) < zrow
            lbs, hls = [], []
            for p in range(npairs):
                z = zs[b][p]
                l2 = jnp.log2(1.0 + jnp.exp2(-jnp.abs(z)))
                lb = jnp.minimum(z, 0.0) - l2
                lk = lb - z
                if before is not None:
                    lk = jnp.where(before, lk, 0.0)
                hi = lk.astype(BF16)
                lo = (lk - hi.astype(F32)).astype(BF16)
                lbs.append(lb)
                hls.append(jnp.concatenate([hi[:, :BLK], lo[:, :BLK]], axis=1))
                hls.append(jnp.concatenate([hi[:, BLK:], lo[:, BLK:]], axis=1))
            cs = _dot(jnp.concatenate(hls, axis=0), cw)
            wb = []
            for p in range(npairs):
                c0 = cs[(2 * p) * tq:(2 * p + 1) * tq]
                c1 = cs[(2 * p + 1) * tq:(2 * p + 2) * tq]
                cum = jnp.concatenate([c0[:, :BLK], c1[:, :BLK]], axis=1)
                rsum = jnp.concatenate([c0[:, BLK:], c1[:, BLK:]], axis=1)
                run = run_ref[p]
                w = jnp.exp2(lbs[p] + (run + cum))
                if before is not None:
                    w = jnp.where(before, w, 0.0)
                run_ref[p] = run + rsum
                wb.append(w.astype(BF16))
            ws.append(wb)
        for p in range(npairs):
            o = acc_ref[p]
            for b in range(nb):
                o = o + _dot(ws[b][p], masked_kv(v_ref, js[b], p))
            acc_ref[p] = o

    acc_ref[...] = jnp.zeros_like(acc_ref)
    run_ref[...] = jnp.zeros_like(run_ref)
    step([i * ndiag + d for d in reversed(range(ndiag))], list(reversed(range(ndiag))))

    def live():
        return jnp.max(run_ref[...]) > SB_DEAD

    def cond(c):
        return jnp.logical_and(c[0] < i, c[1])

    def body(c):
        base = (i - 1 - c[0]) * ndiag
        step([base + d for d in reversed(range(ndiag))], [None] * ndiag)
        return c[0] + 1, live()

    lax.while_loop(cond, body, (jnp.int32(0), live()))
    for p in range(npairs):
        o_ref[:, p * LANES:(p + 1) * LANES] = acc_ref[p].astype(o_ref.dtype)


def _sb_attn(proj, B, S, npairs=SB_W // LANES, tq=256):
    T = B * S
    nq = S // tq
    wblk = npairs * LANES
    ng = SB_W // wblk
    cq = COL_SB // wblk
    ck = (COL_SB + SB_W) // wblk
    cv = (COL_SB + 2 * SB_W) // wblk
    return pl.pallas_call(
        functools.partial(_sb_kernel, npairs=npairs, tq=tq),
        out_shape=jax.ShapeDtypeStruct((T, SB_W), BF16),
        grid=(B * ng, nq),
        in_specs=[
            pl.BlockSpec((tq, wblk), lambda g, i: ((g // ng) * nq + i, cq + g % ng)),
            pl.BlockSpec((S, wblk), lambda g, i: (g // ng, ck + g % ng)),
            pl.BlockSpec((S, wblk), lambda g, i: (g // ng, cv + g % ng)),
        ],
        out_specs=pl.BlockSpec((tq, wblk), lambda g, i: ((g // ng) * nq + i, g % ng)),
        scratch_shapes=[pltpu.VMEM((npairs, tq, LANES), F32),
                        pltpu.VMEM((npairs, tq, 2 * SB_BLOCK), F32)],
        compiler_params=_cparams(("parallel", "arbitrary")),
        name="sb_attn",
    )(proj, proj, proj)


GATE_BLK = 512


def _mix_kernel(x_ref, gin_ref, bin_ref, oa_ref, ob_ref, ga0_ref, ga1_ref, gb0_ref, gb1_ref,
                wa_ref, wb_ref, wm_ref, g1_ref, b1_ref, o_ref):
    rs = x_ref.shape[0] // MIX_SUBTILES
    rows = [slice(k * rs, (k + 1) * rs) for k in range(MIX_SUBTILES)]
    halves = [(slice(0, GATE_BLK), ga0_ref, gb0_ref), (slice(GATE_BLK, 2 * GATE_BLK), ga1_ref, gb1_ref)]
    yab = [[(_dot(oa_ref[r, :], wa_ref[:, cs]), _dot(ob_ref[r, :], wb_ref[:, cs])) for cs, _, _ in halves]
           for r in rows]
    ys = []
    for k, r in enumerate(rows):
        y = None
        for (cs, ga_ref, gb_ref), (ya, yb) in zip(halves, yab[k]):
            merged = (jax.nn.sigmoid(ga_ref[r, :].astype(F32)) * ya
                      + jax.nn.sigmoid(gb_ref[r, :].astype(F32)) * yb)
            part = _dot(merged.astype(BF16), wm_ref[cs, :])
            y = part if y is None else y + part
        ys.append(y)
    for k, r in enumerate(rows):
        h0 = _layer_norm(x_ref[r, :], gin_ref[...], bin_ref[...])
        o_ref[r, :] = _layer_norm(DN_ALPHA * h0 + ys[k], g1_ref[...], b1_ref[...])


MIX_SUBTILES = 4


def _mix(x2, gin, bin_, oa, ob, proj, wa, wb, wm, g1, b1, tm=1024):
    T = x2.shape[0]
    c0 = COL_GATE // GATE_BLK

    def gate_spec(k):
        return pl.BlockSpec((tm, GATE_BLK), lambda i, k=k: (i, c0 + k))

    return pl.pallas_call(
        _mix_kernel,
        out_shape=jax.ShapeDtypeStruct((T, D_MODEL), F32),
        grid=(T // tm,),
        in_specs=[
            pl.BlockSpec((tm, D_MODEL), lambda i: (i, 0)),
            _const_spec((1, D_MODEL)), _const_spec((1, D_MODEL)),
            pl.BlockSpec((tm, HG_W), lambda i: (i, 0)),
            pl.BlockSpec((tm, SB_W), lambda i: (i, 0)),
            gate_spec(0), gate_spec(1), gate_spec(2), gate_spec(3),
            _const_spec((HG_W, D_MODEL)), _const_spec((SB_W, D_MODEL)), _const_spec((D_MODEL, D_MODEL)),
            _const_spec((1, D_MODEL)), _const_spec((1, D_MODEL)),
        ],
        out_specs=pl.BlockSpec((tm, D_MODEL), lambda i: (i, 0)),
        compiler_params=_cparams(("parallel",)),
        name="mix",
    )(x2, gin, bin_, oa, ob, proj, proj, proj, proj, wa, wb, wm, g1, b1)


def _xa_kv_kernel(m_ref, wk_ref, wv_ref, k_ref, v_ref):
    m = m_ref[0].astype(BF16)
    k_ref[0] = _dot(m, wk_ref[...]).astype(BF16)
    v_ref[0] = _dot(m, wv_ref[...]).astype(BF16)


def _xa_kv(mem, wk, wv):
    B, M, _ = mem.shape
    W = wk.shape[1]
    return pl.pallas_call(
        _xa_kv_kernel,
        out_shape=(jax.ShapeDtypeStruct((B, M, W), BF16), jax.ShapeDtypeStruct((B, M, W), BF16)),
        grid=(B,),
        in_specs=[pl.BlockSpec((1, M, D_MODEL), lambda b: (b, 0, 0)),
                  _const_spec(wk.shape), _const_spec(wv.shape)],
        out_specs=(pl.BlockSpec((1, M, W), lambda b: (b, 0, 0)),
                   pl.BlockSpec((1, M, W), lambda b: (b, 0, 0))),
        compiler_params=_cparams(("parallel",)),
        name="xa_kv",
    )(mem, wk, wv)


XA_SUBTILES = 4


def _xattn_kernel(h_ref, k_ref, v_ref, wq_ref, wo_ref, g_ref, b_ref, o_ref):
    nsub = XA_SUBTILES
    rs = h_ref.shape[0] // nsub
    rows = [slice(k * rs, (k + 1) * rs) for k in range(nsub)]
    heads = [slice(hd * XA_HEAD_DIM, (hd + 1) * XA_HEAD_DIM) for hd in range(XA_HEADS)]
    scale = XA_HEAD_DIM ** -0.5
    hs = [h_ref[r, :] for r in rows]
    qs = [_dot(h.astype(BF16), wq_ref[...]).astype(BF16) for h in hs]
    ss = [[_dot_nt(q[:, ls], k_ref[0, :, ls]) * scale for ls in heads] for q in qs]
    os_ = []
    for k in range(nsub):
        outs = []
        for hd, ls in enumerate(heads):
            s = ss[k][hd]
            p = jnp.exp(s - jnp.max(s, axis=-1, keepdims=True))
            p = p / jnp.sum(p, axis=-1, keepdims=True)
            outs.append(_dot(p.astype(BF16), v_ref[0, :, ls]).astype(BF16))
        os_.append(jnp.concatenate(outs, axis=-1))
    ys = [_dot(o, wo_ref[...]) for o in os_]
    for k in range(nsub):
        o_ref[rows[k], :] = _layer_norm(DN_ALPHA * hs[k] + ys[k], g_ref[...], b_ref[...])


def _xattn(h1, kx, vx, wq, wo, g2, b2, B, S, tm=1024):
    T = B * S
    M = kx.shape[1]
    W = kx.shape[2]
    nsb = S // tm
    return pl.pallas_call(
        _xattn_kernel,
        out_shape=jax.ShapeDtypeStruct((T, D_MODEL), F32),
        grid=(B, nsb),
        in_specs=[
            pl.BlockSpec((tm, D_MODEL), lambda b, s: (b * nsb + s, 0)),
            pl.BlockSpec((1, M, W), lambda b, s: (b, 0, 0)),
            pl.BlockSpec((1, M, W), lambda b, s: (b, 0, 0)),
            _const_spec(wq.shape), _const_spec(wo.shape),
            _const_spec((1, D_MODEL)), _const_spec((1, D_MODEL)),
        ],
        out_specs=pl.BlockSpec((tm, D_MODEL), lambda b, s: (b * nsb + s, 0)),
        compiler_params=_cparams(("parallel", "arbitrary")),
        name="xattn",
    )(h1, kx, vx, wq, wo, g2, b2)


MOE_TM = 1024
MOE_M1 = 160
MOE_M2 = 64
MOE_EPS = 4
SUBLANES = 8
GROUP_ROWS = 16


def _hdot(a, b):
    return jnp.dot(a, b, preferred_element_type=F32, precision=lax.Precision.HIGHEST)


def _split2(a):
    hi = a.astype(BF16)
    return hi, (a - hi.astype(F32)).astype(BF16)


def _first_argmax0(vals, row, height):
    m = jnp.max(vals, axis=0, keepdims=True)
    idx = jnp.min(jnp.where(vals == m, row, height), axis=0, keepdims=True)
    return m, idx


def _select_rows(shape, rows):
    r = lax.broadcasted_iota(jnp.int32, shape, 0)
    out = jnp.zeros(shape, F32)
    for k, v in enumerate(rows):
        out = jnp.where(r == k, v, out)
    return out


def _route_kernel(h_ref, wt_ref, bg_ref, be_ref, rinfo_ref, cinfo_ref, meta_ref):
    tm = h_ref.shape[0]
    h = h_ref[...]
    neg = jnp.float32(-jnp.inf)
    rg = lax.broadcasted_iota(jnp.int32, (GROUP_ROWS, tm), 0)
    re = lax.broadcasted_iota(jnp.int32, (N_EXPERTS, tm), 0)
    nw = GROUP_ROWS + N_EXPERTS
    hh, hl = _split2(h)
    wh, wl = _split2(wt_ref[...])
    lg = _dot_nt(jnp.concatenate([wh, wl], axis=0), hh)
    lg = lg[:nw] + (lg[nw:] + _dot_nt(wh, hl))
    gl = jnp.where(rg < N_GROUPS, lg[:GROUP_ROWS] + bg_ref[...], neg)
    el = lg[GROUP_ROWS:] + be_ref[...]
    gm, g_idx = _first_argmax0(gl, rg, GROUP_ROWS)
    g_top = 1.0 / jnp.sum(jnp.exp(gl - gm), axis=0, keepdims=True)
    ml = jnp.where(re // EXPERTS_PER_GROUP == g_idx, el, neg)
    m1, i1 = _first_argmax0(ml, re, N_EXPERTS)
    ml2 = jnp.where(re == i1, neg, ml)
    m2, i2 = _first_argmax0(ml2, re, N_EXPERTS)
    e2 = jnp.exp(m2 - m1)
    gate0 = g_top / (1.0 + e2)
    gate1 = g_top * e2 / (1.0 + e2)

    oh0 = jnp.where(re == i1, 1.0, 0.0)
    oh1 = jnp.where(re == i2, 1.0, 0.0)
    cnt0 = jnp.sum(oh0, axis=1, keepdims=True)
    cnt1 = jnp.sum(oh1, axis=1, keepdims=True)
    er = lax.broadcasted_iota(jnp.int32, (N_EXPERTS, N_EXPERTS), 0)
    ec = lax.broadcasted_iota(jnp.int32, (N_EXPERTS, N_EXPERTS), 1)
    seg = _hdot(jnp.where(ec < er, 1.0, 0.0), jnp.broadcast_to(cnt0 + cnt1, (N_EXPERTS, LANES)))[:, 0:1]
    tr = lax.broadcasted_iota(jnp.int32, (tm, tm), 0)
    tc = lax.broadcasted_iota(jnp.int32, (tm, tm), 1)
    earlier = jnp.where(tr < tc, 1.0, 0.0).astype(BF16)
    cum = _dot(jnp.concatenate([oh0, oh1], axis=0).astype(BF16), earlier)
    slot0 = jnp.sum(oh0 * (seg + cum[:N_EXPERTS]), axis=0, keepdims=True)
    slot1 = jnp.sum(oh1 * (seg + cnt0 + cum[N_EXPERTS:]), axis=0, keepdims=True)

    info = (slot0, slot1, gate0, gate1)
    rinfo_ref[0] = _select_rows((SUBLANES, tm), info)
    cinfo_ref[...] = _select_rows((LANES, tm), info).T

    ohs = jnp.concatenate([oh0 + oh1, jnp.zeros((LANES - N_EXPERTS, tm), F32)], axis=0).astype(BF16)
    cnt_row = _dot_nt(jnp.ones((SUBLANES, tm), BF16), ohs)
    lr = lax.broadcasted_iota(jnp.int32, (LANES, LANES), 0)
    lc = lax.broadcasted_iota(jnp.int32, (LANES, LANES), 1)
    seg_row = _hdot(cnt_row, jnp.where(lr < lc, 1.0, 0.0))
    r8 = lax.broadcasted_iota(jnp.int32, (SUBLANES, LANES), 0)
    meta_ref[0] = jnp.where(r8 == 0, seg_row, jnp.where(r8 == 1, cnt_row, 0.0)).astype(jnp.int32)


def _route(h2, wt, bg, be, tm=MOE_TM):
    T = h2.shape[0]
    nt = T // tm
    return pl.pallas_call(
        _route_kernel,
        out_shape=(jax.ShapeDtypeStruct((nt, SUBLANES, tm), F32),
                   jax.ShapeDtypeStruct((T, LANES), F32),
                   jax.ShapeDtypeStruct((nt, SUBLANES, LANES), jnp.int32)),
        grid=(nt,),
        in_specs=[pl.BlockSpec((tm, D_MODEL), lambda i: (i, 0)),
                  _const_spec(wt.shape), _const_spec(bg.shape), _const_spec(be.shape)],
        out_specs=(pl.BlockSpec((1, SUBLANES, tm), lambda i: (i, 0, 0)),
                   pl.BlockSpec((tm, LANES), lambda i: (i, 0)),
                   pl.BlockSpec((1, SUBLANES, LANES), lambda i: (i, 0, 0))),
        compiler_params=_cparams(("parallel",)),
        name="moe_route",
    )(h2, wt, bg, be)


def _moe_kernel(meta_ref, h_ref, rinfo_ref, cinfo_ref, w1_ref, w3_ref, w2_ref, g_ref, b_ref,
                o_ref, hb_ref, acc_ref):
    t = pl.program_id(0)
    eg = pl.program_id(1)
    tm = h_ref.shape[0]

    @pl.when(eg == 0)
    def _():
        hb_ref[...] = h_ref[...].astype(BF16)
        acc_ref[...] = jnp.zeros_like(acc_ref)

    def chunk(j, r0, m, end):
        info = rinfo_ref[0]
        s = lax.broadcasted_iota(jnp.int32, (m, tm), 0) + r0
        pick = (s == info[0:1].astype(jnp.int32)) | (s == info[1:2].astype(jnp.int32))
        x = _dot(jnp.where(pick, 1.0, 0.0).astype(BF16), hb_ref[...]).astype(BF16)
        a = _dot(x, w1_ref[j])
        b = _dot(x, w3_ref[j])
        hid = (a * jax.nn.sigmoid(a) * b).astype(BF16)
        y = _dot(hid, w2_ref[j]).astype(BF16)
        ci = cinfo_ref[...]
        sc = lax.broadcasted_iota(jnp.int32, (tm, m), 1) + r0
        live = sc < end
        q = (jnp.where((sc == ci[:, 0:1].astype(jnp.int32)) & live, ci[:, 2:3], 0.0)
             + jnp.where((sc == ci[:, 1:2].astype(jnp.int32)) & live, ci[:, 3:4], 0.0))
        acc_ref[...] += _dot(q.astype(BF16), y)

    for j in range(MOE_EPS):
        e = eg * MOE_EPS + j
        start = meta_ref[(2 * t) * N_EXPERTS + e]
        end = start + meta_ref[(2 * t + 1) * N_EXPERTS + e]

        @pl.when(end > start)
        def _(j=j, start=start, end=end):
            chunk(j, start, MOE_M1, end)

            def extra(k, c):
                chunk(j, start + MOE_M1 + k * MOE_M2, MOE_M2, end)
                return c

            lax.fori_loop(0, (jnp.maximum(end - start - MOE_M1, 0) + MOE_M2 - 1) // MOE_M2, extra, 0)

    @pl.when(eg == N_EXPERTS // MOE_EPS - 1)
    def _():
        o_ref[...] = _layer_norm(DN_ALPHA * h_ref[...] + acc_ref[...], g_ref[...], b_ref[...])


def _moe(h2, rinfo, cinfo, meta, w1, w3, w2, g3, b3, tm=MOE_TM):
    T = h2.shape[0]
    return pl.pallas_call(
        _moe_kernel,
        out_shape=jax.ShapeDtypeStruct((T, D_MODEL), F32),
        grid_spec=pltpu.PrefetchScalarGridSpec(
            num_scalar_prefetch=1,
            grid=(T // tm, N_EXPERTS // MOE_EPS),
            in_specs=[
                pl.BlockSpec((tm, D_MODEL), lambda i, e, m: (i, 0)),
                pl.BlockSpec((1, SUBLANES, tm), lambda i, e, m: (i, 0, 0)),
                pl.BlockSpec((tm, LANES), lambda i, e, m: (i, 0)),
                pl.BlockSpec((MOE_EPS, D_MODEL, EXPERT_FF), lambda i, e, m: (e, 0, 0)),
                pl.BlockSpec((MOE_EPS, D_MODEL, EXPERT_FF), lambda i, e, m: (e, 0, 0)),
                pl.BlockSpec((MOE_EPS, EXPERT_FF, D_MODEL), lambda i, e, m: (e, 0, 0)),
                pl.BlockSpec((1, D_MODEL), lambda i, e, m: (0, 0)),
                pl.BlockSpec((1, D_MODEL), lambda i, e, m: (0, 0)),
            ],
            out_specs=pl.BlockSpec((tm, D_MODEL), lambda i, e, m: (i, 0)),
            scratch_shapes=[pltpu.VMEM((tm, D_MODEL), BF16), pltpu.VMEM((tm, D_MODEL), F32)],
        ),
        compiler_params=_cparams(("parallel", "arbitrary")),
        name="moe",
    )(meta, h2, rinfo, cinfo, w1, w3, w2, g3, b3)


def _moe_layer(h2, wg, bg, we, be, w1, w3, w2, g3, b3, tm=MOE_TM):
    pad = GROUP_ROWS - N_GROUPS
    wt = jnp.concatenate([jnp.pad(wg.astype(F32).T, ((0, pad), (0, 0))), we.astype(F32).T], axis=0)
    bgc = jnp.pad(bg.astype(F32).reshape(-1, 1), ((0, pad), (0, 0)))
    rinfo, cinfo, meta = _route(h2, wt, bgc, be.astype(F32).reshape(-1, 1), tm=tm)
    meta = meta[:, :2, :N_EXPERTS].reshape(-1)
    return _moe(h2, rinfo, cinfo, meta, w1, w3, w2, g3, b3, tm=tm)


def kernel(x, mem, ln_in_g, ln_in_b, w_in, hg_lb_logits, hg_norm_g, w_branch_a, w_branch_b, w_mix_out,
           ln1_g, ln1_b, xa_wq, xa_wk, xa_wv, xa_wo, ln2_g, ln2_b, router_wg, router_bg, router_we,
           router_be, moe_w1, moe_w3, moe_w2, ln3_g, ln3_b):
    B, S, D = x.shape
    T = B * S
    row = lambda a: a.reshape(1, -1).astype(F32)
    x2 = x.reshape(T, D)
    gin, bin_ = row(ln_in_g), row(ln_in_b)
    assert w_in.shape[0] == DEPTH
    l = 0
    proj = _in_proj(x2, gin, bin_, w_in[l].astype(BF16))
    oa = _hgrn2(proj, hg_lb_logits.astype(F32), row(hg_norm_g[l]), B, S, l)
    ob = _sb_attn(proj, B, S)
    h1 = _mix(x2, gin, bin_, oa, ob, proj, w_branch_a[l].astype(BF16), w_branch_b[l].astype(BF16),
              w_mix_out[l].astype(BF16), row(ln1_g[l]), row(ln1_b[l]))
    kx, vx = _xa_kv(mem, xa_wk[l].astype(BF16), xa_wv[l].astype(BF16))
    h2 = _xattn(h1, kx, vx, xa_wq[l].astype(BF16), xa_wo[l].astype(BF16), row(ln2_g[l]), row(ln2_b[l]), B, S)
    out = _moe_layer(h2, router_wg[l], router_bg[l], router_we[l], router_be[l],
                     moe_w1[l].astype(BF16), moe_w3[l].astype(BF16), moe_w2[l].astype(BF16),
                     row(ln3_g[l]), row(ln3_b[l]))
    return out.reshape(B, S, D)
```

```python
import functools

import jax
import jax.numpy as jnp
from jax import lax
from jax.experimental import pallas as pl
from jax.experimental.pallas import tpu as pltpu

F32 = jnp.float32
BF16 = jnp.bfloat16

D_MODEL = 1024
HG_HEADS = 4
HG_DIM = 128
HG_W = HG_HEADS * HG_DIM
HG_CHUNK = 64
HG_LOCAL_CHUNKS = 4
SB_HEADS = 8
SB_HEAD_DIM = 64
SB_W = SB_HEADS * SB_HEAD_DIM
SB_BLOCK = 128
XA_HEADS = 4
XA_HEAD_DIM = D_MODEL // XA_HEADS
N_GROUPS = 4
EXPERTS_PER_GROUP = 4
N_EXPERTS = N_GROUPS * EXPERTS_PER_GROUP
EXPERT_FF = 512
N_IN = HG_W * 4 + SB_W * 3 + D_MODEL * 2
COL_HG = 0
COL_SB = 4 * HG_W
COL_GATE = COL_SB + 3 * SB_W
LN_EPS = 1e-5
RMS_EPS = 1e-6
DEPTH = 1
DN_ALPHA = (2 * DEPTH) ** 0.25

LANES = 128
VMEM_LIMIT = 56 * 1024 * 1024


def _cparams(sem, flags=None):
    return pltpu.CompilerParams(dimension_semantics=sem, vmem_limit_bytes=VMEM_LIMIT, flags=flags)


def _layer_norm(x, g, b):
    mu = jnp.mean(x, axis=-1, keepdims=True)
    xc = x - mu
    var = jnp.mean(xc * xc, axis=-1, keepdims=True)
    return xc * lax.rsqrt(var + LN_EPS) * g + b


def _dot(a, b):
    return jnp.dot(a, b, preferred_element_type=F32)


def _dot_nt(a, b):
    return lax.dot_general(a, b, (((1,), (1,)), ((), ())), preferred_element_type=F32)


def _dot_tn(a, b):
    return lax.dot_general(a, b, (((0,), (0,)), ((), ())), preferred_element_type=F32)


def _split3(a):
    hi = a.astype(BF16)
    r1 = a - hi.astype(F32)
    mid = r1.astype(BF16)
    lo = (r1 - mid.astype(F32)).astype(BF16)
    return hi, mid, lo


def _const_spec(shape):
    nd = len(shape)
    return pl.BlockSpec(shape, lambda *_: (0,) * nd)


def _in_proj_kernel(x_ref, g_ref, b_ref, w_ref, o_ref, *, col_chunk):
    h = _layer_norm(x_ref[...], g_ref[...], b_ref[...]).astype(BF16)
    for c in range(N_IN // col_chunk):
        sl = slice(c * col_chunk, (c + 1) * col_chunk)
        o_ref[:, sl] = _dot(h, w_ref[:, sl]).astype(BF16)


def _in_proj(x2, g, b, w_bf, tm=512, col_chunk=512):
    T = x2.shape[0]
    return pl.pallas_call(
        functools.partial(_in_proj_kernel, col_chunk=col_chunk),
        out_shape=jax.ShapeDtypeStruct((T, N_IN), BF16),
        grid=(T // tm,),
        in_specs=[
            pl.BlockSpec((tm, D_MODEL), lambda i: (i, 0)),
            _const_spec((1, D_MODEL)),
            _const_spec((1, D_MODEL)),
            _const_spec((D_MODEL, N_IN)),
        ],
        out_specs=pl.BlockSpec((tm, N_IN), lambda i: (i, 0)),
        compiler_params=_cparams(("parallel",)),
        name="in_proj",
    )(x2, g, b, w_bf)


def _hgrn2_kernel(lbl_ref, ng_ref, q_ref, f_ref, i_ref, g_ref, o_ref,
                  state_ref, intra_ref, qs_ref, dl_ref, u_ref, *, layer, tb):
    C = HG_CHUNK
    H2 = C // 2

    @pl.when(pl.program_id(1) == 0)
    def _():
        state_ref[...] = jnp.zeros_like(state_ref)

    lbl = lbl_ref[...]
    ex = jnp.exp(lbl - jnp.max(lbl, axis=0, keepdims=True))
    lb = jnp.sum(ex[: layer + 1], axis=0, keepdims=True) / jnp.sum(ex, axis=0, keepdims=True)

    R = HG_LOCAL_CHUNKS * C
    row = lax.broadcasted_iota(jnp.int32, (R, R), 0)
    col = lax.broadcasted_iota(jnp.int32, (R, R), 1)
    same_chunk = (row // C) == (col // C)
    tri = (same_chunk & (col <= row)).astype(BF16)
    m_diag = ((row // H2) == (col // H2)) & (col <= row)
    m_cross = same_chunk & ((row % C) >= H2) & ((col % C) < H2)
    rrow = lax.broadcasted_iota(jnp.int32, (R, HG_W), 0)
    ng = ng_ref[...]
    heads = [slice(h * HG_DIM, (h + 1) * HG_DIM) for h in range(HG_HEADS)]

    def row_of_period(x, offset, period):
        out = jnp.broadcast_to(x[offset:offset + 1, :], x.shape)
        for k in range(1, R // period):
            out = jnp.where(rrow >= k * period, x[k * period + offset:k * period + offset + 1, :], out)
        return out

    def local(blk):
        rows = slice(blk * R, (blk + 1) * R)
        forget = lb + (1.0 - lb) * jax.nn.sigmoid(f_ref[rows, :].astype(F32))
        kk = 1.0 - forget
        hi, mid, lo = _split3(jnp.log(forget))
        b = _dot(tri, hi) + _dot(tri, mid) + _dot(tri, lo)
        q = q_ref[rows, :].astype(F32)
        d1 = b - row_of_period(b, H2 // 2 - 1, H2)
        q1 = (q * jnp.exp(d1)).astype(BF16)
        k1 = (kk * jnp.exp(-d1)).astype(BF16)
        e3 = jnp.exp(-jnp.abs(b - row_of_period(b, H2 - 1, C)))
        q3 = (q * e3).astype(BF16)
        k3 = (kk * e3).astype(BF16)
        kd = (kk * jnp.exp(row_of_period(b, C - 1, C) - b)).astype(BF16)
        qs_ref[rows, :] = (q * jnp.exp(b)).astype(BF16)
        for k in range(HG_LOCAL_CHUNKS):
            dl_ref[blk * HG_LOCAL_CHUNKS + k] = jnp.exp(b[k * C + C - 1:(k + 1) * C, :])
        s1 = [_dot_nt(q1[:, ls], k1[:, ls]) for ls in heads]
        s3 = [_dot_nt(q3[:, ls], k3[:, ls]) for ls in heads]
        s = [jnp.where(m_diag, a1, jnp.where(m_cross, a3, 0.0)).astype(BF16) for a1, a3 in zip(s1, s3)]
        for h, ls in enumerate(heads):
            v = i_ref[rows, ls]
            intra_ref[rows, ls] = _dot(s[h], v)
            for k in range(HG_LOCAL_CHUNKS):
                cs = slice(k * C, (k + 1) * C)
                u_ref[blk * HG_LOCAL_CHUNKS + k, h] = _dot_tn(v[cs], kd[cs, ls])

    def carry_state(c, carry):
        rows = pl.ds(pl.multiple_of(c * C, C), C)
        dl = dl_ref[c]
        for h, ls in enumerate(heads):
            st = state_ref[h]
            o = intra_ref[rows, ls] + _dot_nt(qs_ref[rows, ls], st.astype(BF16))
            state_ref[h] = st * dl[:, ls] + u_ref[c, h]
            rms = lax.rsqrt(jnp.mean(o * o, axis=-1, keepdims=True) + RMS_EPS)
            gate = g_ref[rows, ls].astype(F32)
            o_ref[rows, ls] = (o * rms * ng * (gate * jax.nn.sigmoid(gate))).astype(o_ref.dtype)
        return carry

    for blk in range(tb // R):
        local(blk)
    lax.fori_loop(0, tb // C, carry_state, 0, unroll=True)


def _hgrn2(proj, lb_logits, norm_g, B, S, layer, tb=512):
    T = B * S
    nsb = S // tb

    def spec(k):
        return pl.BlockSpec((tb, HG_W), lambda b, s, k=k: (b * nsb + s, COL_HG // HG_W + k))

    return pl.pallas_call(
        functools.partial(_hgrn2_kernel, layer=layer, tb=tb),
        out_shape=jax.ShapeDtypeStruct((T, HG_W), BF16),
        grid=(B, nsb),
        in_specs=[
            _const_spec(lb_logits.shape),
            _const_spec((1, HG_DIM)),
            spec(0), spec(1), spec(2), spec(3),
        ],
        out_specs=pl.BlockSpec((tb, HG_W), lambda b, s: (b * nsb + s, 0)),
        scratch_shapes=[pltpu.VMEM((HG_HEADS, HG_DIM, HG_DIM), F32),
                        pltpu.VMEM((tb, HG_W), F32),
                        pltpu.VMEM((tb, HG_W), BF16),
                        pltpu.VMEM((tb // HG_CHUNK, 1, HG_W), F32),
                        pltpu.VMEM((tb // HG_CHUNK, HG_HEADS, HG_DIM, HG_DIM), F32)],
        compiler_params=_cparams(("parallel", "arbitrary")),
        name="hgrn2",
    )(lb_logits, norm_g, proj, proj, proj, proj)


LOG2E = 1.4426950408889634
SB_DEAD = -160.0


def _sb_kernel(q_ref, k_ref, v_ref, o_ref, acc_ref, run_ref, *, npairs, tq):
    i = pl.program_id(1)
    BLK = SB_BLOCK
    W2 = 2 * BLK
    ndiag = tq // BLK
    scale = SB_HEAD_DIM ** -0.5
    lane = lax.broadcasted_iota(jnp.int32, (BLK, LANES), 1)
    head0 = lane < SB_HEAD_DIM
    zrow = lax.broadcasted_iota(jnp.int32, (tq, W2), 0)
    zcol = lax.broadcasted_iota(jnp.int32, (tq, W2), 1) & (BLK - 1)
    wr = lax.broadcasted_iota(jnp.int32, (W2, W2), 0) & (BLK - 1)
    wc = lax.broadcasted_iota(jnp.int32, (W2, W2), 1)
    cw = ((wc >= BLK) | (wr > wc)).astype(BF16)
    qs = q_ref[...] * jnp.asarray(scale, q_ref.dtype)

    def masked_kv(ref, j, p):
        x = ref[pl.ds(pl.multiple_of(j * BLK, BLK), BLK), p * LANES:(p + 1) * LANES]
        zero = jnp.zeros_like(x)
        return jnp.concatenate([jnp.where(head0, x, zero), jnp.where(head0, zero, x)], axis=0)

    def step(js, ds):
        nb = len(js)
        rsl = [slice((ds[b] or 0) * BLK, tq) for b in range(nb)]
        zs = [[_dot_nt(qs[rsl[b], p * LANES:(p + 1) * LANES], masked_kv(k_ref, js[b], p)) * LOG2E
               for p in range(npairs)] for b in range(nb)]
        ws = []
        for b in range(nb):
            nr = tq - rsl[b].start
            before = None if ds[b] is None else ((zcol + ds[b] * BLK) < zrow)[rsl[b]]
            lbs, hls = [], []
            for p in range(npairs):
                z = zs[b][p]
                l2 = jnp.log2(1.0 + jnp.exp2(-jnp.abs(z)))
                lb = jnp.minimum(z, 0.0) - l2
                lk = lb - z
                if before is not None:
                    lk = jnp.where(before, lk, 0.0)
                hi = lk.astype(BF16)
                lo = (lk - hi.astype(F32)).astype(BF16)
                lbs.append(lb)
                hls.append(jnp.concatenate([hi[:, :BLK], lo[:, :BLK]], axis=1))
                hls.append(jnp.concatenate([hi[:, BLK:], lo[:, BLK:]], axis=1))
            cs = _dot(jnp.concatenate(hls, axis=0), cw)
            wb = []
            for p in range(npairs):
                c0 = cs[(2 * p) * nr:(2 * p + 1) * nr]
                c1 = cs[(2 * p + 1) * nr:(2 * p + 2) * nr]
                cum = jnp.concatenate([c0[:, :BLK], c1[:, :BLK]], axis=1)
                rsum = jnp.concatenate([c0[:, BLK:], c1[:, BLK:]], axis=1)
                run = run_ref[p, rsl[b], :]
                w = jnp.exp2(lbs[p] + (run + cum))
                if before is not None:
                    w = jnp.where(before, w, 0.0)
                run_ref[p, rsl[b], :] = run + rsum
                wb.append(w.astype(BF16))
            ws.append(wb)
        for p in range(npairs):
            for b in range(nb):
                acc_ref[p, rsl[b], :] += _dot(ws[b][p], masked_kv(v_ref, js[b], p))

    acc_ref[...] = jnp.zeros_like(acc_ref)
    run_ref[...] = jnp.zeros_like(run_ref)
    step([i * ndiag + d for d in reversed(range(ndiag))], list(reversed(range(ndiag))))

    def live():
        return jnp.max(run_ref[...]) > SB_DEAD

    def cond(c):
        return jnp.logical_and(c[0] < i, c[1])

    def body(c):
        base = (i - 1 - c[0]) * ndiag
        step([base + d for d in reversed(range(ndiag))], [None] * ndiag)
        return c[0] + 1, live()

    lax.while_loop(cond, body, (jnp.int32(0), live()))
    for p in range(npairs):
        o_ref[:, p * LANES:(p + 1) * LANES] = acc_ref[p].astype(o_ref.dtype)


def _sb_attn(proj, B, S, npairs=SB_W // LANES, tq=256):
    T = B * S
    nq = S // tq
    wblk = npairs * LANES
    ng = SB_W // wblk
    cq = COL_SB // wblk
    ck = (COL_SB + SB_W) // wblk
    cv = (COL_SB + 2 * SB_W) // wblk
    return pl.pallas_call(
        functools.partial(_sb_kernel, npairs=npairs, tq=tq),
        out_shape=jax.ShapeDtypeStruct((T, SB_W), BF16),
        grid=(B * ng, nq),
        in_specs=[
            pl.BlockSpec((tq, wblk), lambda g, i: ((g // ng) * nq + i, cq + g % ng)),
            pl.BlockSpec((S, wblk), lambda g, i: (g // ng, ck + g % ng)),
            pl.BlockSpec((S, wblk), lambda g, i: (g // ng, cv + g % ng)),
        ],
        out_specs=pl.BlockSpec((tq, wblk), lambda g, i: ((g // ng) * nq + i, g % ng)),
        scratch_shapes=[pltpu.VMEM((npairs, tq, LANES), F32),
                        pltpu.VMEM((npairs, tq, 2 * SB_BLOCK), F32)],
        compiler_params=_cparams(("parallel", "arbitrary")),
        name="sb_attn",
    )(proj, proj, proj)


GATE_BLK = 512


def _mix_kernel(x_ref, gin_ref, bin_ref, oa_ref, ob_ref, ga0_ref, ga1_ref, gb0_ref, gb1_ref,
                wa_ref, wb_ref, wm_ref, g1_ref, b1_ref, o_ref):
    rs = x_ref.shape[0] // MIX_SUBTILES
    rows = [slice(k * rs, (k + 1) * rs) for k in range(MIX_SUBTILES)]
    halves = [(slice(0, GATE_BLK), ga0_ref, gb0_ref), (slice(GATE_BLK, 2 * GATE_BLK), ga1_ref, gb1_ref)]
    yab = [[(_dot(oa_ref[r, :], wa_ref[:, cs]), _dot(ob_ref[r, :], wb_ref[:, cs])) for cs, _, _ in halves]
           for r in rows]
    ys = []
    for k, r in enumerate(rows):
        y = None
        for (cs, ga_ref, gb_ref), (ya, yb) in zip(halves, yab[k]):
            merged = (jax.nn.sigmoid(ga_ref[r, :].astype(F32)) * ya
                      + jax.nn.sigmoid(gb_ref[r, :].astype(F32)) * yb)
            part = _dot(merged.astype(BF16), wm_ref[cs, :])
            y = part if y is None else y + part
        ys.append(y)
    for k, r in enumerate(rows):
        h0 = _layer_norm(x_ref[r, :], gin_ref[...], bin_ref[...])
        o_ref[r, :] = _layer_norm(DN_ALPHA * h0 + ys[k], g1_ref[...], b1_ref[...])


MIX_SUBTILES = 4


def _mix(x2, gin, bin_, oa, ob, proj, wa, wb, wm, g1, b1, tm=1024):
    T = x2.shape[0]
    c0 = COL_GATE // GATE_BLK

    def gate_spec(k):
        return pl.BlockSpec((tm, GATE_BLK), lambda i, k=k: (i, c0 + k))

    return pl.pallas_call(
        _mix_kernel,
        out_shape=jax.ShapeDtypeStruct((T, D_MODEL), F32),
        grid=(T // tm,),
        in_specs=[
            pl.BlockSpec((tm, D_MODEL), lambda i: (i, 0)),
            _const_spec((1, D_MODEL)), _const_spec((1, D_MODEL)),
            pl.BlockSpec((tm, HG_W), lambda i: (i, 0)),
            pl.BlockSpec((tm, SB_W), lambda i: (i, 0)),
            gate_spec(0), gate_spec(1), gate_spec(2), gate_spec(3),
            _const_spec((HG_W, D_MODEL)), _const_spec((SB_W, D_MODEL)), _const_spec((D_MODEL, D_MODEL)),
            _const_spec((1, D_MODEL)), _const_spec((1, D_MODEL)),
        ],
        out_specs=pl.BlockSpec((tm, D_MODEL), lambda i: (i, 0)),
        compiler_params=_cparams(("parallel",)),
        name="mix",
    )(x2, gin, bin_, oa, ob, proj, proj, proj, proj, wa, wb, wm, g1, b1)


def _xa_kv_kernel(m_ref, wk_ref, wv_ref, k_ref, v_ref):
    m = m_ref[0].astype(BF16)
    k_ref[0] = _dot(m, wk_ref[...]).astype(BF16)
    v_ref[0] = _dot(m, wv_ref[...]).astype(BF16)


def _xa_kv(mem, wk, wv):
    B, M, _ = mem.shape
    W = wk.shape[1]
    return pl.pallas_call(
        _xa_kv_kernel,
        out_shape=(jax.ShapeDtypeStruct((B, M, W), BF16), jax.ShapeDtypeStruct((B, M, W), BF16)),
        grid=(B,),
        in_specs=[pl.BlockSpec((1, M, D_MODEL), lambda b: (b, 0, 0)),
                  _const_spec(wk.shape), _const_spec(wv.shape)],
        out_specs=(pl.BlockSpec((1, M, W), lambda b: (b, 0, 0)),
                   pl.BlockSpec((1, M, W), lambda b: (b, 0, 0))),
        compiler_params=_cparams(("parallel",)),
        name="xa_kv",
    )(mem, wk, wv)


XA_SUBTILES = 4


def _xattn_kernel(h_ref, k_ref, v_ref, wq_ref, wo_ref, g_ref, b_ref, o_ref):
    nsub = XA_SUBTILES
    rs = h_ref.shape[0] // nsub
    rows = [slice(k * rs, (k + 1) * rs) for k in range(nsub)]
    heads = [slice(hd * XA_HEAD_DIM, (hd + 1) * XA_HEAD_DIM) for hd in range(XA_HEADS)]
    scale = XA_HEAD_DIM ** -0.5
    hs = [h_ref[r, :] for r in rows]
    qs = [_dot(h.astype(BF16), wq_ref[...]).astype(BF16) for h in hs]
    ss = [[_dot_nt(q[:, ls], k_ref[0, :, ls]) * scale for ls in heads] for q in qs]
    os_ = []
    for k in range(nsub):
        outs = []
        for hd, ls in enumerate(heads):
            s = ss[k][hd]
            p = jnp.exp(s - jnp.max(s, axis=-1, keepdims=True))
            p = p / jnp.sum(p, axis=-1, keepdims=True)
            outs.append(_dot(p.astype(BF16), v_ref[0, :, ls]).astype(BF16))
        os_.append(jnp.concatenate(outs, axis=-1))
    ys = [_dot(o, wo_ref[...]) for o in os_]
    for k in range(nsub):
        o_ref[rows[k], :] = _layer_norm(DN_ALPHA * hs[k] + ys[k], g_ref[...], b_ref[...])


def _xattn(h1, kx, vx, wq, wo, g2, b2, B, S, tm=1024):
    T = B * S
    M = kx.shape[1]
    W = kx.shape[2]
    nsb = S // tm
    return pl.pallas_call(
        _xattn_kernel,
        out_shape=jax.ShapeDtypeStruct((T, D_MODEL), F32),
        grid=(B, nsb),
        in_specs=[
            pl.BlockSpec((tm, D_MODEL), lambda b, s: (b * nsb + s, 0)),
            pl.BlockSpec((1, M, W), lambda b, s: (b, 0, 0)),
            pl.BlockSpec((1, M, W), lambda b, s: (b, 0, 0)),
            _const_spec(wq.shape), _const_spec(wo.shape),
            _const_spec((1, D_MODEL)), _const_spec((1, D_MODEL)),
        ],
        out_specs=pl.BlockSpec((tm, D_MODEL), lambda b, s: (b * nsb + s, 0)),
        compiler_params=_cparams(("parallel", "arbitrary")),
        name="xattn",
    )(h1, kx, vx, wq, wo, g2, b2)


MOE_TM = 1024
MOE_M1 = 160
MOE_M2 = 64
MOE_EPS = 2
SEG_ALIGN = 16
MOE_QCH = 512


def _moe_slots(tm):
    return -(-(2 * tm + N_EXPERTS * SEG_ALIGN + MOE_M1 + MOE_M2) // MOE_QCH) * MOE_QCH


def _seg_pad(cnt):
    return jnp.ceil(cnt * (1.0 / SEG_ALIGN)) * SEG_ALIGN
SUBLANES = 8
GROUP_ROWS = 16


def _hdot(a, b):
    return jnp.dot(a, b, preferred_element_type=F32, precision=lax.Precision.HIGHEST)


def _split2(a):
    hi = a.astype(BF16)
    return hi, (a - hi.astype(F32)).astype(BF16)


def _first_argmax0(vals, row, height):
    m = jnp.max(vals, axis=0, keepdims=True)
    idx = jnp.min(jnp.where(vals == m, row, height), axis=0, keepdims=True)
    return m, idx


def _select_rows(shape, rows):
    r = lax.broadcasted_iota(jnp.int32, shape, 0)
    out = jnp.zeros(shape, F32)
    for k, v in enumerate(rows):
        out = jnp.where(r == k, v, out)
    return out


def _route_kernel(h_ref, wt_ref, bg_ref, be_ref, rinfo_ref, cinfo_ref, meta_ref):
    tm = h_ref.shape[0]
    h = h_ref[...]
    neg = jnp.float32(-jnp.inf)
    rg = lax.broadcasted_iota(jnp.int32, (GROUP_ROWS, tm), 0)
    re = lax.broadcasted_iota(jnp.int32, (N_EXPERTS, tm), 0)
    nw = GROUP_ROWS + N_EXPERTS
    hh, hl = _split2(h)
    wh, wl = _split2(wt_ref[...])
    lg = _dot_nt(jnp.concatenate([wh, wl], axis=0), hh)
    lg = lg[:nw] + (lg[nw:] + _dot_nt(wh, hl))
    gl = jnp.where(rg < N_GROUPS, lg[:GROUP_ROWS] + bg_ref[...], neg)
    el = lg[GROUP_ROWS:] + be_ref[...]
    gm, g_idx = _first_argmax0(gl, rg, GROUP_ROWS)
    g_top = 1.0 / jnp.sum(jnp.exp(gl - gm), axis=0, keepdims=True)
    ml = jnp.where(re // EXPERTS_PER_GROUP == g_idx, el, neg)
    m1, i1 = _first_argmax0(ml, re, N_EXPERTS)
    ml2 = jnp.where(re == i1, neg, ml)
    m2, i2 = _first_argmax0(ml2, re, N_EXPERTS)
    e2 = jnp.exp(m2 - m1)
    gate0 = g_top / (1.0 + e2)
    gate1 = g_top * e2 / (1.0 + e2)

    oh0 = jnp.where(re == i1, 1.0, 0.0)
    oh1 = jnp.where(re == i2, 1.0, 0.0)
    cnt0 = jnp.sum(oh0, axis=1, keepdims=True)
    cnt1 = jnp.sum(oh1, axis=1, keepdims=True)
    er = lax.broadcasted_iota(jnp.int32, (N_EXPERTS, N_EXPERTS), 0)
    ec = lax.broadcasted_iota(jnp.int32, (N_EXPERTS, N_EXPERTS), 1)
    seg = _hdot(jnp.where(ec < er, 1.0, 0.0),
                jnp.broadcast_to(_seg_pad(cnt0 + cnt1), (N_EXPERTS, LANES)))[:, 0:1]
    tr = lax.broadcasted_iota(jnp.int32, (tm, tm), 0)
    tc = lax.broadcasted_iota(jnp.int32, (tm, tm), 1)
    earlier = jnp.where(tr < tc, 1.0, 0.0).astype(BF16)
    cum = _dot(jnp.concatenate([oh0, oh1], axis=0).astype(BF16), earlier)
    slot0 = jnp.sum(oh0 * (seg + cum[:N_EXPERTS]), axis=0, keepdims=True)
    slot1 = jnp.sum(oh1 * (seg + cnt0 + cum[N_EXPERTS:]), axis=0, keepdims=True)

    info = (slot0, slot1, gate0, gate1)
    rinfo_ref[0] = _select_rows((SUBLANES, tm), info)
    cinfo_ref[...] = _select_rows((LANES, tm), info).T

    ohs = jnp.concatenate([oh0 + oh1, jnp.zeros((LANES - N_EXPERTS, tm), F32)], axis=0).astype(BF16)
    cnt_row = _dot_nt(jnp.ones((SUBLANES, tm), BF16), ohs)
    lr = lax.broadcasted_iota(jnp.int32, (LANES, LANES), 0)
    lc = lax.broadcasted_iota(jnp.int32, (LANES, LANES), 1)
    seg_row = _hdot(_seg_pad(cnt_row), jnp.where(lr < lc, 1.0, 0.0))
    r8 = lax.broadcasted_iota(jnp.int32, (SUBLANES, LANES), 0)
    meta_ref[0] = jnp.where(r8 == 0, seg_row, jnp.where(r8 == 1, cnt_row, 0.0)).astype(jnp.int32)


def _route(h2, wt, bg, be, tm=MOE_TM):
    T = h2.shape[0]
    nt = T // tm
    return pl.pallas_call(
        _route_kernel,
        out_shape=(jax.ShapeDtypeStruct((nt, SUBLANES, tm), F32),
                   jax.ShapeDtypeStruct((T, LANES), F32),
                   jax.ShapeDtypeStruct((nt, SUBLANES, LANES), jnp.int32)),
        grid=(nt,),
        in_specs=[pl.BlockSpec((tm, D_MODEL), lambda i: (i, 0)),
                  _const_spec(wt.shape), _const_spec(bg.shape), _const_spec(be.shape)],
        out_specs=(pl.BlockSpec((1, SUBLANES, tm), lambda i: (i, 0, 0)),
                   pl.BlockSpec((tm, LANES), lambda i: (i, 0)),
                   pl.BlockSpec((1, SUBLANES, LANES), lambda i: (i, 0, 0))),
        compiler_params=_cparams(("parallel",)),
        name="moe_route",
    )(h2, wt, bg, be)


def _moe_kernel(meta_ref, h_ref, rinfo_ref, cinfo_ref, w1_ref, w3_ref, w2_ref, g_ref, b_ref,
                o_ref, hb_ref, ys_ref):
    t = pl.program_id(0)
    eg = pl.program_id(1)
    tm = h_ref.shape[0]

    @pl.when(eg == 0)
    def _():
        hb_ref[...] = h_ref[...].astype(BF16)
        ys_ref[...] = jnp.zeros_like(ys_ref)

    def chunk(j, r0, m):
        r0 = pl.multiple_of(r0, SEG_ALIGN)
        info = rinfo_ref[0]
        s = lax.broadcasted_iota(jnp.int32, (m, tm), 0) + r0
        pick = (s == info[0:1].astype(jnp.int32)) | (s == info[1:2].astype(jnp.int32))
        x = _dot(jnp.where(pick, 1.0, 0.0).astype(BF16), hb_ref[...]).astype(BF16)
        a = _dot(x, w1_ref[j])
        b = _dot(x, w3_ref[j])
        hid = (a * jax.nn.sigmoid(a) * b).astype(BF16)
        ys_ref[pl.ds(r0, m), :] = _dot(hid, w2_ref[j]).astype(BF16)

    for j in range(MOE_EPS):
        e = eg * MOE_EPS + j
        start = meta_ref[(2 * t) * N_EXPERTS + e]
        cnt = meta_ref[(2 * t + 1) * N_EXPERTS + e]

        @pl.when(cnt > 0)
        def _(j=j, start=start, cnt=cnt):
            chunk(j, start, MOE_M1)

            def extra(k, c):
                chunk(j, start + MOE_M1 + k * MOE_M2, MOE_M2)
                return c

            lax.fori_loop(0, (jnp.maximum(cnt - MOE_M1, 0) + MOE_M2 - 1) // MOE_M2, extra, 0)

    @pl.when(eg == N_EXPERTS // MOE_EPS - 1)
    def _():
        ci = cinfo_ref[...]
        slot0 = ci[:, 0:1].astype(jnp.int32)
        slot1 = ci[:, 1:2].astype(jnp.int32)
        y = None
        for c in range(ys_ref.shape[0] // MOE_QCH):
            sc = lax.broadcasted_iota(jnp.int32, (tm, MOE_QCH), 1) + c * MOE_QCH
            q = (jnp.where(sc == slot0, ci[:, 2:3], 0.0) + jnp.where(sc == slot1, ci[:, 3:4], 0.0)).astype(BF16)
            part = _dot(q, ys_ref[c * MOE_QCH:(c + 1) * MOE_QCH, :])
            y = part if y is None else y + part
        o_ref[...] = _layer_norm(DN_ALPHA * h_ref[...] + y, g_ref[...], b_ref[...])


def _moe(h2, rinfo, cinfo, meta, w1, w3, w2, g3, b3, tm=MOE_TM):
    T = h2.shape[0]
    return pl.pallas_call(
        _moe_kernel,
        out_shape=jax.ShapeDtypeStruct((T, D_MODEL), F32),
        grid_spec=pltpu.PrefetchScalarGridSpec(
            num_scalar_prefetch=1,
            grid=(T // tm, N_EXPERTS // MOE_EPS),
            in_specs=[
                pl.BlockSpec((tm, D_MODEL), lambda i, e, m: (i, 0)),
                pl.BlockSpec((1, SUBLANES, tm), lambda i, e, m: (i, 0, 0)),
                pl.BlockSpec((tm, LANES), lambda i, e, m: (i, 0)),
                pl.BlockSpec((MOE_EPS, D_MODEL, EXPERT_FF), lambda i, e, m: (e, 0, 0)),
                pl.BlockSpec((MOE_EPS, D_MODEL, EXPERT_FF), lambda i, e, m: (e, 0, 0)),
                pl.BlockSpec((MOE_EPS, EXPERT_FF, D_MODEL), lambda i, e, m: (e, 0, 0)),
                pl.BlockSpec((1, D_MODEL), lambda i, e, m: (0, 0)),
                pl.BlockSpec((1, D_MODEL), lambda i, e, m: (0, 0)),
            ],
            out_specs=pl.BlockSpec((tm, D_MODEL), lambda i, e, m: (i, 0)),
            scratch_shapes=[pltpu.VMEM((tm, D_MODEL), BF16), pltpu.VMEM((_moe_slots(tm), D_MODEL), BF16)],
        ),
        compiler_params=_cparams(("parallel", "arbitrary")),
        name="moe",
    )(meta, h2, rinfo, cinfo, w1, w3, w2, g3, b3)


def _moe_layer(h2, wg, bg, we, be, w1, w3, w2, g3, b3, tm=MOE_TM):
    pad = GROUP_ROWS - N_GROUPS
    wt = jnp.concatenate([jnp.pad(wg.astype(F32).T, ((0, pad), (0, 0))), we.astype(F32).T], axis=0)
    bgc = jnp.pad(bg.astype(F32).reshape(-1, 1), ((0, pad), (0, 0)))
    rinfo, cinfo, meta = _route(h2, wt, bgc, be.astype(F32).reshape(-1, 1), tm=tm)
    meta = meta[:, :2, :N_EXPERTS].reshape(-1)
    return _moe(h2, rinfo, cinfo, meta, w1, w3, w2, g3, b3, tm=tm)


def kernel(x, mem, ln_in_g, ln_in_b, w_in, hg_lb_logits, hg_norm_g, w_branch_a, w_branch_b, w_mix_out,
           ln1_g, ln1_b, xa_wq, xa_wk, xa_wv, xa_wo, ln2_g, ln2_b, router_wg, router_bg, router_we,
           router_be, moe_w1, moe_w3, moe_w2, ln3_g, ln3_b):
    B, S, D = x.shape
    T = B * S
    row = lambda a: a.reshape(1, -1).astype(F32)
    x2 = x.reshape(T, D)
    gin, bin_ = row(ln_in_g), row(ln_in_b)
    assert w_in.shape[0] == DEPTH
    l = 0
    proj = _in_proj(x2, gin, bin_, w_in[l].astype(BF16))
    oa = _hgrn2(proj, hg_lb_logits.astype(F32), row(hg_norm_g[l]), B, S, l)
    ob = _sb_attn(proj, B, S)
    h1 = _mix(x2, gin, bin_, oa, ob, proj, w_branch_a[l].astype(BF16), w_branch_b[l].astype(BF16),
              w_mix_out[l].astype(BF16), row(ln1_g[l]), row(ln1_b[l]))
    kx, vx = _xa_kv(mem, xa_wk[l].astype(BF16), xa_wv[l].astype(BF16))
    h2 = _xattn(h1, kx, vx, xa_wq[l].astype(BF16), xa_wo[l].astype(BF16), row(ln2_g[l]), row(ln2_b[l]), B, S)
    out = _moe_layer(h2, router_wg[l], router_bg[l], router_we[l], router_be[l],
                     moe_w1[l].astype(BF16), moe_w3[l].astype(BF16), moe_w2[l].astype(BF16),
                     row(ln3_g[l]), row(ln3_b[l]))
    return out.reshape(B, S, D)
```

```python
import functools

import jax
import jax.numpy as jnp
from jax import lax
from jax.experimental import pallas as pl
from jax.experimental.pallas import tpu as pltpu

F32 = jnp.float32
BF16 = jnp.bfloat16

D_MODEL = 1024
HG_HEADS = 4
HG_DIM = 128
HG_W = HG_HEADS * HG_DIM
HG_CHUNK = 64
HG_LOCAL_CHUNKS = 4
SB_HEADS = 8
SB_HEAD_DIM = 64
SB_W = SB_HEADS * SB_HEAD_DIM
SB_BLOCK = 128
XA_HEADS = 4
XA_HEAD_DIM = D_MODEL // XA_HEADS
N_GROUPS = 4
EXPERTS_PER_GROUP = 4
N_EXPERTS = N_GROUPS * EXPERTS_PER_GROUP
EXPERT_FF = 512
N_IN = HG_W * 4 + SB_W * 3 + D_MODEL * 2
COL_HG = 0
COL_SB = 4 * HG_W
COL_GATE = COL_SB + 3 * SB_W
LN_EPS = 1e-5
RMS_EPS = 1e-6
DEPTH = 1
DN_ALPHA = (2 * DEPTH) ** 0.25

LANES = 128
VMEM_LIMIT = 56 * 1024 * 1024


def _cparams(sem, flags=None):
    return pltpu.CompilerParams(dimension_semantics=sem, vmem_limit_bytes=VMEM_LIMIT, flags=flags)


def _layer_norm(x, g, b):
    mu = jnp.mean(x, axis=-1, keepdims=True)
    xc = x - mu
    var = jnp.mean(xc * xc, axis=-1, keepdims=True)
    return xc * lax.rsqrt(var + LN_EPS) * g + b


def _dot(a, b):
    return jnp.dot(a, b, preferred_element_type=F32)


def _dot_nt(a, b):
    return lax.dot_general(a, b, (((1,), (1,)), ((), ())), preferred_element_type=F32)


def _dot_tn(a, b):
    return lax.dot_general(a, b, (((0,), (0,)), ((), ())), preferred_element_type=F32)


def _split3(a):
    hi = a.astype(BF16)
    r1 = a - hi.astype(F32)
    mid = r1.astype(BF16)
    lo = (r1 - mid.astype(F32)).astype(BF16)
    return hi, mid, lo


def _const_spec(shape):
    nd = len(shape)
    return pl.BlockSpec(shape, lambda *_: (0,) * nd)


def _in_proj_kernel(x_ref, g_ref, b_ref, w_ref, o_ref, *, col_chunk):
    h = _layer_norm(x_ref[...], g_ref[...], b_ref[...]).astype(BF16)
    for c in range(N_IN // col_chunk):
        sl = slice(c * col_chunk, (c + 1) * col_chunk)
        o_ref[:, sl] = _dot(h, w_ref[:, sl]).astype(BF16)


def _in_proj(x2, g, b, w_bf, tm=512, col_chunk=512):
    T = x2.shape[0]
    return pl.pallas_call(
        functools.partial(_in_proj_kernel, col_chunk=col_chunk),
        out_shape=jax.ShapeDtypeStruct((T, N_IN), BF16),
        grid=(T // tm,),
        in_specs=[
            pl.BlockSpec((tm, D_MODEL), lambda i: (i, 0)),
            _const_spec((1, D_MODEL)),
            _const_spec((1, D_MODEL)),
            _const_spec((D_MODEL, N_IN)),
        ],
        out_specs=pl.BlockSpec((tm, N_IN), lambda i: (i, 0)),
        compiler_params=_cparams(("parallel",)),
        name="in_proj",
    )(x2, g, b, w_bf)


def _hgrn2_kernel(lbl_ref, ng_ref, q_ref, f_ref, i_ref, g_ref, o_ref,
                  state_ref, intra_ref, qs_ref, dl_ref, u_ref, *, layer, tb):
    C = HG_CHUNK
    H2 = C // 2

    @pl.when(pl.program_id(1) == 0)
    def _():
        state_ref[...] = jnp.zeros_like(state_ref)

    lbl = lbl_ref[...]
    ex = jnp.exp(lbl - jnp.max(lbl, axis=0, keepdims=True))
    lb = jnp.sum(ex[: layer + 1], axis=0, keepdims=True) / jnp.sum(ex, axis=0, keepdims=True)

    R = HG_LOCAL_CHUNKS * C
    row = lax.broadcasted_iota(jnp.int32, (R, R), 0)
    col = lax.broadcasted_iota(jnp.int32, (R, R), 1)
    same_chunk = (row // C) == (col // C)
    tri = (same_chunk & (col <= row)).astype(BF16)
    m_diag = ((row // H2) == (col // H2)) & (col <= row)
    m_cross = same_chunk & ((row % C) >= H2) & ((col % C) < H2)
    rrow = lax.broadcasted_iota(jnp.int32, (R, HG_W), 0)
    ng = ng_ref[...]
    heads = [slice(h * HG_DIM, (h + 1) * HG_DIM) for h in range(HG_HEADS)]

    def row_of_period(x, offset, period):
        out = jnp.broadcast_to(x[offset:offset + 1, :], x.shape)
        for k in range(1, R // period):
            out = jnp.where(rrow >= k * period, x[k * period + offset:k * period + offset + 1, :], out)
        return out

    def local(blk):
        rows = slice(blk * R, (blk + 1) * R)
        forget = lb + (1.0 - lb) * jax.nn.sigmoid(f_ref[rows, :].astype(F32))
        kk = 1.0 - forget
        hi, mid, lo = _split3(jnp.log(forget))
        b = _dot(tri, hi) + _dot(tri, mid) + _dot(tri, lo)
        q = q_ref[rows, :].astype(F32)
        d1 = b - row_of_period(b, H2 // 2 - 1, H2)
        q1 = (q * jnp.exp(d1)).astype(BF16)
        k1 = (kk * jnp.exp(-d1)).astype(BF16)
        e3 = jnp.exp(-jnp.abs(b - row_of_period(b, H2 - 1, C)))
        q3 = (q * e3).astype(BF16)
        k3 = (kk * e3).astype(BF16)
        kd = (kk * jnp.exp(row_of_period(b, C - 1, C) - b)).astype(BF16)
        qs_ref[rows, :] = (q * jnp.exp(b)).astype(BF16)
        for k in range(HG_LOCAL_CHUNKS):
            dl_ref[blk * HG_LOCAL_CHUNKS + k] = jnp.exp(b[k * C + C - 1:(k + 1) * C, :])
        s1 = [_dot_nt(q1[:, ls], k1[:, ls]) for ls in heads]
        s3 = [_dot_nt(q3[:, ls], k3[:, ls]) for ls in heads]
        s = [jnp.where(m_diag, a1, jnp.where(m_cross, a3, 0.0)).astype(BF16) for a1, a3 in zip(s1, s3)]
        for h, ls in enumerate(heads):
            v = i_ref[rows, ls]
            intra_ref[rows, ls] = _dot(s[h], v)
            for k in range(HG_LOCAL_CHUNKS):
                cs = slice(k * C, (k + 1) * C)
                u_ref[blk * HG_LOCAL_CHUNKS + k, h] = _dot_tn(v[cs], kd[cs, ls])

    def carry_state(c, carry):
        rows = pl.ds(pl.multiple_of(c * C, C), C)
        dl = dl_ref[c]
        for h, ls in enumerate(heads):
            st = state_ref[h]
            o = intra_ref[rows, ls] + _dot_nt(qs_ref[rows, ls], st.astype(BF16))
            state_ref[h] = st * dl[:, ls] + u_ref[c, h]
            rms = lax.rsqrt(jnp.mean(o * o, axis=-1, keepdims=True) + RMS_EPS)
            gate = g_ref[rows, ls].astype(F32)
            o_ref[rows, ls] = (o * rms * ng * (gate * jax.nn.sigmoid(gate))).astype(o_ref.dtype)
        return carry

    for blk in range(tb // R):
        local(blk)
    lax.fori_loop(0, tb // C, carry_state, 0, unroll=True)


def _hgrn2(proj, lb_logits, norm_g, B, S, layer, tb=512):
    T = B * S
    nsb = S // tb

    def spec(k):
        return pl.BlockSpec((tb, HG_W), lambda b, s, k=k: (b * nsb + s, COL_HG // HG_W + k))

    return pl.pallas_call(
        functools.partial(_hgrn2_kernel, layer=layer, tb=tb),
        out_shape=jax.ShapeDtypeStruct((T, HG_W), BF16),
        grid=(B, nsb),
        in_specs=[
            _const_spec(lb_logits.shape),
            _const_spec((1, HG_DIM)),
            spec(0), spec(1), spec(2), spec(3),
        ],
        out_specs=pl.BlockSpec((tb, HG_W), lambda b, s: (b * nsb + s, 0)),
        scratch_shapes=[pltpu.VMEM((HG_HEADS, HG_DIM, HG_DIM), F32),
                        pltpu.VMEM((tb, HG_W), F32),
                        pltpu.VMEM((tb, HG_W), BF16),
                        pltpu.VMEM((tb // HG_CHUNK, 1, HG_W), F32),
                        pltpu.VMEM((tb // HG_CHUNK, HG_HEADS, HG_DIM, HG_DIM), F32)],
        compiler_params=_cparams(("parallel", "arbitrary")),
        name="hgrn2",
    )(lb_logits, norm_g, proj, proj, proj, proj)


LOG2E = 1.4426950408889634
SB_DEAD = -160.0


def _sb_kernel(q_ref, k_ref, v_ref, o_ref, acc_ref, run_ref, *, npairs, tq):
    i = pl.program_id(1)
    BLK = SB_BLOCK
    W2 = 2 * BLK
    ndiag = tq // BLK
    scale = SB_HEAD_DIM ** -0.5
    lane = lax.broadcasted_iota(jnp.int32, (BLK, LANES), 1)
    head0 = lane < SB_HEAD_DIM
    zrow = lax.broadcasted_iota(jnp.int32, (tq, W2), 0)
    zcol = lax.broadcasted_iota(jnp.int32, (tq, W2), 1) & (BLK - 1)
    wr = lax.broadcasted_iota(jnp.int32, (W2, W2), 0) & (BLK - 1)
    wc = lax.broadcasted_iota(jnp.int32, (W2, W2), 1)
    cw = ((wc >= BLK) | (wr > wc)).astype(BF16)
    qs = q_ref[...] * jnp.asarray(scale, q_ref.dtype)

    def masked_kv(ref, j, p):
        x = ref[pl.ds(pl.multiple_of(j * BLK, BLK), BLK), p * LANES:(p + 1) * LANES]
        zero = jnp.zeros_like(x)
        return jnp.concatenate([jnp.where(head0, x, zero), jnp.where(head0, zero, x)], axis=0)

    def step(js, ds):
        nb = len(js)
        rsl = [slice((ds[b] or 0) * BLK, tq) for b in range(nb)]
        zs = [[_dot_nt(qs[rsl[b], p * LANES:(p + 1) * LANES], masked_kv(k_ref, js[b], p)) * LOG2E
               for p in range(npairs)] for b in range(nb)]
        ws = []
        for b in range(nb):
            nr = tq - rsl[b].start
            before = None if ds[b] is None else ((zcol + ds[b] * BLK) < zrow)[rsl[b]]
            lbs, hls = [], []
            for p in range(npairs):
                z = zs[b][p]
                l2 = jnp.log2(1.0 + jnp.exp2(-jnp.abs(z)))
                lb = jnp.minimum(z, 0.0) - l2
                lk = lb - z
                if before is not None:
                    lk = jnp.where(before, lk, 0.0)
                hi = lk.astype(BF16)
                lo = (lk - hi.astype(F32)).astype(BF16)
                lbs.append(lb)
                hls.append(jnp.concatenate([hi[:, :BLK], lo[:, :BLK]], axis=1))
                hls.append(jnp.concatenate([hi[:, BLK:], lo[:, BLK:]], axis=1))
            cs = _dot(jnp.concatenate(hls, axis=0), cw)
            wb = []
            for p in range(npairs):
                c0 = cs[(2 * p) * nr:(2 * p + 1) * nr]
                c1 = cs[(2 * p + 1) * nr:(2 * p + 2) * nr]
                cum = jnp.concatenate([c0[:, :BLK], c1[:, :BLK]], axis=1)
                rsum = jnp.concatenate([c0[:, BLK:], c1[:, BLK:]], axis=1)
                run = run_ref[p, rsl[b], :]
                w = jnp.exp2(lbs[p] + (run + cum))
                if before is not None:
                    w = jnp.where(before, w, 0.0)
                run_ref[p, rsl[b], :] = run + rsum
                wb.append(w.astype(BF16))
            ws.append(wb)
        for p in range(npairs):
            for b in range(nb):
                acc_ref[p, rsl[b], :] += _dot(ws[b][p], masked_kv(v_ref, js[b], p))

    acc_ref[...] = jnp.zeros_like(acc_ref)
    run_ref[...] = jnp.zeros_like(run_ref)
    step([i * ndiag + d for d in reversed(range(ndiag))], list(reversed(range(ndiag))))

    def live():
        return jnp.max(run_ref[...]) > SB_DEAD

    def cond(c):
        return jnp.logical_and(c[0] < i, c[1])

    def body(c):
        base = (i - 1 - c[0]) * ndiag
        step([base + d for d in reversed(range(ndiag))], [None] * ndiag)
        return c[0] + 1, live()

    lax.while_loop(cond, body, (jnp.int32(0), live()))
    for p in range(npairs):
        o_ref[:, p * LANES:(p + 1) * LANES] = acc_ref[p].astype(o_ref.dtype)


def _sb_attn(proj, B, S, npairs=SB_W // LANES, tq=256):
    T = B * S
    nq = S // tq
    wblk = npairs * LANES
    ng = SB_W // wblk
    cq = COL_SB // wblk
    ck = (COL_SB + SB_W) // wblk
    cv = (COL_SB + 2 * SB_W) // wblk
    return pl.pallas_call(
        functools.partial(_sb_kernel, npairs=npairs, tq=tq),
        out_shape=jax.ShapeDtypeStruct((T, SB_W), BF16),
        grid=(B * ng, nq),
        in_specs=[
            pl.BlockSpec((tq, wblk), lambda g, i: ((g // ng) * nq + i, cq + g % ng)),
            pl.BlockSpec((S, wblk), lambda g, i: (g // ng, ck + g % ng)),
            pl.BlockSpec((S, wblk), lambda g, i: (g // ng, cv + g % ng)),
        ],
        out_specs=pl.BlockSpec((tq, wblk), lambda g, i: ((g // ng) * nq + i, g % ng)),
        scratch_shapes=[pltpu.VMEM((npairs, tq, LANES), F32),
                        pltpu.VMEM((npairs, tq, 2 * SB_BLOCK), F32)],
        compiler_params=_cparams(("parallel", "arbitrary")),
        name="sb_attn",
    )(proj, proj, proj)


GATE_BLK = 512


def _mix_kernel(x_ref, gin_ref, bin_ref, oa_ref, ob_ref, ga0_ref, ga1_ref, gb0_ref, gb1_ref,
                wa_ref, wb_ref, wm_ref, g1_ref, b1_ref, o_ref):
    rs = x_ref.shape[0] // MIX_SUBTILES
    rows = [slice(k * rs, (k + 1) * rs) for k in range(MIX_SUBTILES)]
    halves = [(slice(0, GATE_BLK), ga0_ref, gb0_ref), (slice(GATE_BLK, 2 * GATE_BLK), ga1_ref, gb1_ref)]
    yab = [[(_dot(oa_ref[r, :], wa_ref[:, cs]), _dot(ob_ref[r, :], wb_ref[:, cs])) for cs, _, _ in halves]
           for r in rows]
    ys = []
    for k, r in enumerate(rows):
        y = None
        for (cs, ga_ref, gb_ref), (ya, yb) in zip(halves, yab[k]):
            merged = (jax.nn.sigmoid(ga_ref[r, :].astype(F32)) * ya
                      + jax.nn.sigmoid(gb_ref[r, :].astype(F32)) * yb)
            part = _dot(merged.astype(BF16), wm_ref[cs, :])
            y = part if y is None else y + part
        ys.append(y)
    for k, r in enumerate(rows):
        h0 = _layer_norm(x_ref[r, :], gin_ref[...], bin_ref[...])
        o_ref[r, :] = _layer_norm(DN_ALPHA * h0 + ys[k], g1_ref[...], b1_ref[...])


MIX_SUBTILES = 4


def _mix(x2, gin, bin_, oa, ob, proj, wa, wb, wm, g1, b1, tm=1024):
    T = x2.shape[0]
    c0 = COL_GATE // GATE_BLK

    def gate_spec(k):
        return pl.BlockSpec((tm, GATE_BLK), lambda i, k=k: (i, c0 + k))

    return pl.pallas_call(
        _mix_kernel,
        out_shape=jax.ShapeDtypeStruct((T, D_MODEL), F32),
        grid=(T // tm,),
        in_specs=[
            pl.BlockSpec((tm, D_MODEL), lambda i: (i, 0)),
            _const_spec((1, D_MODEL)), _const_spec((1, D_MODEL)),
            pl.BlockSpec((tm, HG_W), lambda i: (i, 0)),
            pl.BlockSpec((tm, SB_W), lambda i: (i, 0)),
            gate_spec(0), gate_spec(1), gate_spec(2), gate_spec(3),
            _const_spec((HG_W, D_MODEL)), _const_spec((SB_W, D_MODEL)), _const_spec((D_MODEL, D_MODEL)),
            _const_spec((1, D_MODEL)), _const_spec((1, D_MODEL)),
        ],
        out_specs=pl.BlockSpec((tm, D_MODEL), lambda i: (i, 0)),
        compiler_params=_cparams(("parallel",)),
        name="mix",
    )(x2, gin, bin_, oa, ob, proj, proj, proj, proj, wa, wb, wm, g1, b1)


def _xa_kv_kernel(m_ref, wk_ref, wv_ref, k_ref, v_ref):
    m = m_ref[0].astype(BF16)
    k_ref[0] = _dot(m, wk_ref[...]).astype(BF16)
    v_ref[0] = _dot(m, wv_ref[...]).astype(BF16)


def _xa_kv(mem, wk, wv):
    B, M, _ = mem.shape
    W = wk.shape[1]
    return pl.pallas_call(
        _xa_kv_kernel,
        out_shape=(jax.ShapeDtypeStruct((B, M, W), BF16), jax.ShapeDtypeStruct((B, M, W), BF16)),
        grid=(B,),
        in_specs=[pl.BlockSpec((1, M, D_MODEL), lambda b: (b, 0, 0)),
                  _const_spec(wk.shape), _const_spec(wv.shape)],
        out_specs=(pl.BlockSpec((1, M, W), lambda b: (b, 0, 0)),
                   pl.BlockSpec((1, M, W), lambda b: (b, 0, 0))),
        compiler_params=_cparams(("parallel",)),
        name="xa_kv",
    )(mem, wk, wv)


XA_SUBTILES = 4


def _xattn_kernel(h_ref, k_ref, v_ref, wq_ref, wo_ref, g_ref, b_ref, o_ref):
    nsub = XA_SUBTILES
    rs = h_ref.shape[0] // nsub
    rows = [slice(k * rs, (k + 1) * rs) for k in range(nsub)]
    heads = [slice(hd * XA_HEAD_DIM, (hd + 1) * XA_HEAD_DIM) for hd in range(XA_HEADS)]
    scale = XA_HEAD_DIM ** -0.5
    hs = [h_ref[r, :] for r in rows]
    qs = [_dot(h.astype(BF16), wq_ref[...]).astype(BF16) for h in hs]
    ss = [[_dot_nt(q[:, ls], k_ref[0, :, ls]) * scale for ls in heads] for q in qs]
    os_ = []
    for k in range(nsub):
        outs = []
        for hd, ls in enumerate(heads):
            s = ss[k][hd]
            p = jnp.exp(s - jnp.max(s, axis=-1, keepdims=True))
            p = p / jnp.sum(p, axis=-1, keepdims=True)
            outs.append(_dot(p.astype(BF16), v_ref[0, :, ls]).astype(BF16))
        os_.append(jnp.concatenate(outs, axis=-1))
    ys = [_dot(o, wo_ref[...]) for o in os_]
    for k in range(nsub):
        o_ref[rows[k], :] = _layer_norm(DN_ALPHA * hs[k] + ys[k], g_ref[...], b_ref[...])


def _xattn(h1, kx, vx, wq, wo, g2, b2, B, S, tm=1024):
    T = B * S
    M = kx.shape[1]
    W = kx.shape[2]
    nsb = S // tm
    return pl.pallas_call(
        _xattn_kernel,
        out_shape=jax.ShapeDtypeStruct((T, D_MODEL), F32),
        grid=(B, nsb),
        in_specs=[
            pl.BlockSpec((tm, D_MODEL), lambda b, s: (b * nsb + s, 0)),
            pl.BlockSpec((1, M, W), lambda b, s: (b, 0, 0)),
            pl.BlockSpec((1, M, W), lambda b, s: (b, 0, 0)),
            _const_spec(wq.shape), _const_spec(wo.shape),
            _const_spec((1, D_MODEL)), _const_spec((1, D_MODEL)),
        ],
        out_specs=pl.BlockSpec((tm, D_MODEL), lambda b, s: (b * nsb + s, 0)),
        compiler_params=_cparams(("parallel", "arbitrary")),
        name="xattn",
    )(h1, kx, vx, wq, wo, g2, b2)


MOE_TM = 1024
MOE_M1 = 160
MOE_M2 = 64
MOE_EPS = 2
SEG_ALIGN = 16
MOE_QCH = 512


def _moe_slots(tm):
    return -(-(2 * tm + N_EXPERTS * SEG_ALIGN + MOE_M1 + MOE_M2) // MOE_QCH) * MOE_QCH


def _seg_pad(cnt):
    return jnp.ceil(cnt * (1.0 / SEG_ALIGN)) * SEG_ALIGN
SUBLANES = 8
GROUP_ROWS = 16


def _hdot(a, b):
    return jnp.dot(a, b, preferred_element_type=F32, precision=lax.Precision.HIGHEST)


def _split2(a):
    hi = a.astype(BF16)
    return hi, (a - hi.astype(F32)).astype(BF16)


def _first_argmax0(vals, row, height):
    m = jnp.max(vals, axis=0, keepdims=True)
    idx = jnp.min(jnp.where(vals == m, row, height), axis=0, keepdims=True)
    return m, idx


def _select_rows(shape, rows):
    r = lax.broadcasted_iota(jnp.int32, shape, 0)
    out = jnp.zeros(shape, F32)
    for k, v in enumerate(rows):
        out = jnp.where(r == k, v, out)
    return out


def _route_kernel(h_ref, wt_ref, bg_ref, be_ref, rinfo_ref, cinfo_ref, meta_ref):
    tm = h_ref.shape[0]
    h = h_ref[...]
    neg = jnp.float32(-jnp.inf)
    rg = lax.broadcasted_iota(jnp.int32, (GROUP_ROWS, tm), 0)
    re = lax.broadcasted_iota(jnp.int32, (N_EXPERTS, tm), 0)
    nw = GROUP_ROWS + N_EXPERTS
    hh, hl = _split2(h)
    wh, wl = _split2(wt_ref[...])
    lg = _dot_nt(jnp.concatenate([wh, wl], axis=0), hh)
    lg = lg[:nw] + (lg[nw:] + _dot_nt(wh, hl))
    gl = jnp.where(rg < N_GROUPS, lg[:GROUP_ROWS] + bg_ref[...], neg)
    el = lg[GROUP_ROWS:] + be_ref[...]
    gm, g_idx = _first_argmax0(gl, rg, GROUP_ROWS)
    g_top = 1.0 / jnp.sum(jnp.exp(gl - gm), axis=0, keepdims=True)
    ml = jnp.where(re // EXPERTS_PER_GROUP == g_idx, el, neg)
    m1, i1 = _first_argmax0(ml, re, N_EXPERTS)
    ml2 = jnp.where(re == i1, neg, ml)
    m2, i2 = _first_argmax0(ml2, re, N_EXPERTS)
    e2 = jnp.exp(m2 - m1)
    gate0 = g_top / (1.0 + e2)
    gate1 = g_top * e2 / (1.0 + e2)

    oh0 = jnp.where(re == i1, 1.0, 0.0)
    oh1 = jnp.where(re == i2, 1.0, 0.0)
    cnt0 = jnp.sum(oh0, axis=1, keepdims=True)
    cnt1 = jnp.sum(oh1, axis=1, keepdims=True)
    er = lax.broadcasted_iota(jnp.int32, (N_EXPERTS, N_EXPERTS), 0)
    ec = lax.broadcasted_iota(jnp.int32, (N_EXPERTS, N_EXPERTS), 1)
    seg = _hdot(jnp.where(ec < er, 1.0, 0.0),
                jnp.broadcast_to(_seg_pad(cnt0 + cnt1), (N_EXPERTS, LANES)))[:, 0:1]
    tr = lax.broadcasted_iota(jnp.int32, (tm, tm), 0)
    tc = lax.broadcasted_iota(jnp.int32, (tm, tm), 1)
    earlier = jnp.where(tr < tc, 1.0, 0.0).astype(BF16)
    cum = _dot(jnp.concatenate([oh0, oh1], axis=0).astype(BF16), earlier)
    slot0 = jnp.sum(oh0 * (seg + cum[:N_EXPERTS]), axis=0, keepdims=True)
    slot1 = jnp.sum(oh1 * (seg + cnt0 + cum[N_EXPERTS:]), axis=0, keepdims=True)

    info = (slot0, slot1, gate0, gate1)
    rinfo_ref[0] = _select_rows((SUBLANES, tm), info)
    cinfo_ref[...] = _select_rows((LANES, tm), info).T

    ohs = jnp.concatenate([oh0 + oh1, jnp.zeros((LANES - N_EXPERTS, tm), F32)], axis=0).astype(BF16)
    cnt_row = _dot_nt(jnp.ones((SUBLANES, tm), BF16), ohs)
    lr = lax.broadcasted_iota(jnp.int32, (LANES, LANES), 0)
    lc = lax.broadcasted_iota(jnp.int32, (LANES, LANES), 1)
    seg_row = _hdot(_seg_pad(cnt_row), jnp.where(lr < lc, 1.0, 0.0))
    r8 = lax.broadcasted_iota(jnp.int32, (SUBLANES, LANES), 0)
    meta_ref[0] = jnp.where(r8 == 0, seg_row, jnp.where(r8 == 1, cnt_row, 0.0)).astype(jnp.int32)


def _route(h2, wt, bg, be, tm=MOE_TM):
    T = h2.shape[0]
    nt = T // tm
    return pl.pallas_call(
        _route_kernel,
        out_shape=(jax.ShapeDtypeStruct((nt, SUBLANES, tm), F32),
                   jax.ShapeDtypeStruct((T, LANES), F32),
                   jax.ShapeDtypeStruct((nt, SUBLANES, LANES), jnp.int32)),
        grid=(nt,),
        in_specs=[pl.BlockSpec((tm, D_MODEL), lambda i: (i, 0)),
                  _const_spec(wt.shape), _const_spec(bg.shape), _const_spec(be.shape)],
        out_specs=(pl.BlockSpec((1, SUBLANES, tm), lambda i: (i, 0, 0)),
                   pl.BlockSpec((tm, LANES), lambda i: (i, 0)),
                   pl.BlockSpec((1, SUBLANES, LANES), lambda i: (i, 0, 0))),
        compiler_params=_cparams(("parallel",)),
        name="moe_route",
    )(h2, wt, bg, be)


def _moe_kernel(meta_ref, h_ref, rinfo_ref, cinfo_ref, w1_ref, w3_ref, w2_ref, g_ref, b_ref,
                o_ref, hb_ref, ys_ref):
    t = pl.program_id(0)
    eg = pl.program_id(1)
    tm = h_ref.shape[0]

    @pl.when(eg == 0)
    def _():
        hb_ref[...] = h_ref[...].astype(BF16)
        ys_ref[...] = jnp.zeros_like(ys_ref)

    def chunks(js, r0s, m):
        r0s = [pl.multiple_of(r0, SEG_ALIGN) for r0 in r0s]
        info = rinfo_ref[0]
        slot0 = info[0:1].astype(jnp.int32)
        slot1 = info[1:2].astype(jnp.int32)
        xs = []
        for r0 in r0s:
            s = lax.broadcasted_iota(jnp.int32, (m, tm), 0) + r0
            pick = jnp.where((s == slot0) | (s == slot1), 1.0, 0.0).astype(BF16)
            xs.append(_dot(pick, hb_ref[...]).astype(BF16))
        ab = [(_dot(x, w1_ref[j]), _dot(x, w3_ref[j])) for j, x in zip(js, xs)]
        hids = [(a * jax.nn.sigmoid(a) * b).astype(BF16) for a, b in ab]
        ys = [_dot(hid, w2_ref[j]).astype(BF16) for j, hid in zip(js, hids)]
        for r0, y in zip(r0s, ys):
            ys_ref[pl.ds(r0, m), :] = y

    base = eg * MOE_EPS
    starts = [meta_ref[(2 * t) * N_EXPERTS + base + j] for j in range(MOE_EPS)]
    cnts = [meta_ref[(2 * t + 1) * N_EXPERTS + base + j] for j in range(MOE_EPS)]

    for j in range(MOE_EPS):
        @pl.when(cnts[j] > MOE_M1)
        def _(j=j):
            def extra(k, c):
                chunks([j], [starts[j] + MOE_M1 + k * MOE_M2], MOE_M2)
                return c

            lax.fori_loop(0, (cnts[j] - MOE_M1 + MOE_M2 - 1) // MOE_M2, extra, 0)

    chunks(list(range(MOE_EPS)), starts, MOE_M1)

    @pl.when(eg == N_EXPERTS // MOE_EPS - 1)
    def _():
        ci = cinfo_ref[...]
        slot0 = ci[:, 0:1].astype(jnp.int32)
        slot1 = ci[:, 1:2].astype(jnp.int32)
        y = None
        for c in range(ys_ref.shape[0] // MOE_QCH):
            sc = lax.broadcasted_iota(jnp.int32, (tm, MOE_QCH), 1) + c * MOE_QCH
            q = (jnp.where(sc == slot0, ci[:, 2:3], 0.0) + jnp.where(sc == slot1, ci[:, 3:4], 0.0)).astype(BF16)
            part = _dot(q, ys_ref[c * MOE_QCH:(c + 1) * MOE_QCH, :])
            y = part if y is None else y + part
        o_ref[...] = _layer_norm(DN_ALPHA * h_ref[...] + y, g_ref[...], b_ref[...])


def _moe(h2, rinfo, cinfo, meta, w1, w3, w2, g3, b3, tm=MOE_TM):
    T = h2.shape[0]
    return pl.pallas_call(
        _moe_kernel,
        out_shape=jax.ShapeDtypeStruct((T, D_MODEL), F32),
        grid_spec=pltpu.PrefetchScalarGridSpec(
            num_scalar_prefetch=1,
            grid=(T // tm, N_EXPERTS // MOE_EPS),
            in_specs=[
                pl.BlockSpec((tm, D_MODEL), lambda i, e, m: (i, 0)),
                pl.BlockSpec((1, SUBLANES, tm), lambda i, e, m: (i, 0, 0)),
                pl.BlockSpec((tm, LANES), lambda i, e, m: (i, 0)),
                pl.BlockSpec((MOE_EPS, D_MODEL, EXPERT_FF), lambda i, e, m: (e, 0, 0)),
                pl.BlockSpec((MOE_EPS, D_MODEL, EXPERT_FF), lambda i, e, m: (e, 0, 0)),
                pl.BlockSpec((MOE_EPS, EXPERT_FF, D_MODEL), lambda i, e, m: (e, 0, 0)),
                pl.BlockSpec((1, D_MODEL), lambda i, e, m: (0, 0)),
                pl.BlockSpec((1, D_MODEL), lambda i, e, m: (0, 0)),
            ],
            out_specs=pl.BlockSpec((tm, D_MODEL), lambda i, e, m: (i, 0)),
            scratch_shapes=[pltpu.VMEM((tm, D_MODEL), BF16), pltpu.VMEM((_moe_slots(tm), D_MODEL), BF16)],
        ),
        compiler_params=_cparams(("parallel", "arbitrary")),
        name="moe",
    )(meta, h2, rinfo, cinfo, w1, w3, w2, g3, b3)


def _moe_layer(h2, wg, bg, we, be, w1, w3, w2, g3, b3, tm=MOE_TM):
    pad = GROUP_ROWS - N_GROUPS
    wt = jnp.concatenate([jnp.pad(wg.astype(F32).T, ((0, pad), (0, 0))), we.astype(F32).T], axis=0)
    bgc = jnp.pad(bg.astype(F32).reshape(-1, 1), ((0, pad), (0, 0)))
    rinfo, cinfo, meta = _route(h2, wt, bgc, be.astype(F32).reshape(-1, 1), tm=tm)
    meta = meta[:, :2, :N_EXPERTS].reshape(-1)
    return _moe(h2, rinfo, cinfo, meta, w1, w3, w2, g3, b3, tm=tm)


def kernel(x, mem, ln_in_g, ln_in_b, w_in, hg_lb_logits, hg_norm_g, w_branch_a, w_branch_b, w_mix_out,
           ln1_g, ln1_b, xa_wq, xa_wk, xa_wv, xa_wo, ln2_g, ln2_b, router_wg, router_bg, router_we,
           router_be, moe_w1, moe_w3, moe_w2, ln3_g, ln3_b):
    B, S, D = x.shape
    T = B * S
    row = lambda a: a.reshape(1, -1).astype(F32)
    x2 = x.reshape(T, D)
    gin, bin_ = row(ln_in_g), row(ln_in_b)
    assert w_in.shape[0] == DEPTH
    l = 0
    proj = _in_proj(x2, gin, bin_, w_in[l].astype(BF16))
    oa = _hgrn2(proj, hg_lb_logits.astype(F32), row(hg_norm_g[l]), B, S, l)
    ob = _sb_attn(proj, B, S)
    h1 = _mix(x2, gin, bin_, oa, ob, proj, w_branch_a[l].astype(BF16), w_branch_b[l].astype(BF16),
              w_mix_out[l].astype(BF16), row(ln1_g[l]), row(ln1_b[l]))
    kx, vx = _xa_kv(mem, xa_wk[l].astype(BF16), xa_wv[l].astype(BF16))
    h2 = _xattn(h1, kx, vx, xa_wq[l].astype(BF16), xa_wo[l].astype(BF16), row(ln2_g[l]), row(ln2_b[l]), B, S)
    out = _moe_layer(h2, router_wg[l], router_bg[l], router_we[l], router_be[l],
                     moe_w1[l].astype(BF16), moe_w3[l].astype(BF16), moe_w2[l].astype(BF16),
                     row(ln3_g[l]), row(ln3_b[l]))
    return out.reshape(B, S, D)
```

```python
import functools

import jax
import jax.numpy as jnp
from jax import lax
from jax.experimental import pallas as pl
from jax.experimental.pallas import tpu as pltpu

F32 = jnp.float32
BF16 = jnp.bfloat16

D_MODEL = 1024
HG_HEADS = 4
HG_DIM = 128
HG_W = HG_HEADS * HG_DIM
HG_CHUNK = 64
HG_LOCAL_CHUNKS = 4
SB_HEADS = 8
SB_HEAD_DIM = 64
SB_W = SB_HEADS * SB_HEAD_DIM
SB_BLOCK = 128
XA_HEADS = 4
XA_HEAD_DIM = D_MODEL // XA_HEADS
N_GROUPS = 4
EXPERTS_PER_GROUP = 4
N_EXPERTS = N_GROUPS * EXPERTS_PER_GROUP
EXPERT_FF = 512
N_IN = HG_W * 4 + SB_W * 3 + D_MODEL * 2
COL_HG = 0
COL_SB = 4 * HG_W
COL_GATE = COL_SB + 3 * SB_W
LN_EPS = 1e-5
RMS_EPS = 1e-6
DEPTH = 1
DN_ALPHA = (2 * DEPTH) ** 0.25

LANES = 128
VMEM_LIMIT = 56 * 1024 * 1024


def _cparams(sem, flags=None):
    return pltpu.CompilerParams(dimension_semantics=sem, vmem_limit_bytes=VMEM_LIMIT, flags=flags)


def _layer_norm(x, g, b):
    mu = jnp.mean(x, axis=-1, keepdims=True)
    xc = x - mu
    var = jnp.mean(xc * xc, axis=-1, keepdims=True)
    return xc * lax.rsqrt(var + LN_EPS) * g + b


def _dot(a, b):
    return jnp.dot(a, b, preferred_element_type=F32)


def _dot_nt(a, b):
    return lax.dot_general(a, b, (((1,), (1,)), ((), ())), preferred_element_type=F32)


def _dot_tn(a, b):
    return lax.dot_general(a, b, (((0,), (0,)), ((), ())), preferred_element_type=F32)


def _split3(a):
    hi = a.astype(BF16)
    r1 = a - hi.astype(F32)
    mid = r1.astype(BF16)
    lo = (r1 - mid.astype(F32)).astype(BF16)
    return hi, mid, lo


def _const_spec(shape):
    nd = len(shape)
    return pl.BlockSpec(shape, lambda *_: (0,) * nd)


def _in_proj_kernel(x_ref, g_ref, b_ref, w_ref, o_ref, *, col_chunk):
    h = _layer_norm(x_ref[...], g_ref[...], b_ref[...]).astype(BF16)
    for c in range(N_IN // col_chunk):
        sl = slice(c * col_chunk, (c + 1) * col_chunk)
        o_ref[:, sl] = _dot(h, w_ref[:, sl]).astype(BF16)


def _in_proj(x2, g, b, w_bf, tm=512, col_chunk=512):
    T = x2.shape[0]
    return pl.pallas_call(
        functools.partial(_in_proj_kernel, col_chunk=col_chunk),
        out_shape=jax.ShapeDtypeStruct((T, N_IN), BF16),
        grid=(T // tm,),
        in_specs=[
            pl.BlockSpec((tm, D_MODEL), lambda i: (i, 0)),
            _const_spec((1, D_MODEL)),
            _const_spec((1, D_MODEL)),
            _const_spec((D_MODEL, N_IN)),
        ],
        out_specs=pl.BlockSpec((tm, N_IN), lambda i: (i, 0)),
        compiler_params=_cparams(("parallel",)),
        name="in_proj",
    )(x2, g, b, w_bf)


def _hgrn2_kernel(lbl_ref, ng_ref, q_ref, f_ref, i_ref, g_ref, o_ref,
                  state_ref, intra_ref, qs_ref, dl_ref, u_ref, *, layer, tb):
    C = HG_CHUNK
    H2 = C // 2

    @pl.when(pl.program_id(1) == 0)
    def _():
        state_ref[...] = jnp.zeros_like(state_ref)

    lbl = lbl_ref[...]
    ex = jnp.exp(lbl - jnp.max(lbl, axis=0, keepdims=True))
    lb = jnp.sum(ex[: layer + 1], axis=0, keepdims=True) / jnp.sum(ex, axis=0, keepdims=True)

    R = HG_LOCAL_CHUNKS * C
    row = lax.broadcasted_iota(jnp.int32, (R, R), 0)
    col = lax.broadcasted_iota(jnp.int32, (R, R), 1)
    same_chunk = (row // C) == (col // C)
    tri = (same_chunk & (col <= row)).astype(BF16)
    m_diag = ((row // H2) == (col // H2)) & (col <= row)
    m_cross = same_chunk & ((row % C) >= H2) & ((col % C) < H2)
    rrow = lax.broadcasted_iota(jnp.int32, (R, HG_W), 0)
    ng = ng_ref[...]
    heads = [slice(h * HG_DIM, (h + 1) * HG_DIM) for h in range(HG_HEADS)]

    def row_of_period(x, offset, period):
        out = jnp.broadcast_to(x[offset:offset + 1, :], x.shape)
        for k in range(1, R // period):
            out = jnp.where(rrow >= k * period, x[k * period + offset:k * period + offset + 1, :], out)
        return out

    def local(blk):
        rows = slice(blk * R, (blk + 1) * R)
        forget = lb + (1.0 - lb) * jax.nn.sigmoid(f_ref[rows, :].astype(F32))
        kk = 1.0 - forget
        hi, mid, lo = _split3(jnp.log(forget))
        b = _dot(tri, hi) + _dot(tri, mid) + _dot(tri, lo)
        q = q_ref[rows, :].astype(F32)
        d1 = b - row_of_period(b, H2 // 2 - 1, H2)
        q1 = (q * jnp.exp(d1)).astype(BF16)
        k1 = (kk * jnp.exp(-d1)).astype(BF16)
        e3 = jnp.exp(-jnp.abs(b - row_of_period(b, H2 - 1, C)))
        q3 = (q * e3).astype(BF16)
        k3 = (kk * e3).astype(BF16)
        kd = (kk * jnp.exp(row_of_period(b, C - 1, C) - b)).astype(BF16)
        qs_ref[rows, :] = (q * jnp.exp(b)).astype(BF16)
        for k in range(HG_LOCAL_CHUNKS):
            dl_ref[blk * HG_LOCAL_CHUNKS + k] = jnp.exp(b[k * C + C - 1:(k + 1) * C, :])
        s1 = [_dot_nt(q1[:, ls], k1[:, ls]) for ls in heads]
        s3 = [_dot_nt(q3[:, ls], k3[:, ls]) for ls in heads]
        s = [jnp.where(m_diag, a1, jnp.where(m_cross, a3, 0.0)).astype(BF16) for a1, a3 in zip(s1, s3)]
        for h, ls in enumerate(heads):
            v = i_ref[rows, ls]
            intra_ref[rows, ls] = _dot(s[h], v)
            for k in range(HG_LOCAL_CHUNKS):
                cs = slice(k * C, (k + 1) * C)
                u_ref[blk * HG_LOCAL_CHUNKS + k, h] = _dot_tn(v[cs], kd[cs, ls])

    def carry_state(c, carry):
        rows = pl.ds(pl.multiple_of(c * C, C), C)
        dl = dl_ref[c]
        for h, ls in enumerate(heads):
            st = state_ref[h]
            o = intra_ref[rows, ls] + _dot_nt(qs_ref[rows, ls], st.astype(BF16))
            state_ref[h] = st * dl[:, ls] + u_ref[c, h]
            rms = lax.rsqrt(jnp.mean(o * o, axis=-1, keepdims=True) + RMS_EPS)
            gate = g_ref[rows, ls].astype(F32)
            o_ref[rows, ls] = (o * rms * ng * (gate * jax.nn.sigmoid(gate))).astype(o_ref.dtype)
        return carry

    for blk in range(tb // R):
        local(blk)
    lax.fori_loop(0, tb // C, carry_state, 0, unroll=True)


def _hgrn2(proj, lb_logits, norm_g, B, S, layer, tb=512):
    T = B * S
    nsb = S // tb

    def spec(k):
        return pl.BlockSpec((tb, HG_W), lambda b, s, k=k: (b * nsb + s, COL_HG // HG_W + k))

    return pl.pallas_call(
        functools.partial(_hgrn2_kernel, layer=layer, tb=tb),
        out_shape=jax.ShapeDtypeStruct((T, HG_W), BF16),
        grid=(B, nsb),
        in_specs=[
            _const_spec(lb_logits.shape),
            _const_spec((1, HG_DIM)),
            spec(0), spec(1), spec(2), spec(3),
        ],
        out_specs=pl.BlockSpec((tb, HG_W), lambda b, s: (b * nsb + s, 0)),
        scratch_shapes=[pltpu.VMEM((HG_HEADS, HG_DIM, HG_DIM), F32),
                        pltpu.VMEM((tb, HG_W), F32),
                        pltpu.VMEM((tb, HG_W), BF16),
                        pltpu.VMEM((tb // HG_CHUNK, 1, HG_W), F32),
                        pltpu.VMEM((tb // HG_CHUNK, HG_HEADS, HG_DIM, HG_DIM), F32)],
        compiler_params=_cparams(("parallel", "arbitrary")),
        name="hgrn2",
    )(lb_logits, norm_g, proj, proj, proj, proj)


LOG2E = 1.4426950408889634
SB_DEAD = -160.0


def _sb_kernel(q_ref, k_ref, v_ref, o_ref, acc_ref, run_ref, *, npairs, tq):
    i = pl.program_id(1)
    BLK = SB_BLOCK
    W2 = 2 * BLK
    ndiag = tq // BLK
    scale = SB_HEAD_DIM ** -0.5
    lane = lax.broadcasted_iota(jnp.int32, (BLK, LANES), 1)
    head0 = lane < SB_HEAD_DIM
    zrow = lax.broadcasted_iota(jnp.int32, (tq, W2), 0)
    zcol = lax.broadcasted_iota(jnp.int32, (tq, W2), 1) & (BLK - 1)
    wr = lax.broadcasted_iota(jnp.int32, (W2, W2), 0) & (BLK - 1)
    wc = lax.broadcasted_iota(jnp.int32, (W2, W2), 1)
    cw = ((wc >= BLK) | (wr > wc)).astype(BF16)
    qs = q_ref[...] * jnp.asarray(scale, q_ref.dtype)

    def masked_kv(ref, j, p):
        x = ref[pl.ds(pl.multiple_of(j * BLK, BLK), BLK), p * LANES:(p + 1) * LANES]
        zero = jnp.zeros_like(x)
        return jnp.concatenate([jnp.where(head0, x, zero), jnp.where(head0, zero, x)], axis=0)

    def step(js, ds):
        nb = len(js)
        rsl = [slice((ds[b] or 0) * BLK, tq) for b in range(nb)]
        zs = [[_dot_nt(qs[rsl[b], p * LANES:(p + 1) * LANES], masked_kv(k_ref, js[b], p)) * LOG2E
               for p in range(npairs)] for b in range(nb)]
        ws = []
        for b in range(nb):
            nr = tq - rsl[b].start
            before = None if ds[b] is None else ((zcol + ds[b] * BLK) < zrow)[rsl[b]]
            lbs, hls = [], []
            for p in range(npairs):
                z = zs[b][p]
                l2 = jnp.log2(1.0 + jnp.exp2(-jnp.abs(z)))
                lb = jnp.minimum(z, 0.0) - l2
                lk = lb - z
                if before is not None:
                    lk = jnp.where(before, lk, 0.0)
                hi = lk.astype(BF16)
                lo = (lk - hi.astype(F32)).astype(BF16)
                lbs.append(lb)
                hls.append(jnp.concatenate([hi[:, :BLK], lo[:, :BLK]], axis=1))
                hls.append(jnp.concatenate([hi[:, BLK:], lo[:, BLK:]], axis=1))
            cs = _dot(jnp.concatenate(hls, axis=0), cw)
            wb = []
            for p in range(npairs):
                c0 = cs[(2 * p) * nr:(2 * p + 1) * nr]
                c1 = cs[(2 * p + 1) * nr:(2 * p + 2) * nr]
                cum = jnp.concatenate([c0[:, :BLK], c1[:, :BLK]], axis=1)
                rsum = jnp.concatenate([c0[:, BLK:], c1[:, BLK:]], axis=1)
                run = run_ref[p, rsl[b], :]
                w = jnp.exp2(lbs[p] + (run + cum))
                if before is not None:
                    w = jnp.where(before, w, 0.0)
                run_ref[p, rsl[b], :] = run + rsum
                wb.append(w.astype(BF16))
            ws.append(wb)
        for p in range(npairs):
            for b in range(nb):
                acc_ref[p, rsl[b], :] += _dot(ws[b][p], masked_kv(v_ref, js[b], p))

    acc_ref[...] = jnp.zeros_like(acc_ref)
    run_ref[...] = jnp.zeros_like(run_ref)
    step([i * ndiag + d for d in reversed(range(ndiag))], list(reversed(range(ndiag))))

    def live():
        return jnp.max(run_ref[...]) > SB_DEAD

    def cond(c):
        return jnp.logical_and(c[0] < i, c[1])

    def body(c):
        base = (i - 1 - c[0]) * ndiag
        step([base + d for d in reversed(range(ndiag))], [None] * ndiag)
        return c[0] + 1, live()

    lax.while_loop(cond, body, (jnp.int32(0), live()))
    for p in range(npairs):
        o_ref[:, p * LANES:(p + 1) * LANES] = acc_ref[p].astype(o_ref.dtype)


def _sb_attn(proj, B, S, npairs=SB_W // LANES, tq=256):
    T = B * S
    nq = S // tq
    wblk = npairs * LANES
    ng = SB_W // wblk
    cq = COL_SB // wblk
    ck = (COL_SB + SB_W) // wblk
    cv = (COL_SB + 2 * SB_W) // wblk
    return pl.pallas_call(
        functools.partial(_sb_kernel, npairs=npairs, tq=tq),
        out_shape=jax.ShapeDtypeStruct((T, SB_W), BF16),
        grid=(B * ng, nq),
        in_specs=[
            pl.BlockSpec((tq, wblk), lambda g, i: ((g // ng) * nq + i, cq + g % ng)),
            pl.BlockSpec((S, wblk), lambda g, i: (g // ng, ck + g % ng)),
            pl.BlockSpec((S, wblk), lambda g, i: (g // ng, cv + g % ng)),
        ],
        out_specs=pl.BlockSpec((tq, wblk), lambda g, i: ((g // ng) * nq + i, g % ng)),
        scratch_shapes=[pltpu.VMEM((npairs, tq, LANES), F32),
                        pltpu.VMEM((npairs, tq, 2 * SB_BLOCK), F32)],
        compiler_params=_cparams(("parallel", "arbitrary")),
        name="sb_attn",
    )(proj, proj, proj)


GATE_BLK = 512


def _mix_kernel(x_ref, gin_ref, bin_ref, oa_ref, ob_ref, ga0_ref, ga1_ref, gb0_ref, gb1_ref,
                wa_ref, wb_ref, wm_ref, g1_ref, b1_ref, o_ref):
    rs = x_ref.shape[0] // MIX_SUBTILES
    rows = [slice(k * rs, (k + 1) * rs) for k in range(MIX_SUBTILES)]
    halves = [(slice(0, GATE_BLK), ga0_ref, gb0_ref), (slice(GATE_BLK, 2 * GATE_BLK), ga1_ref, gb1_ref)]
    yab = [[(_dot(oa_ref[r, :], wa_ref[:, cs]), _dot(ob_ref[r, :], wb_ref[:, cs])) for cs, _, _ in halves]
           for r in rows]
    ys = []
    for k, r in enumerate(rows):
        y = None
        for (cs, ga_ref, gb_ref), (ya, yb) in zip(halves, yab[k]):
            merged = (jax.nn.sigmoid(ga_ref[r, :].astype(F32)) * ya
                      + jax.nn.sigmoid(gb_ref[r, :].astype(F32)) * yb)
            part = _dot(merged.astype(BF16), wm_ref[cs, :])
            y = part if y is None else y + part
        ys.append(y)
    for k, r in enumerate(rows):
        h0 = _layer_norm(x_ref[r, :], gin_ref[...], bin_ref[...])
        o_ref[r, :] = _layer_norm(DN_ALPHA * h0 + ys[k], g1_ref[...], b1_ref[...])


MIX_SUBTILES = 4


def _mix(x2, gin, bin_, oa, ob, proj, wa, wb, wm, g1, b1, tm=1024):
    T = x2.shape[0]
    c0 = COL_GATE // GATE_BLK

    def gate_spec(k):
        return pl.BlockSpec((tm, GATE_BLK), lambda i, k=k: (i, c0 + k))

    return pl.pallas_call(
        _mix_kernel,
        out_shape=jax.ShapeDtypeStruct((T, D_MODEL), F32),
        grid=(T // tm,),
        in_specs=[
            pl.BlockSpec((tm, D_MODEL), lambda i: (i, 0)),
            _const_spec((1, D_MODEL)), _const_spec((1, D_MODEL)),
            pl.BlockSpec((tm, HG_W), lambda i: (i, 0)),
            pl.BlockSpec((tm, SB_W), lambda i: (i, 0)),
            gate_spec(0), gate_spec(1), gate_spec(2), gate_spec(3),
            _const_spec((HG_W, D_MODEL)), _const_spec((SB_W, D_MODEL)), _const_spec((D_MODEL, D_MODEL)),
            _const_spec((1, D_MODEL)), _const_spec((1, D_MODEL)),
        ],
        out_specs=pl.BlockSpec((tm, D_MODEL), lambda i: (i, 0)),
        compiler_params=_cparams(("parallel",)),
        name="mix",
    )(x2, gin, bin_, oa, ob, proj, proj, proj, proj, wa, wb, wm, g1, b1)


def _xa_kv_kernel(m_ref, wk_ref, wv_ref, k_ref, v_ref):
    m = m_ref[0].astype(BF16)
    k_ref[0] = _dot(m, wk_ref[...]).astype(BF16)
    v_ref[0] = _dot(m, wv_ref[...]).astype(BF16)


def _xa_kv(mem, wk, wv):
    B, M, _ = mem.shape
    W = wk.shape[1]
    return pl.pallas_call(
        _xa_kv_kernel,
        out_shape=(jax.ShapeDtypeStruct((B, M, W), BF16), jax.ShapeDtypeStruct((B, M, W), BF16)),
        grid=(B,),
        in_specs=[pl.BlockSpec((1, M, D_MODEL), lambda b: (b, 0, 0)),
                  _const_spec(wk.shape), _const_spec(wv.shape)],
        out_specs=(pl.BlockSpec((1, M, W), lambda b: (b, 0, 0)),
                   pl.BlockSpec((1, M, W), lambda b: (b, 0, 0))),
        compiler_params=_cparams(("parallel",)),
        name="xa_kv",
    )(mem, wk, wv)


XA_SUBTILES = 4


def _xattn_kernel(h_ref, k_ref, v_ref, wq_ref, wo_ref, g_ref, b_ref, o_ref):
    nsub = XA_SUBTILES
    rs = h_ref.shape[0] // nsub
    rows = [slice(k * rs, (k + 1) * rs) for k in range(nsub)]
    heads = [slice(hd * XA_HEAD_DIM, (hd + 1) * XA_HEAD_DIM) for hd in range(XA_HEADS)]
    scale = XA_HEAD_DIM ** -0.5
    hs = [h_ref[r, :] for r in rows]
    qs = [_dot(h.astype(BF16), wq_ref[...]).astype(BF16) for h in hs]
    ss = [[_dot_nt(q[:, ls], k_ref[0, :, ls]) * scale for ls in heads] for q in qs]
    os_ = []
    for k in range(nsub):
        outs = []
        for hd, ls in enumerate(heads):
            s = ss[k][hd]
            p = jnp.exp(s - jnp.max(s, axis=-1, keepdims=True))
            p = p / jnp.sum(p, axis=-1, keepdims=True)
            outs.append(_dot(p.astype(BF16), v_ref[0, :, ls]).astype(BF16))
        os_.append(jnp.concatenate(outs, axis=-1))
    ys = [_dot(o, wo_ref[...]) for o in os_]
    for k in range(nsub):
        o_ref[rows[k], :] = _layer_norm(DN_ALPHA * hs[k] + ys[k], g_ref[...], b_ref[...])


def _xattn(h1, kx, vx, wq, wo, g2, b2, B, S, tm=1024):
    T = B * S
    M = kx.shape[1]
    W = kx.shape[2]
    nsb = S // tm
    return pl.pallas_call(
        _xattn_kernel,
        out_shape=jax.ShapeDtypeStruct((T, D_MODEL), F32),
        grid=(B, nsb),
        in_specs=[
            pl.BlockSpec((tm, D_MODEL), lambda b, s: (b * nsb + s, 0)),
            pl.BlockSpec((1, M, W), lambda b, s: (b, 0, 0)),
            pl.BlockSpec((1, M, W), lambda b, s: (b, 0, 0)),
            _const_spec(wq.shape), _const_spec(wo.shape),
            _const_spec((1, D_MODEL)), _const_spec((1, D_MODEL)),
        ],
        out_specs=pl.BlockSpec((tm, D_MODEL), lambda b, s: (b * nsb + s, 0)),
        compiler_params=_cparams(("parallel", "arbitrary")),
        name="xattn",
    )(h1, kx, vx, wq, wo, g2, b2)


MOE_TM = 1024
MOE_M1 = 160
MOE_M2 = 64
MOE_EPS = 2
SEG_ALIGN = 16
MOE_QCH = 512
MOE_OUT_SUBTILES = 4


def _moe_slots(tm):
    return -(-(2 * tm + N_EXPERTS * SEG_ALIGN + MOE_M1 + MOE_M2) // MOE_QCH) * MOE_QCH


def _seg_pad(cnt):
    return jnp.ceil(cnt * (1.0 / SEG_ALIGN)) * SEG_ALIGN
SUBLANES = 8
GROUP_ROWS = 16


def _hdot(a, b):
    return jnp.dot(a, b, preferred_element_type=F32, precision=lax.Precision.HIGHEST)


def _split2(a):
    hi = a.astype(BF16)
    return hi, (a - hi.astype(F32)).astype(BF16)


def _first_argmax0(vals, row, height):
    m = jnp.max(vals, axis=0, keepdims=True)
    idx = jnp.min(jnp.where(vals == m, row, height), axis=0, keepdims=True)
    return m, idx


def _select_rows(shape, rows):
    r = lax.broadcasted_iota(jnp.int32, shape, 0)
    out = jnp.zeros(shape, F32)
    for k, v in enumerate(rows):
        out = jnp.where(r == k, v, out)
    return out


def _route_kernel(h_ref, wt_ref, bg_ref, be_ref, rinfo_ref, cinfo_ref, meta_ref):
    tm = h_ref.shape[0]
    h = h_ref[...]
    neg = jnp.float32(-jnp.inf)
    rg = lax.broadcasted_iota(jnp.int32, (GROUP_ROWS, tm), 0)
    re = lax.broadcasted_iota(jnp.int32, (N_EXPERTS, tm), 0)
    nw = GROUP_ROWS + N_EXPERTS
    hh, hl = _split2(h)
    wh, wl = _split2(wt_ref[...])
    lg = _dot_nt(jnp.concatenate([wh, wl], axis=0), hh)
    lg = lg[:nw] + (lg[nw:] + _dot_nt(wh, hl))
    gl = jnp.where(rg < N_GROUPS, lg[:GROUP_ROWS] + bg_ref[...], neg)
    el = lg[GROUP_ROWS:] + be_ref[...]
    gm, g_idx = _first_argmax0(gl, rg, GROUP_ROWS)
    g_top = 1.0 / jnp.sum(jnp.exp(gl - gm), axis=0, keepdims=True)
    ml = jnp.where(re // EXPERTS_PER_GROUP == g_idx, el, neg)
    m1, i1 = _first_argmax0(ml, re, N_EXPERTS)
    ml2 = jnp.where(re == i1, neg, ml)
    m2, i2 = _first_argmax0(ml2, re, N_EXPERTS)
    e2 = jnp.exp(m2 - m1)
    gate0 = g_top / (1.0 + e2)
    gate1 = g_top * e2 / (1.0 + e2)

    oh0 = jnp.where(re == i1, 1.0, 0.0)
    oh1 = jnp.where(re == i2, 1.0, 0.0)
    cnt0 = jnp.sum(oh0, axis=1, keepdims=True)
    cnt1 = jnp.sum(oh1, axis=1, keepdims=True)
    er = lax.broadcasted_iota(jnp.int32, (N_EXPERTS, N_EXPERTS), 0)
    ec = lax.broadcasted_iota(jnp.int32, (N_EXPERTS, N_EXPERTS), 1)
    seg = _hdot(jnp.where(ec < er, 1.0, 0.0),
                jnp.broadcast_to(_seg_pad(cnt0 + cnt1), (N_EXPERTS, LANES)))[:, 0:1]
    tr = lax.broadcasted_iota(jnp.int32, (tm, tm), 0)
    tc = lax.broadcasted_iota(jnp.int32, (tm, tm), 1)
    earlier = jnp.where(tr < tc, 1.0, 0.0).astype(BF16)
    cum = _dot(jnp.concatenate([oh0, oh1], axis=0).astype(BF16), earlier)
    slot0 = jnp.sum(oh0 * (seg + cum[:N_EXPERTS]), axis=0, keepdims=True)
    slot1 = jnp.sum(oh1 * (seg + cnt0 + cum[N_EXPERTS:]), axis=0, keepdims=True)

    info = (slot0, slot1, gate0, gate1)
    rinfo_ref[0] = _select_rows((SUBLANES, tm), info)
    cinfo_ref[...] = _select_rows((LANES, tm), info).T

    ohs = jnp.concatenate([oh0 + oh1, jnp.zeros((LANES - N_EXPERTS, tm), F32)], axis=0).astype(BF16)
    cnt_row = _dot_nt(jnp.ones((SUBLANES, tm), BF16), ohs)
    lr = lax.broadcasted_iota(jnp.int32, (LANES, LANES), 0)
    lc = lax.broadcasted_iota(jnp.int32, (LANES, LANES), 1)
    seg_row = _hdot(_seg_pad(cnt_row), jnp.where(lr < lc, 1.0, 0.0))
    r8 = lax.broadcasted_iota(jnp.int32, (SUBLANES, LANES), 0)
    meta_ref[0] = jnp.where(r8 == 0, seg_row, jnp.where(r8 == 1, cnt_row, 0.0)).astype(jnp.int32)


def _route(h2, wt, bg, be, tm=MOE_TM):
    T = h2.shape[0]
    nt = T // tm
    return pl.pallas_call(
        _route_kernel,
        out_shape=(jax.ShapeDtypeStruct((nt, SUBLANES, tm), F32),
                   jax.ShapeDtypeStruct((T, LANES), F32),
                   jax.ShapeDtypeStruct((nt, SUBLANES, LANES), jnp.int32)),
        grid=(nt,),
        in_specs=[pl.BlockSpec((tm, D_MODEL), lambda i: (i, 0)),
                  _const_spec(wt.shape), _const_spec(bg.shape), _const_spec(be.shape)],
        out_specs=(pl.BlockSpec((1, SUBLANES, tm), lambda i: (i, 0, 0)),
                   pl.BlockSpec((tm, LANES), lambda i: (i, 0)),
                   pl.BlockSpec((1, SUBLANES, LANES), lambda i: (i, 0, 0))),
        compiler_params=_cparams(("parallel",)),
        name="moe_route",
    )(h2, wt, bg, be)


def _moe_kernel(meta_ref, h_ref, rinfo_ref, cinfo_ref, w1_ref, w3_ref, w2_ref, g_ref, b_ref,
                o_ref, hb_ref, ys_ref):
    t = pl.program_id(0)
    eg = pl.program_id(1)
    tm = h_ref.shape[0]

    @pl.when(eg == 0)
    def _():
        hb_ref[...] = h_ref[...].astype(BF16)
        ys_ref[...] = jnp.zeros_like(ys_ref)

    def chunks(js, r0s, m):
        r0s = [pl.multiple_of(r0, SEG_ALIGN) for r0 in r0s]
        info = rinfo_ref[0]
        slot0 = info[0:1].astype(jnp.int32)
        slot1 = info[1:2].astype(jnp.int32)
        xs = []
        for r0 in r0s:
            s = lax.broadcasted_iota(jnp.int32, (m, tm), 0) + r0
            pick = jnp.where((s == slot0) | (s == slot1), 1.0, 0.0).astype(BF16)
            xs.append(_dot(pick, hb_ref[...]).astype(BF16))
        ab = [(_dot(x, w1_ref[j]), _dot(x, w3_ref[j])) for j, x in zip(js, xs)]
        hids = [(a * jax.nn.sigmoid(a) * b).astype(BF16) for a, b in ab]
        ys = [_dot(hid, w2_ref[j]).astype(BF16) for j, hid in zip(js, hids)]
        for r0, y in zip(r0s, ys):
            ys_ref[pl.ds(r0, m), :] = y

    base = eg * MOE_EPS
    starts = [meta_ref[(2 * t) * N_EXPERTS + base + j] for j in range(MOE_EPS)]
    cnts = [meta_ref[(2 * t + 1) * N_EXPERTS + base + j] for j in range(MOE_EPS)]

    for j in range(MOE_EPS):
        @pl.when(cnts[j] > MOE_M1)
        def _(j=j):
            def extra(k, c):
                chunks([j], [starts[j] + MOE_M1 + k * MOE_M2], MOE_M2)
                return c

            lax.fori_loop(0, (cnts[j] - MOE_M1 + MOE_M2 - 1) // MOE_M2, extra, 0)

    chunks(list(range(MOE_EPS)), starts, MOE_M1)

    @pl.when(eg == N_EXPERTS // MOE_EPS - 1)
    def _():
        rs = tm // MOE_OUT_SUBTILES
        for k in range(MOE_OUT_SUBTILES):
            r = slice(k * rs, (k + 1) * rs)
            ci = cinfo_ref[r, :]
            slot0 = ci[:, 0:1].astype(jnp.int32)
            slot1 = ci[:, 1:2].astype(jnp.int32)
            y = None
            for c in range(ys_ref.shape[0] // MOE_QCH):
                sc = lax.broadcasted_iota(jnp.int32, (rs, MOE_QCH), 1) + c * MOE_QCH
                q = (jnp.where(sc == slot0, ci[:, 2:3], 0.0)
                     + jnp.where(sc == slot1, ci[:, 3:4], 0.0)).astype(BF16)
                part = _dot(q, ys_ref[c * MOE_QCH:(c + 1) * MOE_QCH, :])
                y = part if y is None else y + part
            o_ref[r, :] = _layer_norm(DN_ALPHA * h_ref[r, :] + y, g_ref[...], b_ref[...])


def _moe(h2, rinfo, cinfo, meta, w1, w3, w2, g3, b3, tm=MOE_TM):
    T = h2.shape[0]
    return pl.pallas_call(
        _moe_kernel,
        out_shape=jax.ShapeDtypeStruct((T, D_MODEL), F32),
        grid_spec=pltpu.PrefetchScalarGridSpec(
            num_scalar_prefetch=1,
            grid=(T // tm, N_EXPERTS // MOE_EPS),
            in_specs=[
                pl.BlockSpec((tm, D_MODEL), lambda i, e, m: (i, 0)),
                pl.BlockSpec((1, SUBLANES, tm), lambda i, e, m: (i, 0, 0)),
                pl.BlockSpec((tm, LANES), lambda i, e, m: (i, 0)),
                pl.BlockSpec((MOE_EPS, D_MODEL, EXPERT_FF), lambda i, e, m: (e, 0, 0)),
                pl.BlockSpec((MOE_EPS, D_MODEL, EXPERT_FF), lambda i, e, m: (e, 0, 0)),
                pl.BlockSpec((MOE_EPS, EXPERT_FF, D_MODEL), lambda i, e, m: (e, 0, 0)),
                pl.BlockSpec((1, D_MODEL), lambda i, e, m: (0, 0)),
                pl.BlockSpec((1, D_MODEL), lambda i, e, m: (0, 0)),
            ],
            out_specs=pl.BlockSpec((tm, D_MODEL), lambda i, e, m: (i, 0)),
            scratch_shapes=[pltpu.VMEM((tm, D_MODEL), BF16), pltpu.VMEM((_moe_slots(tm), D_MODEL), BF16)],
        ),
        compiler_params=_cparams(("parallel", "arbitrary")),
        name="moe",
    )(meta, h2, rinfo, cinfo, w1, w3, w2, g3, b3)


def _moe_layer(h2, wg, bg, we, be, w1, w3, w2, g3, b3, tm=MOE_TM):
    pad = GROUP_ROWS - N_GROUPS
    wt = jnp.concatenate([jnp.pad(wg.astype(F32).T, ((0, pad), (0, 0))), we.astype(F32).T], axis=0)
    bgc = jnp.pad(bg.astype(F32).reshape(-1, 1), ((0, pad), (0, 0)))
    rinfo, cinfo, meta = _route(h2, wt, bgc, be.astype(F32).reshape(-1, 1), tm=tm)
    meta = meta[:, :2, :N_EXPERTS].reshape(-1)
    return _moe(h2, rinfo, cinfo, meta, w1, w3, w2, g3, b3, tm=tm)


def kernel(x, mem, ln_in_g, ln_in_b, w_in, hg_lb_logits, hg_norm_g, w_branch_a, w_branch_b, w_mix_out,
           ln1_g, ln1_b, xa_wq, xa_wk, xa_wv, xa_wo, ln2_g, ln2_b, router_wg, router_bg, router_we,
           router_be, moe_w1, moe_w3, moe_w2, ln3_g, ln3_b):
    B, S, D = x.shape
    T = B * S
    row = lambda a: a.reshape(1, -1).astype(F32)
    x2 = x.reshape(T, D)
    gin, bin_ = row(ln_in_g), row(ln_in_b)
    assert w_in.shape[0] == DEPTH
    l = 0
    proj = _in_proj(x2, gin, bin_, w_in[l].astype(BF16))
    oa = _hgrn2(proj, hg_lb_logits.astype(F32), row(hg_norm_g[l]), B, S, l)
    ob = _sb_attn(proj, B, S)
    h1 = _mix(x2, gin, bin_, oa, ob, proj, w_branch_a[l].astype(BF16), w_branch_b[l].astype(BF16),
              w_mix_out[l].astype(BF16), row(ln1_g[l]), row(ln1_b[l]))
    kx, vx = _xa_kv(mem, xa_wk[l].astype(BF16), xa_wv[l].astype(BF16))
    h2 = _xattn(h1, kx, vx, xa_wq[l].astype(BF16), xa_wo[l].astype(BF16), row(ln2_g[l]), row(ln2_b[l]), B, S)
    out = _moe_layer(h2, router_wg[l], router_bg[l], router_we[l], router_be[l],
                     moe_w1[l].astype(BF16), moe_w3[l].astype(BF16), moe_w2[l].astype(BF16),
                     row(ln3_g[l]), row(ln3_b[l]))
    return out.reshape(B, S, D)
```

```python
import functools

import jax
import jax.numpy as jnp
from jax import lax
from jax.experimental import pallas as pl
from jax.experimental.pallas import tpu as pltpu

F32 = jnp.float32
BF16 = jnp.bfloat16

D_MODEL = 1024
HG_HEADS = 4
HG_DIM = 128
HG_W = HG_HEADS * HG_DIM
HG_CHUNK = 64
HG_LOCAL_CHUNKS = 4
SB_HEADS = 8
SB_HEAD_DIM = 64
SB_W = SB_HEADS * SB_HEAD_DIM
SB_BLOCK = 128
XA_HEADS = 4
XA_HEAD_DIM = D_MODEL // XA_HEADS
N_GROUPS = 4
EXPERTS_PER_GROUP = 4
N_EXPERTS = N_GROUPS * EXPERTS_PER_GROUP
EXPERT_FF = 512
N_IN = HG_W * 4 + SB_W * 3 + D_MODEL * 2
COL_HG = 0
COL_SB = 4 * HG_W
COL_GATE = COL_SB + 3 * SB_W
LN_EPS = 1e-5
RMS_EPS = 1e-6
DEPTH = 1
DN_ALPHA = (2 * DEPTH) ** 0.25

LANES = 128
VMEM_LIMIT = 56 * 1024 * 1024


def _cparams(sem, flags=None):
    return pltpu.CompilerParams(dimension_semantics=sem, vmem_limit_bytes=VMEM_LIMIT, flags=flags)


def _layer_norm(x, g, b):
    mu = jnp.mean(x, axis=-1, keepdims=True)
    xc = x - mu
    var = jnp.mean(xc * xc, axis=-1, keepdims=True)
    return xc * lax.rsqrt(var + LN_EPS) * g + b


def _dot(a, b):
    return jnp.dot(a, b, preferred_element_type=F32)


def _dot_nt(a, b):
    return lax.dot_general(a, b, (((1,), (1,)), ((), ())), preferred_element_type=F32)


def _dot_tn(a, b):
    return lax.dot_general(a, b, (((0,), (0,)), ((), ())), preferred_element_type=F32)


def _split3(a):
    hi = a.astype(BF16)
    r1 = a - hi.astype(F32)
    mid = r1.astype(BF16)
    lo = (r1 - mid.astype(F32)).astype(BF16)
    return hi, mid, lo


def _const_spec(shape):
    nd = len(shape)
    return pl.BlockSpec(shape, lambda *_: (0,) * nd)


def _in_proj_kernel(x_ref, g_ref, b_ref, w_ref, o_ref, *, col_chunk):
    h = _layer_norm(x_ref[...], g_ref[...], b_ref[...]).astype(BF16)
    for c in range(N_IN // col_chunk):
        sl = slice(c * col_chunk, (c + 1) * col_chunk)
        o_ref[:, sl] = _dot(h, w_ref[:, sl]).astype(BF16)


def _in_proj(x2, g, b, w_bf, tm=512, col_chunk=512):
    T = x2.shape[0]
    return pl.pallas_call(
        functools.partial(_in_proj_kernel, col_chunk=col_chunk),
        out_shape=jax.ShapeDtypeStruct((T, N_IN), BF16),
        grid=(T // tm,),
        in_specs=[
            pl.BlockSpec((tm, D_MODEL), lambda i: (i, 0)),
            _const_spec((1, D_MODEL)),
            _const_spec((1, D_MODEL)),
            _const_spec((D_MODEL, N_IN)),
        ],
        out_specs=pl.BlockSpec((tm, N_IN), lambda i: (i, 0)),
        compiler_params=_cparams(("parallel",)),
        name="in_proj",
    )(x2, g, b, w_bf)


def _hgrn2_kernel(lbl_ref, ng_ref, q_ref, f_ref, i_ref, g_ref, o_ref,
                  state_ref, intra_ref, qs_ref, dl_ref, u_ref, *, layer, tb):
    C = HG_CHUNK
    H2 = C // 2

    @pl.when(pl.program_id(1) == 0)
    def _():
        state_ref[...] = jnp.zeros_like(state_ref)

    lbl = lbl_ref[...]
    ex = jnp.exp(lbl - jnp.max(lbl, axis=0, keepdims=True))
    lb = jnp.sum(ex[: layer + 1], axis=0, keepdims=True) / jnp.sum(ex, axis=0, keepdims=True)

    R = HG_LOCAL_CHUNKS * C
    row = lax.broadcasted_iota(jnp.int32, (R, R), 0)
    col = lax.broadcasted_iota(jnp.int32, (R, R), 1)
    same_chunk = (row // C) == (col // C)
    tri = (same_chunk & (col <= row)).astype(BF16)
    m_diag = ((row // H2) == (col // H2)) & (col <= row)
    m_cross = same_chunk & ((row % C) >= H2) & ((col % C) < H2)
    rrow = lax.broadcasted_iota(jnp.int32, (R, HG_W), 0)
    ng = ng_ref[...]
    heads = [slice(h * HG_DIM, (h + 1) * HG_DIM) for h in range(HG_HEADS)]

    def row_of_period(x, offset, period):
        out = jnp.broadcast_to(x[offset:offset + 1, :], x.shape)
        for k in range(1, R // period):
            out = jnp.where(rrow >= k * period, x[k * period + offset:k * period + offset + 1, :], out)
        return out

    def local(blk):
        rows = slice(blk * R, (blk + 1) * R)
        forget = lb + (1.0 - lb) * jax.nn.sigmoid(f_ref[rows, :].astype(F32))
        kk = 1.0 - forget
        hi, mid, lo = _split3(jnp.log(forget))
        b = _dot(tri, hi) + _dot(tri, mid) + _dot(tri, lo)
        q = q_ref[rows, :].astype(F32)
        d1 = b - row_of_period(b, H2 // 2 - 1, H2)
        q1 = (q * jnp.exp(d1)).astype(BF16)
        k1 = (kk * jnp.exp(-d1)).astype(BF16)
        e3 = jnp.exp(-jnp.abs(b - row_of_period(b, H2 - 1, C)))
        q3 = (q * e3).astype(BF16)
        k3 = (kk * e3).astype(BF16)
        kd = (kk * jnp.exp(row_of_period(b, C - 1, C) - b)).astype(BF16)
        qs_ref[rows, :] = (q * jnp.exp(b)).astype(BF16)
        for k in range(HG_LOCAL_CHUNKS):
            dl_ref[blk * HG_LOCAL_CHUNKS + k] = jnp.exp(b[k * C + C - 1:(k + 1) * C, :])
        s1 = [_dot_nt(q1[:, ls], k1[:, ls]) for ls in heads]
        s3 = [_dot_nt(q3[:, ls], k3[:, ls]) for ls in heads]
        s = [jnp.where(m_diag, a1, jnp.where(m_cross, a3, 0.0)).astype(BF16) for a1, a3 in zip(s1, s3)]
        for h, ls in enumerate(heads):
            v = i_ref[rows, ls]
            intra_ref[rows, ls] = _dot(s[h], v)
            for k in range(HG_LOCAL_CHUNKS):
                cs = slice(k * C, (k + 1) * C)
                u_ref[blk * HG_LOCAL_CHUNKS + k, h] = _dot_tn(v[cs], kd[cs, ls])

    def carry_state(c, carry):
        rows = pl.ds(pl.multiple_of(c * C, C), C)
        dl = dl_ref[c]
        for h, ls in enumerate(heads):
            st = state_ref[h]
            o = intra_ref[rows, ls] + _dot_nt(qs_ref[rows, ls], st.astype(BF16))
            state_ref[h] = st * dl[:, ls] + u_ref[c, h]
            rms = lax.rsqrt(jnp.mean(o * o, axis=-1, keepdims=True) + RMS_EPS)
            gate = g_ref[rows, ls].astype(F32)
            o_ref[rows, ls] = (o * rms * ng * (gate * jax.nn.sigmoid(gate))).astype(o_ref.dtype)
        return carry

    for blk in range(tb // R):
        local(blk)
    lax.fori_loop(0, tb // C, carry_state, 0, unroll=True)


def _hgrn2(proj, lb_logits, norm_g, B, S, layer, tb=512):
    T = B * S
    nsb = S // tb

    def spec(k):
        return pl.BlockSpec((tb, HG_W), lambda b, s, k=k: (b * nsb + s, COL_HG // HG_W + k))

    return pl.pallas_call(
        functools.partial(_hgrn2_kernel, layer=layer, tb=tb),
        out_shape=jax.ShapeDtypeStruct((T, HG_W), BF16),
        grid=(B, nsb),
        in_specs=[
            _const_spec(lb_logits.shape),
            _const_spec((1, HG_DIM)),
            spec(0), spec(1), spec(2), spec(3),
        ],
        out_specs=pl.BlockSpec((tb, HG_W), lambda b, s: (b * nsb + s, 0)),
        scratch_shapes=[pltpu.VMEM((HG_HEADS, HG_DIM, HG_DIM), F32),
                        pltpu.VMEM((tb, HG_W), F32),
                        pltpu.VMEM((tb, HG_W), BF16),
                        pltpu.VMEM((tb // HG_CHUNK, 1, HG_W), F32),
                        pltpu.VMEM((tb // HG_CHUNK, HG_HEADS, HG_DIM, HG_DIM), F32)],
        compiler_params=_cparams(("parallel", "arbitrary")),
        name="hgrn2",
    )(lb_logits, norm_g, proj, proj, proj, proj)


LOG2E = 1.4426950408889634
SB_DEAD = -160.0


def _sb_kernel(q_ref, k_ref, v_ref, o_ref, acc_ref, run_ref, *, npairs, tq):
    i = pl.program_id(1)
    BLK = SB_BLOCK
    W2 = 2 * BLK
    ndiag = tq // BLK
    scale = SB_HEAD_DIM ** -0.5
    lane = lax.broadcasted_iota(jnp.int32, (BLK, LANES), 1)
    head0 = lane < SB_HEAD_DIM
    zrow = lax.broadcasted_iota(jnp.int32, (tq, W2), 0)
    zcol = lax.broadcasted_iota(jnp.int32, (tq, W2), 1) & (BLK - 1)
    wr = lax.broadcasted_iota(jnp.int32, (W2, W2), 0) & (BLK - 1)
    wc = lax.broadcasted_iota(jnp.int32, (W2, W2), 1)
    cw = ((wc >= BLK) | (wr > wc)).astype(BF16)
    qs = q_ref[...] * jnp.asarray(scale, q_ref.dtype)

    def masked_kv(ref, j, p):
        x = ref[pl.ds(pl.multiple_of(j * BLK, BLK), BLK), p * LANES:(p + 1) * LANES]
        zero = jnp.zeros_like(x)
        return jnp.concatenate([jnp.where(head0, x, zero), jnp.where(head0, zero, x)], axis=0)

    def step(js, ds):
        nb = len(js)
        rsl = [slice((ds[b] or 0) * BLK, tq) for b in range(nb)]
        zs = [[_dot_nt(qs[rsl[b], p * LANES:(p + 1) * LANES], masked_kv(k_ref, js[b], p)) * LOG2E
               for p in range(npairs)] for b in range(nb)]
        ws = []
        for b in range(nb):
            nr = tq - rsl[b].start
            before = None if ds[b] is None else ((zcol + ds[b] * BLK) < zrow)[rsl[b]]
            lbs, hls = [], []
            for p in range(npairs):
                z = zs[b][p]
                l2 = jnp.log2(1.0 + jnp.exp2(-jnp.abs(z)))
                lb = jnp.minimum(z, 0.0) - l2
                lk = lb - z
                if before is not None:
                    lk = jnp.where(before, lk, 0.0)
                hi = lk.astype(BF16)
                lo = (lk - hi.astype(F32)).astype(BF16)
                lbs.append(lb)
                hls.append(jnp.concatenate([hi[:, :BLK], lo[:, :BLK]], axis=1))
                hls.append(jnp.concatenate([hi[:, BLK:], lo[:, BLK:]], axis=1))
            cs = _dot(jnp.concatenate(hls, axis=0), cw)
            wb = []
            for p in range(npairs):
                c0 = cs[(2 * p) * nr:(2 * p + 1) * nr]
                c1 = cs[(2 * p + 1) * nr:(2 * p + 2) * nr]
                cum = jnp.concatenate([c0[:, :BLK], c1[:, :BLK]], axis=1)
                rsum = jnp.concatenate([c0[:, BLK:], c1[:, BLK:]], axis=1)
                run = run_ref[p, rsl[b], :]
                w = jnp.exp2(lbs[p] + (run + cum))
                if before is not None:
                    w = jnp.where(before, w, 0.0)
                run_ref[p, rsl[b], :] = run + rsum
                wb.append(w.astype(BF16))
            ws.append(wb)
        for p in range(npairs):
            for b in range(nb):
                acc_ref[p, rsl[b], :] += _dot(ws[b][p], masked_kv(v_ref, js[b], p))

    acc_ref[...] = jnp.zeros_like(acc_ref)
    run_ref[...] = jnp.zeros_like(run_ref)
    step([i * ndiag + d for d in reversed(range(ndiag))], list(reversed(range(ndiag))))

    def live():
        return jnp.max(run_ref[...]) > SB_DEAD

    def cond(c):
        return jnp.logical_and(c[0] < i, c[1])

    def body(c):
        base = (i - 1 - c[0]) * ndiag
        step([base + d for d in reversed(range(ndiag))], [None] * ndiag)
        return c[0] + 1, live()

    lax.while_loop(cond, body, (jnp.int32(0), live()))
    for p in range(npairs):
        o_ref[:, p * LANES:(p + 1) * LANES] = acc_ref[p].astype(o_ref.dtype)


def _sb_attn(proj, B, S, npairs=SB_W // LANES, tq=256):
    T = B * S
    nq = S // tq
    wblk = npairs * LANES
    ng = SB_W // wblk
    cq = COL_SB // wblk
    ck = (COL_SB + SB_W) // wblk
    cv = (COL_SB + 2 * SB_W) // wblk
    return pl.pallas_call(
        functools.partial(_sb_kernel, npairs=npairs, tq=tq),
        out_shape=jax.ShapeDtypeStruct((T, SB_W), BF16),
        grid=(B * ng, nq),
        in_specs=[
            pl.BlockSpec((tq, wblk), lambda g, i: ((g // ng) * nq + i, cq + g % ng)),
            pl.BlockSpec((S, wblk), lambda g, i: (g // ng, ck + g % ng)),
            pl.BlockSpec((S, wblk), lambda g, i: (g // ng, cv + g % ng)),
        ],
        out_specs=pl.BlockSpec((tq, wblk), lambda g, i: ((g // ng) * nq + i, g % ng)),
        scratch_shapes=[pltpu.VMEM((npairs, tq, LANES), F32),
                        pltpu.VMEM((npairs, tq, 2 * SB_BLOCK), F32)],
        compiler_params=_cparams(("parallel", "arbitrary")),
        name="sb_attn",
    )(proj, proj, proj)


GATE_BLK = 512


def _mix_kernel(x_ref, gin_ref, bin_ref, oa_ref, ob_ref, ga0_ref, ga1_ref, gb0_ref, gb1_ref,
                wa_ref, wb_ref, wm_ref, g1_ref, b1_ref, o_ref):
    rs = x_ref.shape[0] // MIX_SUBTILES
    rows = [slice(k * rs, (k + 1) * rs) for k in range(MIX_SUBTILES)]
    halves = [(slice(0, GATE_BLK), ga0_ref, gb0_ref), (slice(GATE_BLK, 2 * GATE_BLK), ga1_ref, gb1_ref)]
    yab = [[(_dot(oa_ref[r, :], wa_ref[:, cs]), _dot(ob_ref[r, :], wb_ref[:, cs])) for cs, _, _ in halves]
           for r in rows]
    ys = []
    for k, r in enumerate(rows):
        y = None
        for (cs, ga_ref, gb_ref), (ya, yb) in zip(halves, yab[k]):
            merged = (jax.nn.sigmoid(ga_ref[r, :].astype(F32)) * ya
                      + jax.nn.sigmoid(gb_ref[r, :].astype(F32)) * yb)
            part = _dot(merged.astype(BF16), wm_ref[cs, :])
            y = part if y is None else y + part
        ys.append(y)
    for k, r in enumerate(rows):
        h0 = _layer_norm(x_ref[r, :], gin_ref[...], bin_ref[...])
        o_ref[r, :] = _layer_norm(DN_ALPHA * h0 + ys[k], g1_ref[...], b1_ref[...])


MIX_SUBTILES = 4


def _mix(x2, gin, bin_, oa, ob, proj, wa, wb, wm, g1, b1, tm=1024):
    T = x2.shape[0]
    c0 = COL_GATE // GATE_BLK

    def gate_spec(k):
        return pl.BlockSpec((tm, GATE_BLK), lambda i, k=k: (i, c0 + k))

    return pl.pallas_call(
        _mix_kernel,
        out_shape=jax.ShapeDtypeStruct((T, D_MODEL), F32),
        grid=(T // tm,),
        in_specs=[
            pl.BlockSpec((tm, D_MODEL), lambda i: (i, 0)),
            _const_spec((1, D_MODEL)), _const_spec((1, D_MODEL)),
            pl.BlockSpec((tm, HG_W), lambda i: (i, 0)),
            pl.BlockSpec((tm, SB_W), lambda i: (i, 0)),
            gate_spec(0), gate_spec(1), gate_spec(2), gate_spec(3),
            _const_spec((HG_W, D_MODEL)), _const_spec((SB_W, D_MODEL)), _const_spec((D_MODEL, D_MODEL)),
            _const_spec((1, D_MODEL)), _const_spec((1, D_MODEL)),
        ],
        out_specs=pl.BlockSpec((tm, D_MODEL), lambda i: (i, 0)),
        compiler_params=_cparams(("parallel",)),
        name="mix",
    )(x2, gin, bin_, oa, ob, proj, proj, proj, proj, wa, wb, wm, g1, b1)


def _xa_kv_kernel(m_ref, wk_ref, wv_ref, k_ref, v_ref):
    m = m_ref[0].astype(BF16)
    k_ref[0] = _dot(m, wk_ref[...]).astype(BF16)
    v_ref[0] = _dot(m, wv_ref[...]).astype(BF16)


def _xa_kv(mem, wk, wv):
    B, M, _ = mem.shape
    W = wk.shape[1]
    return pl.pallas_call(
        _xa_kv_kernel,
        out_shape=(jax.ShapeDtypeStruct((B, M, W), BF16), jax.ShapeDtypeStruct((B, M, W), BF16)),
        grid=(B,),
        in_specs=[pl.BlockSpec((1, M, D_MODEL), lambda b: (b, 0, 0)),
                  _const_spec(wk.shape), _const_spec(wv.shape)],
        out_specs=(pl.BlockSpec((1, M, W), lambda b: (b, 0, 0)),
                   pl.BlockSpec((1, M, W), lambda b: (b, 0, 0))),
        compiler_params=_cparams(("parallel",)),
        name="xa_kv",
    )(mem, wk, wv)


XA_SUBTILES = 4


def _xattn_kernel(h_ref, k_ref, v_ref, wq_ref, wo_ref, g_ref, b_ref, o_ref):
    nsub = XA_SUBTILES
    rs = h_ref.shape[0] // nsub
    rows = [slice(k * rs, (k + 1) * rs) for k in range(nsub)]
    heads = [slice(hd * XA_HEAD_DIM, (hd + 1) * XA_HEAD_DIM) for hd in range(XA_HEADS)]
    scale = XA_HEAD_DIM ** -0.5
    hs = [h_ref[r, :] for r in rows]
    qs = [_dot(h.astype(BF16), wq_ref[...]).astype(BF16) for h in hs]
    ss = [[_dot_nt(q[:, ls], k_ref[0, :, ls]) * scale for ls in heads] for q in qs]
    os_ = []
    for k in range(nsub):
        outs = []
        for hd, ls in enumerate(heads):
            s = ss[k][hd]
            p = jnp.exp(s - jnp.max(s, axis=-1, keepdims=True))
            p = p / jnp.sum(p, axis=-1, keepdims=True)
            outs.append(_dot(p.astype(BF16), v_ref[0, :, ls]).astype(BF16))
        os_.append(jnp.concatenate(outs, axis=-1))
    ys = [_dot(o, wo_ref[...]) for o in os_]
    for k in range(nsub):
        o_ref[rows[k], :] = _layer_norm(DN_ALPHA * hs[k] + ys[k], g_ref[...], b_ref[...])


def _xattn(h1, kx, vx, wq, wo, g2, b2, B, S, tm=1024):
    T = B * S
    M = kx.shape[1]
    W = kx.shape[2]
    nsb = S // tm
    return pl.pallas_call(
        _xattn_kernel,
        out_shape=jax.ShapeDtypeStruct((T, D_MODEL), F32),
        grid=(B, nsb),
        in_specs=[
            pl.BlockSpec((tm, D_MODEL), lambda b, s: (b * nsb + s, 0)),
            pl.BlockSpec((1, M, W), lambda b, s: (b, 0, 0)),
            pl.BlockSpec((1, M, W), lambda b, s: (b, 0, 0)),
            _const_spec(wq.shape), _const_spec(wo.shape),
            _const_spec((1, D_MODEL)), _const_spec((1, D_MODEL)),
        ],
        out_specs=pl.BlockSpec((tm, D_MODEL), lambda b, s: (b * nsb + s, 0)),
        compiler_params=_cparams(("parallel", "arbitrary")),
        name="xattn",
    )(h1, kx, vx, wq, wo, g2, b2)


MOE_TM = 1024
MOE_M1 = 160
MOE_M2 = 64
MOE_EPS = 4
SEG_ALIGN = 16
MOE_QCH = 512
MOE_OUT_SUBTILES = 4


def _moe_slots(tm):
    return -(-(2 * tm + N_EXPERTS * SEG_ALIGN + MOE_M1 + MOE_M2) // MOE_QCH) * MOE_QCH


def _seg_pad(cnt):
    return jnp.ceil(cnt * (1.0 / SEG_ALIGN)) * SEG_ALIGN
SUBLANES = 8
GROUP_ROWS = 16


def _hdot(a, b):
    return jnp.dot(a, b, preferred_element_type=F32, precision=lax.Precision.HIGHEST)


def _split2(a):
    hi = a.astype(BF16)
    return hi, (a - hi.astype(F32)).astype(BF16)


def _first_argmax0(vals, row, height):
    m = jnp.max(vals, axis=0, keepdims=True)
    idx = jnp.min(jnp.where(vals == m, row, height), axis=0, keepdims=True)
    return m, idx


def _select_rows(shape, rows):
    r = lax.broadcasted_iota(jnp.int32, shape, 0)
    out = jnp.zeros(shape, F32)
    for k, v in enumerate(rows):
        out = jnp.where(r == k, v, out)
    return out


def _route_kernel(h_ref, wt_ref, bg_ref, be_ref, rinfo_ref, cinfo_ref, meta_ref):
    tm = h_ref.shape[0]
    h = h_ref[...]
    neg = jnp.float32(-jnp.inf)
    rg = lax.broadcasted_iota(jnp.int32, (GROUP_ROWS, tm), 0)
    re = lax.broadcasted_iota(jnp.int32, (N_EXPERTS, tm), 0)
    nw = GROUP_ROWS + N_EXPERTS
    hh, hl = _split2(h)
    wh, wl = _split2(wt_ref[...])
    lg = _dot_nt(jnp.concatenate([wh, wl], axis=0), hh)
    lg = lg[:nw] + (lg[nw:] + _dot_nt(wh, hl))
    gl = jnp.where(rg < N_GROUPS, lg[:GROUP_ROWS] + bg_ref[...], neg)
    el = lg[GROUP_ROWS:] + be_ref[...]
    gm, g_idx = _first_argmax0(gl, rg, GROUP_ROWS)
    g_top = 1.0 / jnp.sum(jnp.exp(gl - gm), axis=0, keepdims=True)
    ml = jnp.where(re // EXPERTS_PER_GROUP == g_idx, el, neg)
    m1, i1 = _first_argmax0(ml, re, N_EXPERTS)
    ml2 = jnp.where(re == i1, neg, ml)
    m2, i2 = _first_argmax0(ml2, re, N_EXPERTS)
    e2 = jnp.exp(m2 - m1)
    gate0 = g_top / (1.0 + e2)
    gate1 = g_top * e2 / (1.0 + e2)

    oh0 = jnp.where(re == i1, 1.0, 0.0)
    oh1 = jnp.where(re == i2, 1.0, 0.0)
    cnt0 = jnp.sum(oh0, axis=1, keepdims=True)
    cnt1 = jnp.sum(oh1, axis=1, keepdims=True)
    er = lax.broadcasted_iota(jnp.int32, (N_EXPERTS, N_EXPERTS), 0)
    ec = lax.broadcasted_iota(jnp.int32, (N_EXPERTS, N_EXPERTS), 1)
    seg = _hdot(jnp.where(ec < er, 1.0, 0.0),
                jnp.broadcast_to(_seg_pad(cnt0 + cnt1), (N_EXPERTS, LANES)))[:, 0:1]
    tr = lax.broadcasted_iota(jnp.int32, (tm, tm), 0)
    tc = lax.broadcasted_iota(jnp.int32, (tm, tm), 1)
    earlier = jnp.where(tr < tc, 1.0, 0.0).astype(BF16)
    cum = _dot(jnp.concatenate([oh0, oh1], axis=0).astype(BF16), earlier)
    slot0 = jnp.sum(oh0 * (seg + cum[:N_EXPERTS]), axis=0, keepdims=True)
    slot1 = jnp.sum(oh1 * (seg + cnt0 + cum[N_EXPERTS:]), axis=0, keepdims=True)

    info = (slot0, slot1, gate0, gate1)
    rinfo_ref[0] = _select_rows((SUBLANES, tm), info)
    cinfo_ref[...] = _select_rows((LANES, tm), info).T

    ohs = jnp.concatenate([oh0 + oh1, jnp.zeros((LANES - N_EXPERTS, tm), F32)], axis=0).astype(BF16)
    cnt_row = _dot_nt(jnp.ones((SUBLANES, tm), BF16), ohs)
    lr = lax.broadcasted_iota(jnp.int32, (LANES, LANES), 0)
    lc = lax.broadcasted_iota(jnp.int32, (LANES, LANES), 1)
    seg_row = _hdot(_seg_pad(cnt_row), jnp.where(lr < lc, 1.0, 0.0))
    r8 = lax.broadcasted_iota(jnp.int32, (SUBLANES, LANES), 0)
    meta_ref[0] = jnp.where(r8 == 0, seg_row, jnp.where(r8 == 1, cnt_row, 0.0)).astype(jnp.int32)


def _route(h2, wt, bg, be, tm=MOE_TM):
    T = h2.shape[0]
    nt = T // tm
    return pl.pallas_call(
        _route_kernel,
        out_shape=(jax.ShapeDtypeStruct((nt, SUBLANES, tm), F32),
                   jax.ShapeDtypeStruct((T, LANES), F32),
                   jax.ShapeDtypeStruct((nt, SUBLANES, LANES), jnp.int32)),
        grid=(nt,),
        in_specs=[pl.BlockSpec((tm, D_MODEL), lambda i: (i, 0)),
                  _const_spec(wt.shape), _const_spec(bg.shape), _const_spec(be.shape)],
        out_specs=(pl.BlockSpec((1, SUBLANES, tm), lambda i: (i, 0, 0)),
                   pl.BlockSpec((tm, LANES), lambda i: (i, 0)),
                   pl.BlockSpec((1, SUBLANES, LANES), lambda i: (i, 0, 0))),
        compiler_params=_cparams(("parallel",)),
        name="moe_route",
    )(h2, wt, bg, be)


def _moe_kernel(meta_ref, h_ref, rinfo_ref, cinfo_ref, w1_ref, w3_ref, w2_ref, g_ref, b_ref,
                o_ref, hb_ref, ys_ref):
    t = pl.program_id(0)
    eg = pl.program_id(1)
    tm = h_ref.shape[0]

    @pl.when(eg == 0)
    def _():
        hb_ref[...] = h_ref[...].astype(BF16)
        ys_ref[...] = jnp.zeros_like(ys_ref)

    def chunks(js, r0s, m):
        r0s = [pl.multiple_of(r0, SEG_ALIGN) for r0 in r0s]
        info = rinfo_ref[0]
        slot0 = info[0:1].astype(jnp.int32)
        slot1 = info[1:2].astype(jnp.int32)
        xs = []
        for r0 in r0s:
            s = lax.broadcasted_iota(jnp.int32, (m, tm), 0) + r0
            pick = jnp.where((s == slot0) | (s == slot1), 1.0, 0.0).astype(BF16)
            xs.append(_dot(pick, hb_ref[...]).astype(BF16))
        ab = [(_dot(x, w1_ref[j]), _dot(x, w3_ref[j])) for j, x in zip(js, xs)]
        hids = [(a * jax.nn.sigmoid(a) * b).astype(BF16) for a, b in ab]
        ys = [_dot(hid, w2_ref[j]).astype(BF16) for j, hid in zip(js, hids)]
        for r0, y in zip(r0s, ys):
            ys_ref[pl.ds(r0, m), :] = y

    base = eg * MOE_EPS
    starts = [meta_ref[(2 * t) * N_EXPERTS + base + j] for j in range(MOE_EPS)]
    cnts = [meta_ref[(2 * t + 1) * N_EXPERTS + base + j] for j in range(MOE_EPS)]

    for j in range(MOE_EPS):
        @pl.when(cnts[j] > MOE_M1)
        def _(j=j):
            def extra(k, c):
                chunks([j], [starts[j] + MOE_M1 + k * MOE_M2], MOE_M2)
                return c

            lax.fori_loop(0, (cnts[j] - MOE_M1 + MOE_M2 - 1) // MOE_M2, extra, 0)

    chunks(list(range(MOE_EPS)), starts, MOE_M1)

    @pl.when(eg == N_EXPERTS // MOE_EPS - 1)
    def _():
        rs = tm // MOE_OUT_SUBTILES
        for k in range(MOE_OUT_SUBTILES):
            r = slice(k * rs, (k + 1) * rs)
            ci = cinfo_ref[r, :]
            slot0 = ci[:, 0:1].astype(jnp.int32)
            slot1 = ci[:, 1:2].astype(jnp.int32)
            y = None
            for c in range(ys_ref.shape[0] // MOE_QCH):
                sc = lax.broadcasted_iota(jnp.int32, (rs, MOE_QCH), 1) + c * MOE_QCH
                q = (jnp.where(sc == slot0, ci[:, 2:3], 0.0)
                     + jnp.where(sc == slot1, ci[:, 3:4], 0.0)).astype(BF16)
                part = _dot(q, ys_ref[c * MOE_QCH:(c + 1) * MOE_QCH, :])
                y = part if y is None else y + part
            o_ref[r, :] = _layer_norm(DN_ALPHA * h_ref[r, :] + y, g_ref[...], b_ref[...])


def _moe(h2, rinfo, cinfo, meta, w1, w3, w2, g3, b3, tm=MOE_TM):
    T = h2.shape[0]
    return pl.pallas_call(
        _moe_kernel,
        out_shape=jax.ShapeDtypeStruct((T, D_MODEL), F32),
        grid_spec=pltpu.PrefetchScalarGridSpec(
            num_scalar_prefetch=1,
            grid=(T // tm, N_EXPERTS // MOE_EPS),
            in_specs=[
                pl.BlockSpec((tm, D_MODEL), lambda i, e, m: (i, 0)),
                pl.BlockSpec((1, SUBLANES, tm), lambda i, e, m: (i, 0, 0)),
                pl.BlockSpec((tm, LANES), lambda i, e, m: (i, 0)),
                pl.BlockSpec((MOE_EPS, D_MODEL, EXPERT_FF), lambda i, e, m: (e, 0, 0)),
                pl.BlockSpec((MOE_EPS, D_MODEL, EXPERT_FF), lambda i, e, m: (e, 0, 0)),
                pl.BlockSpec((MOE_EPS, EXPERT_FF, D_MODEL), lambda i, e, m: (e, 0, 0)),
                pl.BlockSpec((1, D_MODEL), lambda i, e, m: (0, 0)),
                pl.BlockSpec((1, D_MODEL), lambda i, e, m: (0, 0)),
            ],
            out_specs=pl.BlockSpec((tm, D_MODEL), lambda i, e, m: (i, 0)),
            scratch_shapes=[pltpu.VMEM((tm, D_MODEL), BF16), pltpu.VMEM((_moe_slots(tm), D_MODEL), BF16)],
        ),
        compiler_params=_cparams(("parallel", "arbitrary")),
        name="moe",
    )(meta, h2, rinfo, cinfo, w1, w3, w2, g3, b3)


def _moe_layer(h2, wg, bg, we, be, w1, w3, w2, g3, b3, tm=MOE_TM):
    pad = GROUP_ROWS - N_GROUPS
    wt = jnp.concatenate([jnp.pad(wg.astype(F32).T, ((0, pad), (0, 0))), we.astype(F32).T], axis=0)
    bgc = jnp.pad(bg.astype(F32).reshape(-1, 1), ((0, pad), (0, 0)))
    rinfo, cinfo, meta = _route(h2, wt, bgc, be.astype(F32).reshape(-1, 1), tm=tm)
    meta = meta[:, :2, :N_EXPERTS].reshape(-1)
    return _moe(h2, rinfo, cinfo, meta, w1, w3, w2, g3, b3, tm=tm)


def kernel(x, mem, ln_in_g, ln_in_b, w_in, hg_lb_logits, hg_norm_g, w_branch_a, w_branch_b, w_mix_out,
           ln1_g, ln1_b, xa_wq, xa_wk, xa_wv, xa_wo, ln2_g, ln2_b, router_wg, router_bg, router_we,
           router_be, moe_w1, moe_w3, moe_w2, ln3_g, ln3_b):
    B, S, D = x.shape
    T = B * S
    row = lambda a: a.reshape(1, -1).astype(F32)
    x2 = x.reshape(T, D)
    gin, bin_ = row(ln_in_g), row(ln_in_b)
    assert w_in.shape[0] == DEPTH
    l = 0
    proj = _in_proj(x2, gin, bin_, w_in[l].astype(BF16))
    oa = _hgrn2(proj, hg_lb_logits.astype(F32), row(hg_norm_g[l]), B, S, l)
    ob = _sb_attn(proj, B, S)
    h1 = _mix(x2, gin, bin_, oa, ob, proj, w_branch_a[l].astype(BF16), w_branch_b[l].astype(BF16),
              w_mix_out[l].astype(BF16), row(ln1_g[l]), row(ln1_b[l]))
    kx, vx = _xa_kv(mem, xa_wk[l].astype(BF16), xa_wv[l].astype(BF16))
    h2 = _xattn(h1, kx, vx, xa_wq[l].astype(BF16), xa_wo[l].astype(BF16), row(ln2_g[l]), row(ln2_b[l]), B, S)
    out = _moe_layer(h2, router_wg[l], router_bg[l], router_we[l], router_be[l],
                     moe_w1[l].astype(BF16), moe_w3[l].astype(BF16), moe_w2[l].astype(BF16),
                     row(ln3_g[l]), row(ln3_b[l]))
    return out.reshape(B, S, D)
```

```python
import functools

import jax
import jax.numpy as jnp
from jax import lax
from jax.experimental import pallas as pl
from jax.experimental.pallas import tpu as pltpu

F32 = jnp.float32
BF16 = jnp.bfloat16

D_MODEL = 1024
HG_HEADS = 4
HG_DIM = 128
HG_W = HG_HEADS * HG_DIM
HG_CHUNK = 64
HG_LOCAL_CHUNKS = 4
SB_HEADS = 8
SB_HEAD_DIM = 64
SB_W = SB_HEADS * SB_HEAD_DIM
SB_BLOCK = 128
XA_HEADS = 4
XA_HEAD_DIM = D_MODEL // XA_HEADS
N_GROUPS = 4
EXPERTS_PER_GROUP = 4
N_EXPERTS = N_GROUPS * EXPERTS_PER_GROUP
EXPERT_FF = 512
N_IN = HG_W * 4 + SB_W * 3 + D_MODEL * 2
COL_HG = 0
COL_SB = 4 * HG_W
COL_GATE = COL_SB + 3 * SB_W
LN_EPS = 1e-5
RMS_EPS = 1e-6
DEPTH = 1
DN_ALPHA = (2 * DEPTH) ** 0.25

LANES = 128
VMEM_LIMIT = 56 * 1024 * 1024


def _cparams(sem):
    return pltpu.CompilerParams(dimension_semantics=sem, vmem_limit_bytes=VMEM_LIMIT)


def _layer_norm(x, g, b):
    mu = jnp.mean(x, axis=-1, keepdims=True)
    xc = x - mu
    var = jnp.mean(xc * xc, axis=-1, keepdims=True)
    return xc * lax.rsqrt(var + LN_EPS) * g + b


def _dot(a, b):
    return jnp.dot(a, b, preferred_element_type=F32)


def _dot_nt(a, b):
    return lax.dot_general(a, b, (((1,), (1,)), ((), ())), preferred_element_type=F32)


def _dot_tn(a, b):
    return lax.dot_general(a, b, (((0,), (0,)), ((), ())), preferred_element_type=F32)


def _split3(a):
    hi = a.astype(BF16)
    r1 = a - hi.astype(F32)
    mid = r1.astype(BF16)
    lo = (r1 - mid.astype(F32)).astype(BF16)
    return hi, mid, lo


def _const_spec(shape):
    nd = len(shape)
    return pl.BlockSpec(shape, lambda *_: (0,) * nd)


def _in_proj_kernel(x_ref, g_ref, b_ref, w_ref, o_ref, h0_ref, *, col_chunk):
    h0 = _layer_norm(x_ref[...], g_ref[...], b_ref[...])
    h0_ref[...] = h0
    h = h0.astype(BF16)
    for c in range(N_IN // col_chunk):
        sl = slice(c * col_chunk, (c + 1) * col_chunk)
        o_ref[:, sl] = _dot(h, w_ref[:, sl]).astype(BF16)


def _in_proj(x2, g, b, w_bf, tm=512, col_chunk=512):
    T = x2.shape[0]
    return pl.pallas_call(
        functools.partial(_in_proj_kernel, col_chunk=col_chunk),
        out_shape=(jax.ShapeDtypeStruct((T, N_IN), BF16), jax.ShapeDtypeStruct((T, D_MODEL), F32)),
        grid=(T // tm,),
        in_specs=[
            pl.BlockSpec((tm, D_MODEL), lambda i: (i, 0)),
            _const_spec((1, D_MODEL)),
            _const_spec((1, D_MODEL)),
            _const_spec((D_MODEL, N_IN)),
        ],
        out_specs=(pl.BlockSpec((tm, N_IN), lambda i: (i, 0)), pl.BlockSpec((tm, D_MODEL), lambda i: (i, 0))),
        compiler_params=_cparams(("parallel",)),
        name="in_proj",
    )(x2, g, b, w_bf)


def _hgrn2_kernel(lbl_ref, ng_ref, q_ref, f_ref, i_ref, g_ref, o_ref,
                  state_ref, intra_ref, qs_ref, dl_ref, u_ref, *, layer, tb):
    C = HG_CHUNK
    H2 = C // 2

    @pl.when(pl.program_id(1) == 0)
    def _():
        state_ref[...] = jnp.zeros_like(state_ref)

    lbl = lbl_ref[...]
    ex = jnp.exp(lbl - jnp.max(lbl, axis=0, keepdims=True))
    lb = jnp.sum(ex[: layer + 1], axis=0, keepdims=True) / jnp.sum(ex, axis=0, keepdims=True)

    R = HG_LOCAL_CHUNKS * C
    row = lax.broadcasted_iota(jnp.int32, (R, R), 0)
    col = lax.broadcasted_iota(jnp.int32, (R, R), 1)
    same_chunk = (row // C) == (col // C)
    tri = (same_chunk & (col <= row)).astype(BF16)
    m_diag = ((row // H2) == (col // H2)) & (col <= row)
    m_cross = same_chunk & ((row % C) >= H2) & ((col % C) < H2)
    rrow = lax.broadcasted_iota(jnp.int32, (R, HG_W), 0)
    ng = ng_ref[...]
    heads = [slice(h * HG_DIM, (h + 1) * HG_DIM) for h in range(HG_HEADS)]

    def row_of_period(x, offset, period):
        out = jnp.broadcast_to(x[offset:offset + 1, :], x.shape)
        for k in range(1, R // period):
            out = jnp.where(rrow >= k * period, x[k * period + offset:k * period + offset + 1, :], out)
        return out

    def local(blk):
        rows = slice(blk * R, (blk + 1) * R)
        forget = lb + (1.0 - lb) * jax.nn.sigmoid(f_ref[rows, :].astype(F32))
        kk = 1.0 - forget
        hi, mid, lo = _split3(jnp.log(forget))
        b = _dot(tri, hi) + _dot(tri, mid) + _dot(tri, lo)
        q = q_ref[rows, :].astype(F32)
        d1 = b - row_of_period(b, H2 // 2 - 1, H2)
        q1 = (q * jnp.exp(d1)).astype(BF16)
        k1 = (kk * jnp.exp(-d1)).astype(BF16)
        e3 = jnp.exp(-jnp.abs(b - row_of_period(b, H2 - 1, C)))
        q3 = (q * e3).astype(BF16)
        k3 = (kk * e3).astype(BF16)
        kd = (kk * jnp.exp(row_of_period(b, C - 1, C) - b)).astype(BF16)
        qs_ref[rows, :] = (q * jnp.exp(b)).astype(BF16)
        for k in range(HG_LOCAL_CHUNKS):
            dl_ref[blk * HG_LOCAL_CHUNKS + k] = jnp.exp(b[k * C + C - 1:(k + 1) * C, :])
        s1 = [_dot_nt(q1[:, ls], k1[:, ls]) for ls in heads]
        s3 = [_dot_nt(q3[:, ls], k3[:, ls]) for ls in heads]
        s = [jnp.where(m_diag, a1, jnp.where(m_cross, a3, 0.0)).astype(BF16) for a1, a3 in zip(s1, s3)]
        for h, ls in enumerate(heads):
            v = i_ref[rows, ls]
            intra_ref[rows, ls] = _dot(s[h], v)
            for k in range(HG_LOCAL_CHUNKS):
                cs = slice(k * C, (k + 1) * C)
                u_ref[blk * HG_LOCAL_CHUNKS + k, h] = _dot_tn(v[cs], kd[cs, ls])

    def carry_state(c, carry):
        rows = pl.ds(pl.multiple_of(c * C, C), C)
        dl = dl_ref[c]
        for h, ls in enumerate(heads):
            st = state_ref[h]
            o = intra_ref[rows, ls] + _dot_nt(qs_ref[rows, ls], st.astype(BF16))
            state_ref[h] = st * dl[:, ls] + u_ref[c, h]
            rms = lax.rsqrt(jnp.mean(o * o, axis=-1, keepdims=True) + RMS_EPS)
            gate = g_ref[rows, ls].astype(F32)
            o_ref[rows, ls] = (o * rms * ng * (gate * jax.nn.sigmoid(gate))).astype(o_ref.dtype)
        return carry

    for blk in range(tb // R):
        local(blk)
    lax.fori_loop(0, tb // C, carry_state, 0, unroll=True)


def _hgrn2(proj, lb_logits, norm_g, B, S, layer, tb=512):
    T = B * S
    nsb = S // tb

    def spec(k):
        return pl.BlockSpec((tb, HG_W), lambda b, s, k=k: (b * nsb + s, COL_HG // HG_W + k))

    return pl.pallas_call(
        functools.partial(_hgrn2_kernel, layer=layer, tb=tb),
        out_shape=jax.ShapeDtypeStruct((T, HG_W), BF16),
        grid=(B, nsb),
        in_specs=[
            _const_spec(lb_logits.shape),
            _const_spec((1, HG_DIM)),
            spec(0), spec(1), spec(2), spec(3),
        ],
        out_specs=pl.BlockSpec((tb, HG_W), lambda b, s: (b * nsb + s, 0)),
        scratch_shapes=[pltpu.VMEM((HG_HEADS, HG_DIM, HG_DIM), F32),
                        pltpu.VMEM((tb, HG_W), F32),
                        pltpu.VMEM((tb, HG_W), BF16),
                        pltpu.VMEM((tb // HG_CHUNK, 1, HG_W), F32),
                        pltpu.VMEM((tb // HG_CHUNK, HG_HEADS, HG_DIM, HG_DIM), F32)],
        compiler_params=_cparams(("parallel", "arbitrary")),
        name="hgrn2",
    )(lb_logits, norm_g, proj, proj, proj, proj)


LOG2E = 1.4426950408889634
SB_DEAD = -160.0


def _sb_kernel(q_ref, k_ref, v_ref, o_ref, acc_ref, run_ref, *, npairs, tq):
    i = pl.program_id(1)
    BLK = SB_BLOCK
    W2 = 2 * BLK
    ndiag = tq // BLK
    scale = SB_HEAD_DIM ** -0.5
    lane = lax.broadcasted_iota(jnp.int32, (BLK, LANES), 1)
    head0 = lane < SB_HEAD_DIM
    zrow = lax.broadcasted_iota(jnp.int32, (tq, W2), 0)
    zcol = lax.broadcasted_iota(jnp.int32, (tq, W2), 1) & (BLK - 1)
    wr = lax.broadcasted_iota(jnp.int32, (W2, W2), 0) & (BLK - 1)
    wc = lax.broadcasted_iota(jnp.int32, (W2, W2), 1)
    cw = ((wc >= BLK) | (wr > wc)).astype(BF16)
    qs = q_ref[...] * jnp.asarray(scale, q_ref.dtype)

    def masked_kv(ref, j, p):
        x = ref[pl.ds(pl.multiple_of(j * BLK, BLK), BLK), p * LANES:(p + 1) * LANES]
        zero = jnp.zeros_like(x)
        return jnp.concatenate([jnp.where(head0, x, zero), jnp.where(head0, zero, x)], axis=0)

    def step(js, ds):
        nb = len(js)
        rsl = [slice((ds[b] or 0) * BLK, tq) for b in range(nb)]
        zs = [[_dot_nt(qs[rsl[b], p * LANES:(p + 1) * LANES], masked_kv(k_ref, js[b], p)) * LOG2E
               for p in range(npairs)] for b in range(nb)]
        ws = []
        for b in range(nb):
            nr = tq - rsl[b].start
            before = None if ds[b] is None else ((zcol + ds[b] * BLK) < zrow)[rsl[b]]
            lbs, hls = [], []
            for p in range(npairs):
                z = zs[b][p]
                l2 = jnp.log2(1.0 + jnp.exp2(-jnp.abs(z)))
                lb = jnp.minimum(z, 0.0) - l2
                lk = lb - z
                if before is not None:
                    lk = jnp.where(before, lk, 0.0)
                hi = lk.astype(BF16)
                lo = (lk - hi.astype(F32)).astype(BF16)
                lbs.append(lb)
                hls.append(jnp.concatenate([hi[:, :BLK], lo[:, :BLK]], axis=1))
                hls.append(jnp.concatenate([hi[:, BLK:], lo[:, BLK:]], axis=1))
            cs = _dot(jnp.concatenate(hls, axis=0), cw)
            wb = []
            for p in range(npairs):
                c0 = cs[(2 * p) * nr:(2 * p + 1) * nr]
                c1 = cs[(2 * p + 1) * nr:(2 * p + 2) * nr]
                cum = jnp.concatenate([c0[:, :BLK], c1[:, :BLK]], axis=1)
                rsum = jnp.concatenate([c0[:, BLK:], c1[:, BLK:]], axis=1)
                run = run_ref[p, rsl[b], :]
                w = jnp.exp2(lbs[p] + (run + cum))
                if before is not None:
                    w = jnp.where(before, w, 0.0)
                run_ref[p, rsl[b], :] = run + rsum
                wb.append(w.astype(BF16))
            ws.append(wb)
        for p in range(npairs):
            for b in range(nb):
                acc_ref[p, rsl[b], :] += _dot(ws[b][p], masked_kv(v_ref, js[b], p))

    acc_ref[...] = jnp.zeros_like(acc_ref)
    run_ref[...] = jnp.zeros_like(run_ref)
    step([i * ndiag + d for d in reversed(range(ndiag))], list(reversed(range(ndiag))))

    def live():
        return jnp.max(run_ref[...]) > SB_DEAD

    def cond(c):
        return jnp.logical_and(c[0] < i, c[1])

    def body(c):
        base = (i - 1 - c[0]) * ndiag
        step([base + d for d in reversed(range(ndiag))], [None] * ndiag)
        return c[0] + 1, live()

    lax.while_loop(cond, body, (jnp.int32(0), live()))
    for p in range(npairs):
        o_ref[:, p * LANES:(p + 1) * LANES] = acc_ref[p].astype(o_ref.dtype)


def _sb_attn(proj, B, S, npairs=SB_W // LANES, tq=256):
    T = B * S
    nq = S // tq
    wblk = npairs * LANES
    ng = SB_W // wblk
    cq = COL_SB // wblk
    ck = (COL_SB + SB_W) // wblk
    cv = (COL_SB + 2 * SB_W) // wblk
    return pl.pallas_call(
        functools.partial(_sb_kernel, npairs=npairs, tq=tq),
        out_shape=jax.ShapeDtypeStruct((T, SB_W), BF16),
        grid=(B * ng, nq),
        in_specs=[
            pl.BlockSpec((tq, wblk), lambda g, i: ((g // ng) * nq + i, cq + g % ng)),
            pl.BlockSpec((S, wblk), lambda g, i: (g // ng, ck + g % ng)),
            pl.BlockSpec((S, wblk), lambda g, i: (g // ng, cv + g % ng)),
        ],
        out_specs=pl.BlockSpec((tq, wblk), lambda g, i: ((g // ng) * nq + i, g % ng)),
        scratch_shapes=[pltpu.VMEM((npairs, tq, LANES), F32),
                        pltpu.VMEM((npairs, tq, 2 * SB_BLOCK), F32)],
        compiler_params=_cparams(("parallel", "arbitrary")),
        name="sb_attn",
    )(proj, proj, proj)


GATE_BLK = 512


def _mix_kernel(h0_ref, oa_ref, ob_ref, ga0_ref, ga1_ref, gb0_ref, gb1_ref,
                wa_ref, wb_ref, wm_ref, g1_ref, b1_ref, o_ref):
    rs = h0_ref.shape[0] // MIX_SUBTILES
    rows = [slice(k * rs, (k + 1) * rs) for k in range(MIX_SUBTILES)]
    halves = [(slice(0, GATE_BLK), ga0_ref, gb0_ref), (slice(GATE_BLK, 2 * GATE_BLK), ga1_ref, gb1_ref)]
    yab = [[(_dot(oa_ref[r, :], wa_ref[:, cs]), _dot(ob_ref[r, :], wb_ref[:, cs])) for cs, _, _ in halves]
           for r in rows]
    ys = []
    for k, r in enumerate(rows):
        y = None
        for (cs, ga_ref, gb_ref), (ya, yb) in zip(halves, yab[k]):
            merged = (jax.nn.sigmoid(ga_ref[r, :].astype(F32)) * ya
                      + jax.nn.sigmoid(gb_ref[r, :].astype(F32)) * yb)
            part = _dot(merged.astype(BF16), wm_ref[cs, :])
            y = part if y is None else y + part
        ys.append(y)
    for k, r in enumerate(rows):
        o_ref[r, :] = _layer_norm(DN_ALPHA * h0_ref[r, :] + ys[k], g1_ref[...], b1_ref[...])


MIX_SUBTILES = 4


def _mix(h0, oa, ob, proj, wa, wb, wm, g1, b1, tm=1024):
    T = h0.shape[0]
    c0 = COL_GATE // GATE_BLK

    def gate_spec(k):
        return pl.BlockSpec((tm, GATE_BLK), lambda i, k=k: (i, c0 + k))

    return pl.pallas_call(
        _mix_kernel,
        out_shape=jax.ShapeDtypeStruct((T, D_MODEL), F32),
        grid=(T // tm,),
        in_specs=[
            pl.BlockSpec((tm, D_MODEL), lambda i: (i, 0)),
            pl.BlockSpec((tm, HG_W), lambda i: (i, 0)),
            pl.BlockSpec((tm, SB_W), lambda i: (i, 0)),
            gate_spec(0), gate_spec(1), gate_spec(2), gate_spec(3),
            _const_spec((HG_W, D_MODEL)), _const_spec((SB_W, D_MODEL)), _const_spec((D_MODEL, D_MODEL)),
            _const_spec((1, D_MODEL)), _const_spec((1, D_MODEL)),
        ],
        out_specs=pl.BlockSpec((tm, D_MODEL), lambda i: (i, 0)),
        compiler_params=_cparams(("parallel",)),
        name="mix",
    )(h0, oa, ob, proj, proj, proj, proj, wa, wb, wm, g1, b1)


def _xa_kv_kernel(m_ref, wk_ref, wv_ref, k_ref, v_ref):
    m = m_ref[0].astype(BF16)
    k_ref[0] = _dot(m, wk_ref[...]).astype(BF16)
    v_ref[0] = _dot(m, wv_ref[...]).astype(BF16)


def _xa_kv(mem, wk, wv):
    B, M, _ = mem.shape
    W = wk.shape[1]
    return pl.pallas_call(
        _xa_kv_kernel,
        out_shape=(jax.ShapeDtypeStruct((B, M, W), BF16), jax.ShapeDtypeStruct((B, M, W), BF16)),
        grid=(B,),
        in_specs=[pl.BlockSpec((1, M, D_MODEL), lambda b: (b, 0, 0)),
                  _const_spec(wk.shape), _const_spec(wv.shape)],
        out_specs=(pl.BlockSpec((1, M, W), lambda b: (b, 0, 0)),
                   pl.BlockSpec((1, M, W), lambda b: (b, 0, 0))),
        compiler_params=_cparams(("parallel",)),
        name="xa_kv",
    )(mem, wk, wv)


XA_SUBTILES = 4


def _xattn_kernel(h_ref, k_ref, v_ref, wq_ref, wo_ref, g_ref, b_ref, o_ref):
    nsub = XA_SUBTILES
    rs = h_ref.shape[0] // nsub
    rows = [slice(k * rs, (k + 1) * rs) for k in range(nsub)]
    heads = [slice(hd * XA_HEAD_DIM, (hd + 1) * XA_HEAD_DIM) for hd in range(XA_HEADS)]
    scale = XA_HEAD_DIM ** -0.5
    hs = [h_ref[r, :] for r in rows]
    qs = [_dot(h.astype(BF16), wq_ref[...]).astype(BF16) for h in hs]
    ss = [[_dot_nt(q[:, ls], k_ref[0, :, ls]) * scale for ls in heads] for q in qs]
    os_ = []
    for k in range(nsub):
        outs = []
        for hd, ls in enumerate(heads):
            s = ss[k][hd]
            p = jnp.exp(s - jnp.max(s, axis=-1, keepdims=True))
            p = p / jnp.sum(p, axis=-1, keepdims=True)
            outs.append(_dot(p.astype(BF16), v_ref[0, :, ls]).astype(BF16))
        os_.append(jnp.concatenate(outs, axis=-1))
    ys = [_dot(o, wo_ref[...]) for o in os_]
    for k in range(nsub):
        o_ref[rows[k], :] = _layer_norm(DN_ALPHA * hs[k] + ys[k], g_ref[...], b_ref[...])


def _xattn(h1, kx, vx, wq, wo, g2, b2, B, S, tm=1024):
    T = B * S
    M = kx.shape[1]
    W = kx.shape[2]
    nsb = S // tm
    return pl.pallas_call(
        _xattn_kernel,
        out_shape=jax.ShapeDtypeStruct((T, D_MODEL), F32),
        grid=(B, nsb),
        in_specs=[
            pl.BlockSpec((tm, D_MODEL), lambda b, s: (b * nsb + s, 0)),
            pl.BlockSpec((1, M, W), lambda b, s: (b, 0, 0)),
            pl.BlockSpec((1, M, W), lambda b, s: (b, 0, 0)),
            _const_spec(wq.shape), _const_spec(wo.shape),
            _const_spec((1, D_MODEL)), _const_spec((1, D_MODEL)),
        ],
        out_specs=pl.BlockSpec((tm, D_MODEL), lambda b, s: (b * nsb + s, 0)),
        compiler_params=_cparams(("parallel", "arbitrary")),
        name="xattn",
    )(h1, kx, vx, wq, wo, g2, b2)


MOE_TM = 1024
MOE_M1 = 160
MOE_M2 = 64
MOE_EPS = 4
SEG_ALIGN = 16
MOE_QCH = 512
MOE_OUT_SUBTILES = 4


def _moe_slots(tm):
    return -(-(2 * tm + N_EXPERTS * SEG_ALIGN + MOE_M1 + MOE_M2) // MOE_QCH) * MOE_QCH


def _seg_pad(cnt):
    return jnp.ceil(cnt * (1.0 / SEG_ALIGN)) * SEG_ALIGN
SUBLANES = 8
GROUP_ROWS = 16


def _hdot(a, b):
    return jnp.dot(a, b, preferred_element_type=F32, precision=lax.Precision.HIGHEST)


def _split2(a):
    hi = a.astype(BF16)
    return hi, (a - hi.astype(F32)).astype(BF16)


def _first_argmax0(vals, row, height):
    m = jnp.max(vals, axis=0, keepdims=True)
    idx = jnp.min(jnp.where(vals == m, row, height), axis=0, keepdims=True)
    return m, idx


def _select_rows(shape, rows):
    r = lax.broadcasted_iota(jnp.int32, shape, 0)
    out = jnp.zeros(shape, F32)
    for k, v in enumerate(rows):
        out = jnp.where(r == k, v, out)
    return out


def _route_kernel(h_ref, wt_ref, bg_ref, be_ref, rinfo_ref, cinfo_ref, meta_ref):
    tm = h_ref.shape[0]
    h = h_ref[...]
    neg = jnp.float32(-jnp.inf)
    rg = lax.broadcasted_iota(jnp.int32, (GROUP_ROWS, tm), 0)
    re = lax.broadcasted_iota(jnp.int32, (N_EXPERTS, tm), 0)
    nw = GROUP_ROWS + N_EXPERTS
    hh, hl = _split2(h)
    wh, wl = _split2(wt_ref[...])
    lg = _dot_nt(jnp.concatenate([wh, wl], axis=0), hh)
    lg = lg[:nw] + (lg[nw:] + _dot_nt(wh, hl))
    gl = jnp.where(rg < N_GROUPS, lg[:GROUP_ROWS] + bg_ref[...], neg)
    el = lg[GROUP_ROWS:] + be_ref[...]
    gm, g_idx = _first_argmax0(gl, rg, GROUP_ROWS)
    g_top = 1.0 / jnp.sum(jnp.exp(gl - gm), axis=0, keepdims=True)
    ml = jnp.where(re // EXPERTS_PER_GROUP == g_idx, el, neg)
    m1, i1 = _first_argmax0(ml, re, N_EXPERTS)
    ml2 = jnp.where(re == i1, neg, ml)
    m2, i2 = _first_argmax0(ml2, re, N_EXPERTS)
    e2 = jnp.exp(m2 - m1)
    gate0 = g_top / (1.0 + e2)
    gate1 = g_top * e2 / (1.0 + e2)

    oh0 = jnp.where(re == i1, 1.0, 0.0)
    oh1 = jnp.where(re == i2, 1.0, 0.0)
    cnt0 = jnp.sum(oh0, axis=1, keepdims=True)
    cnt1 = jnp.sum(oh1, axis=1, keepdims=True)
    er = lax.broadcasted_iota(jnp.int32, (N_EXPERTS, N_EXPERTS), 0)
    ec = lax.broadcasted_iota(jnp.int32, (N_EXPERTS, N_EXPERTS), 1)
    seg = _hdot(jnp.where(ec < er, 1.0, 0.0),
                jnp.broadcast_to(_seg_pad(cnt0 + cnt1), (N_EXPERTS, LANES)))[:, 0:1]
    tr = lax.broadcasted_iota(jnp.int32, (tm, tm), 0)
    tc = lax.broadcasted_iota(jnp.int32, (tm, tm), 1)
    earlier = jnp.where(tr < tc, 1.0, 0.0).astype(BF16)
    cum = _dot(jnp.concatenate([oh0, oh1], axis=0).astype(BF16), earlier)
    slot0 = jnp.sum(oh0 * (seg + cum[:N_EXPERTS]), axis=0, keepdims=True)
    slot1 = jnp.sum(oh1 * (seg + cnt0 + cum[N_EXPERTS:]), axis=0, keepdims=True)

    info = (slot0, slot1, gate0, gate1)
    rinfo_ref[0] = _select_rows((SUBLANES, tm), info)
    cinfo_ref[...] = _select_rows((LANES, tm), info).T

    ohs = jnp.concatenate([oh0 + oh1, jnp.zeros((LANES - N_EXPERTS, tm), F32)], axis=0).astype(BF16)
    cnt_row = _dot_nt(jnp.ones((SUBLANES, tm), BF16), ohs)
    lr = lax.broadcasted_iota(jnp.int32, (LANES, LANES), 0)
    lc = lax.broadcasted_iota(jnp.int32, (LANES, LANES), 1)
    seg_row = _hdot(_seg_pad(cnt_row), jnp.where(lr < lc, 1.0, 0.0))
    r8 = lax.broadcasted_iota(jnp.int32, (SUBLANES, LANES), 0)
    meta_ref[0] = jnp.where(r8 == 0, seg_row, jnp.where(r8 == 1, cnt_row, 0.0)).astype(jnp.int32)


def _route(h2, wt, bg, be, tm=MOE_TM):
    T = h2.shape[0]
    nt = T // tm
    return pl.pallas_call(
        _route_kernel,
        out_shape=(jax.ShapeDtypeStruct((nt, SUBLANES, tm), F32),
                   jax.ShapeDtypeStruct((T, LANES), F32),
                   jax.ShapeDtypeStruct((nt, SUBLANES, LANES), jnp.int32)),
        grid=(nt,),
        in_specs=[pl.BlockSpec((tm, D_MODEL), lambda i: (i, 0)),
                  _const_spec(wt.shape), _const_spec(bg.shape), _const_spec(be.shape)],
        out_specs=(pl.BlockSpec((1, SUBLANES, tm), lambda i: (i, 0, 0)),
                   pl.BlockSpec((tm, LANES), lambda i: (i, 0)),
                   pl.BlockSpec((1, SUBLANES, LANES), lambda i: (i, 0, 0))),
        compiler_params=_cparams(("parallel",)),
        name="moe_route",
    )(h2, wt, bg, be)


def _moe_kernel(meta_ref, h_ref, rinfo_ref, cinfo_ref, w1_ref, w3_ref, w2_ref, g_ref, b_ref,
                o_ref, hb_ref, ys_ref):
    t = pl.program_id(0)
    eg = pl.program_id(1)
    tm = h_ref.shape[0]

    @pl.when(eg == 0)
    def _():
        hb_ref[...] = h_ref[...].astype(BF16)
        ys_ref[...] = jnp.zeros_like(ys_ref)

    def chunks(js, r0s, m):
        r0s = [pl.multiple_of(r0, SEG_ALIGN) for r0 in r0s]
        info = rinfo_ref[0]
        slot0 = info[0:1].astype(jnp.int32)
        slot1 = info[1:2].astype(jnp.int32)
        xs = []
        for r0 in r0s:
            s = lax.broadcasted_iota(jnp.int32, (m, tm), 0) + r0
            pick = jnp.where((s == slot0) | (s == slot1), 1.0, 0.0).astype(BF16)
            xs.append(_dot(pick, hb_ref[...]).astype(BF16))
        ab = [(_dot(x, w1_ref[j]), _dot(x, w3_ref[j])) for j, x in zip(js, xs)]
        hids = [(a * jax.nn.sigmoid(a) * b).astype(BF16) for a, b in ab]
        ys = [_dot(hid, w2_ref[j]).astype(BF16) for j, hid in zip(js, hids)]
        for r0, y in zip(r0s, ys):
            ys_ref[pl.ds(r0, m), :] = y

    base = eg * MOE_EPS
    starts = [meta_ref[(2 * t) * N_EXPERTS + base + j] for j in range(MOE_EPS)]
    cnts = [meta_ref[(2 * t + 1) * N_EXPERTS + base + j] for j in range(MOE_EPS)]

    for j in range(MOE_EPS):
        @pl.when(cnts[j] > MOE_M1)
        def _(j=j):
            def extra(k, c):
                chunks([j], [starts[j] + MOE_M1 + k * MOE_M2], MOE_M2)
                return c

            lax.fori_loop(0, (cnts[j] - MOE_M1 + MOE_M2 - 1) // MOE_M2, extra, 0)

    chunks(list(range(MOE_EPS)), starts, MOE_M1)

    @pl.when(eg == N_EXPERTS // MOE_EPS - 1)
    def _():
        rs = tm // MOE_OUT_SUBTILES
        for k in range(MOE_OUT_SUBTILES):
            r = slice(k * rs, (k + 1) * rs)
            ci = cinfo_ref[r, :]
            slot0 = ci[:, 0:1].astype(jnp.int32)
            slot1 = ci[:, 1:2].astype(jnp.int32)
            y = None
            for c in range(ys_ref.shape[0] // MOE_QCH):
                sc = lax.broadcasted_iota(jnp.int32, (rs, MOE_QCH), 1) + c * MOE_QCH
                q = (jnp.where(sc == slot0, ci[:, 2:3], 0.0)
                     + jnp.where(sc == slot1, ci[:, 3:4], 0.0)).astype(BF16)
                part = _dot(q, ys_ref[c * MOE_QCH:(c + 1) * MOE_QCH, :])
                y = part if y is None else y + part
            o_ref[r, :] = _layer_norm(DN_ALPHA * h_ref[r, :] + y, g_ref[...], b_ref[...])


def _moe(h2, rinfo, cinfo, meta, w1, w3, w2, g3, b3, tm=MOE_TM):
    T = h2.shape[0]
    return pl.pallas_call(
        _moe_kernel,
        out_shape=jax.ShapeDtypeStruct((T, D_MODEL), F32),
        grid_spec=pltpu.PrefetchScalarGridSpec(
            num_scalar_prefetch=1,
            grid=(T // tm, N_EXPERTS // MOE_EPS),
            in_specs=[
                pl.BlockSpec((tm, D_MODEL), lambda i, e, m: (i, 0)),
                pl.BlockSpec((1, SUBLANES, tm), lambda i, e, m: (i, 0, 0)),
                pl.BlockSpec((tm, LANES), lambda i, e, m: (i, 0)),
                pl.BlockSpec((MOE_EPS, D_MODEL, EXPERT_FF), lambda i, e, m: (e, 0, 0)),
                pl.BlockSpec((MOE_EPS, D_MODEL, EXPERT_FF), lambda i, e, m: (e, 0, 0)),
                pl.BlockSpec((MOE_EPS, EXPERT_FF, D_MODEL), lambda i, e, m: (e, 0, 0)),
                pl.BlockSpec((1, D_MODEL), lambda i, e, m: (0, 0)),
                pl.BlockSpec((1, D_MODEL), lambda i, e, m: (0, 0)),
            ],
            out_specs=pl.BlockSpec((tm, D_MODEL), lambda i, e, m: (i, 0)),
            scratch_shapes=[pltpu.VMEM((tm, D_MODEL), BF16), pltpu.VMEM((_moe_slots(tm), D_MODEL), BF16)],
        ),
        compiler_params=_cparams(("parallel", "arbitrary")),
        name="moe",
    )(meta, h2, rinfo, cinfo, w1, w3, w2, g3, b3)


def _moe_layer(h2, wg, bg, we, be, w1, w3, w2, g3, b3, tm=MOE_TM):
    pad = GROUP_ROWS - N_GROUPS
    wt = jnp.concatenate([jnp.pad(wg.astype(F32).T, ((0, pad), (0, 0))), we.astype(F32).T], axis=0)
    bgc = jnp.pad(bg.astype(F32).reshape(-1, 1), ((0, pad), (0, 0)))
    rinfo, cinfo, meta = _route(h2, wt, bgc, be.astype(F32).reshape(-1, 1), tm=tm)
    meta = meta[:, :2, :N_EXPERTS].reshape(-1)
    return _moe(h2, rinfo, cinfo, meta, w1, w3, w2, g3, b3, tm=tm)


def kernel(x, mem, ln_in_g, ln_in_b, w_in, hg_lb_logits, hg_norm_g, w_branch_a, w_branch_b, w_mix_out,
           ln1_g, ln1_b, xa_wq, xa_wk, xa_wv, xa_wo, ln2_g, ln2_b, router_wg, router_bg, router_we,
           router_be, moe_w1, moe_w3, moe_w2, ln3_g, ln3_b):
    B, S, D = x.shape
    T = B * S
    row = lambda a: a.reshape(1, -1).astype(F32)
    x2 = x.reshape(T, D)
    gin, bin_ = row(ln_in_g), row(ln_in_b)
    assert w_in.shape[0] == DEPTH
    l = 0
    proj, h0 = _in_proj(x2, gin, bin_, w_in[l].astype(BF16))
    oa = _hgrn2(proj, hg_lb_logits.astype(F32), row(hg_norm_g[l]), B, S, l)
    ob = _sb_attn(proj, B, S)
    h1 = _mix(h0, oa, ob, proj, w_branch_a[l].astype(BF16), w_branch_b[l].astype(BF16),
              w_mix_out[l].astype(BF16), row(ln1_g[l]), row(ln1_b[l]))
    kx, vx = _xa_kv(mem, xa_wk[l].astype(BF16), xa_wv[l].astype(BF16))
    h2 = _xattn(h1, kx, vx, xa_wq[l].astype(BF16), xa_wo[l].astype(BF16), row(ln2_g[l]), row(ln2_b[l]), B, S)
    out = _moe_layer(h2, router_wg[l], router_bg[l], router_we[l], router_be[l],
                     moe_w1[l].astype(BF16), moe_w3[l].astype(BF16), moe_w2[l].astype(BF16),
                     row(ln3_g[l]), row(ln3_b[l]))
    return out.reshape(B, S, D)
```

```python
import functools

import jax
import jax.numpy as jnp
from jax import lax
from jax.experimental import pallas as pl
from jax.experimental.pallas import tpu as pltpu

F32 = jnp.float32
BF16 = jnp.bfloat16

D_MODEL = 1024
HG_HEADS = 4
HG_DIM = 128
HG_W = HG_HEADS * HG_DIM
HG_CHUNK = 64
HG_LOCAL_CHUNKS = 4
SB_HEADS = 8
SB_HEAD_DIM = 64
SB_W = SB_HEADS * SB_HEAD_DIM
SB_BLOCK = 128
XA_HEADS = 4
XA_HEAD_DIM = D_MODEL // XA_HEADS
N_GROUPS = 4
EXPERTS_PER_GROUP = 4
N_EXPERTS = N_GROUPS * EXPERTS_PER_GROUP
EXPERT_FF = 512
N_IN = HG_W * 4 + SB_W * 3 + D_MODEL * 2
COL_HG = 0
COL_SB = 4 * HG_W
COL_GATE = COL_SB + 3 * SB_W
LN_EPS = 1e-5
RMS_EPS = 1e-6
DEPTH = 1
DN_ALPHA = (2 * DEPTH) ** 0.25

LANES = 128
VMEM_LIMIT = 56 * 1024 * 1024


def _cparams(sem):
    return pltpu.CompilerParams(dimension_semantics=sem, vmem_limit_bytes=VMEM_LIMIT)


def _layer_norm(x, g, b):
    mu = jnp.mean(x, axis=-1, keepdims=True)
    xc = x - mu
    var = jnp.mean(xc * xc, axis=-1, keepdims=True)
    return xc * lax.rsqrt(var + LN_EPS) * g + b


def _dot(a, b):
    return jnp.dot(a, b, preferred_element_type=F32)


def _dot_nt(a, b):
    return lax.dot_general(a, b, (((1,), (1,)), ((), ())), preferred_element_type=F32)


def _dot_tn(a, b):
    return lax.dot_general(a, b, (((0,), (0,)), ((), ())), preferred_element_type=F32)


def _split3(a):
    hi = a.astype(BF16)
    r1 = a - hi.astype(F32)
    mid = r1.astype(BF16)
    lo = (r1 - mid.astype(F32)).astype(BF16)
    return hi, mid, lo


def _const_spec(shape):
    nd = len(shape)
    return pl.BlockSpec(shape, lambda *_: (0,) * nd)


def _in_proj_kernel(x_ref, g_ref, b_ref, w_ref, o_ref, h0_ref, *, col_chunk):
    h0 = _layer_norm(x_ref[...], g_ref[...], b_ref[...])
    h0_ref[...] = h0
    h = h0.astype(BF16)
    for c in range(N_IN // col_chunk):
        sl = slice(c * col_chunk, (c + 1) * col_chunk)
        o_ref[:, sl] = _dot(h, w_ref[:, sl]).astype(BF16)


def _in_proj(x2, g, b, w_bf, tm=512, col_chunk=512):
    T = x2.shape[0]
    return pl.pallas_call(
        functools.partial(_in_proj_kernel, col_chunk=col_chunk),
        out_shape=(jax.ShapeDtypeStruct((T, N_IN), BF16), jax.ShapeDtypeStruct((T, D_MODEL), F32)),
        grid=(T // tm,),
        in_specs=[
            pl.BlockSpec((tm, D_MODEL), lambda i: (i, 0)),
            _const_spec((1, D_MODEL)),
            _const_spec((1, D_MODEL)),
            _const_spec((D_MODEL, N_IN)),
        ],
        out_specs=(pl.BlockSpec((tm, N_IN), lambda i: (i, 0)), pl.BlockSpec((tm, D_MODEL), lambda i: (i, 0))),
        compiler_params=_cparams(("parallel",)),
        name="in_proj",
    )(x2, g, b, w_bf)


def _hgrn2_kernel(lbl_ref, ng_ref, q_ref, f_ref, i_ref, g_ref, o_ref,
                  state_ref, intra_ref, qs_ref, dl_ref, u_ref, *, layer, tb):
    C = HG_CHUNK
    H2 = C // 2

    @pl.when(pl.program_id(1) == 0)
    def _():
        state_ref[...] = jnp.zeros_like(state_ref)

    lbl = lbl_ref[...]
    ex = jnp.exp(lbl - jnp.max(lbl, axis=0, keepdims=True))
    lb = jnp.sum(ex[: layer + 1], axis=0, keepdims=True) / jnp.sum(ex, axis=0, keepdims=True)

    R = HG_LOCAL_CHUNKS * C
    row = lax.broadcasted_iota(jnp.int32, (R, R), 0)
    col = lax.broadcasted_iota(jnp.int32, (R, R), 1)
    same_chunk = (row // C) == (col // C)
    tri = (same_chunk & (col <= row)).astype(BF16)
    m_diag = ((row // H2) == (col // H2)) & (col <= row)
    m_cross = same_chunk & ((row % C) >= H2) & ((col % C) < H2)
    rrow = lax.broadcasted_iota(jnp.int32, (R, HG_W), 0)
    ng = ng_ref[...]
    heads = [slice(h * HG_DIM, (h + 1) * HG_DIM) for h in range(HG_HEADS)]

    def row_of_period(x, offset, period):
        out = jnp.broadcast_to(x[offset:offset + 1, :], x.shape)
        for k in range(1, R // period):
            out = jnp.where(rrow >= k * period, x[k * period + offset:k * period + offset + 1, :], out)
        return out

    def local(blk):
        rows = slice(blk * R, (blk + 1) * R)
        forget = lb + (1.0 - lb) * jax.nn.sigmoid(f_ref[rows, :].astype(F32))
        kk = 1.0 - forget
        hi, mid, lo = _split3(jnp.log(forget))
        b = _dot(tri, hi) + _dot(tri, mid) + _dot(tri, lo)
        q = q_ref[rows, :].astype(F32)
        d1 = b - row_of_period(b, H2 // 2 - 1, H2)
        q1 = (q * jnp.exp(d1)).astype(BF16)
        k1 = (kk * jnp.exp(-d1)).astype(BF16)
        e3 = jnp.exp(-jnp.abs(b - row_of_period(b, H2 - 1, C)))
        q3 = (q * e3).astype(BF16)
        k3 = (kk * e3).astype(BF16)
        kd = (kk * jnp.exp(row_of_period(b, C - 1, C) - b)).astype(BF16)
        qs_ref[rows, :] = (q * jnp.exp(b)).astype(BF16)
        for k in range(HG_LOCAL_CHUNKS):
            dl_ref[blk * HG_LOCAL_CHUNKS + k] = jnp.exp(b[k * C + C - 1:(k + 1) * C, :])
        s1 = [_dot_nt(q1[:, ls], k1[:, ls]) for ls in heads]
        s3 = [_dot_nt(q3[:, ls], k3[:, ls]) for ls in heads]
        s = [jnp.where(m_diag, a1, jnp.where(m_cross, a3, 0.0)).astype(BF16) for a1, a3 in zip(s1, s3)]
        for h, ls in enumerate(heads):
            v = i_ref[rows, ls]
            intra_ref[rows, ls] = _dot(s[h], v)
            for k in range(HG_LOCAL_CHUNKS):
                cs = slice(k * C, (k + 1) * C)
                u_ref[blk * HG_LOCAL_CHUNKS + k, h] = _dot_tn(v[cs], kd[cs, ls])

    def carry_state(c, carry):
        rows = pl.ds(pl.multiple_of(c * C, C), C)
        dl = dl_ref[c]
        for h, ls in enumerate(heads):
            st = state_ref[h]
            o = intra_ref[rows, ls] + _dot_nt(qs_ref[rows, ls], st.astype(BF16))
            state_ref[h] = st * dl[:, ls] + u_ref[c, h]
            rms = lax.rsqrt(jnp.mean(o * o, axis=-1, keepdims=True) + RMS_EPS)
            gate = g_ref[rows, ls].astype(F32)
            o_ref[rows, ls] = (o * rms * ng * (gate * jax.nn.sigmoid(gate))).astype(o_ref.dtype)
        return carry

    for blk in range(tb // R):
        local(blk)
    lax.fori_loop(0, tb // C, carry_state, 0, unroll=True)


def _hgrn2(proj, lb_logits, norm_g, B, S, layer, tb=512):
    T = B * S
    nsb = S // tb

    def spec(k):
        return pl.BlockSpec((tb, HG_W), lambda b, s, k=k: (b * nsb + s, COL_HG // HG_W + k))

    return pl.pallas_call(
        functools.partial(_hgrn2_kernel, layer=layer, tb=tb),
        out_shape=jax.ShapeDtypeStruct((T, HG_W), BF16),
        grid=(B, nsb),
        in_specs=[
            _const_spec(lb_logits.shape),
            _const_spec((1, HG_DIM)),
            spec(0), spec(1), spec(2), spec(3),
        ],
        out_specs=pl.BlockSpec((tb, HG_W), lambda b, s: (b * nsb + s, 0)),
        scratch_shapes=[pltpu.VMEM((HG_HEADS, HG_DIM, HG_DIM), F32),
                        pltpu.VMEM((tb, HG_W), F32),
                        pltpu.VMEM((tb, HG_W), BF16),
                        pltpu.VMEM((tb // HG_CHUNK, 1, HG_W), F32),
                        pltpu.VMEM((tb // HG_CHUNK, HG_HEADS, HG_DIM, HG_DIM), F32)],
        compiler_params=_cparams(("parallel", "arbitrary")),
        name="hgrn2",
    )(lb_logits, norm_g, proj, proj, proj, proj)


LOG2E = 1.4426950408889634
SB_DEAD = -160.0


def _sb_kernel(q_ref, k_ref, v_ref, o_ref, acc_ref, run_ref, *, npairs, tq):
    i = pl.program_id(1)
    BLK = SB_BLOCK
    W2 = 2 * BLK
    ndiag = tq // BLK
    scale = SB_HEAD_DIM ** -0.5
    lane = lax.broadcasted_iota(jnp.int32, (BLK, LANES), 1)
    head0 = lane < SB_HEAD_DIM
    zrow = lax.broadcasted_iota(jnp.int32, (tq, W2), 0)
    zcol = lax.broadcasted_iota(jnp.int32, (tq, W2), 1) & (BLK - 1)
    wr = lax.broadcasted_iota(jnp.int32, (W2, W2), 0) & (BLK - 1)
    wc = lax.broadcasted_iota(jnp.int32, (W2, W2), 1)
    cw = ((wc >= BLK) | (wr > wc)).astype(BF16)
    qs = q_ref[...] * jnp.asarray(scale, q_ref.dtype)

    def masked_kv(ref, j, p):
        x = ref[pl.ds(pl.multiple_of(j * BLK, BLK), BLK), p * LANES:(p + 1) * LANES]
        zero = jnp.zeros_like(x)
        return jnp.concatenate([jnp.where(head0, x, zero), jnp.where(head0, zero, x)], axis=0)

    def step(js, ds):
        nb = len(js)
        rsl = [slice((ds[b] or 0) * BLK, tq) for b in range(nb)]
        zs = [[_dot_nt(qs[rsl[b], p * LANES:(p + 1) * LANES], masked_kv(k_ref, js[b], p)) * LOG2E
               for p in range(npairs)] for b in range(nb)]
        ws = []
        for b in range(nb):
            nr = tq - rsl[b].start
            before = None if ds[b] is None else ((zcol + ds[b] * BLK) < zrow)[rsl[b]]
            lbs, hls = [], []
            for p in range(npairs):
                z = zs[b][p]
                l2 = jnp.log2(1.0 + jnp.exp2(-jnp.abs(z)))
                lb = jnp.minimum(z, 0.0) - l2
                lk = lb - z
                if before is not None:
                    lk = jnp.where(before, lk, 0.0)
                hi = lk.astype(BF16)
                lo = (lk - hi.astype(F32)).astype(BF16)
                lbs.append(lb)
                hls.append(jnp.concatenate([hi[:, :BLK], lo[:, :BLK]], axis=1))
                hls.append(jnp.concatenate([hi[:, BLK:], lo[:, BLK:]], axis=1))
            cs = _dot(jnp.concatenate(hls, axis=0), cw)
            wb = []
            for p in range(npairs):
                c0 = cs[(2 * p) * nr:(2 * p + 1) * nr]
                c1 = cs[(2 * p + 1) * nr:(2 * p + 2) * nr]
                cum = jnp.concatenate([c0[:, :BLK], c1[:, :BLK]], axis=1)
                rsum = jnp.concatenate([c0[:, BLK:], c1[:, BLK:]], axis=1)
                run = run_ref[p, rsl[b], :]
                w = jnp.exp2(lbs[p] + (run + cum))
                if before is not None:
                    w = jnp.where(before, w, 0.0)
                run_ref[p, rsl[b], :] = run + rsum
                wb.append(w.astype(BF16))
            ws.append(wb)
        for p in range(npairs):
            for b in range(nb):
                acc_ref[p, rsl[b], :] += _dot(ws[b][p], masked_kv(v_ref, js[b], p))

    acc_ref[...] = jnp.zeros_like(acc_ref)
    run_ref[...] = jnp.zeros_like(run_ref)
    step([i * ndiag + d for d in reversed(range(ndiag))], list(reversed(range(ndiag))))

    def live():
        return jnp.max(run_ref[...]) > SB_DEAD

    def cond(c):
        return jnp.logical_and(c[0] < i, c[1])

    def body(c):
        base = (i - 1 - c[0]) * ndiag
        step([base + d for d in reversed(range(ndiag))], [None] * ndiag)
        return c[0] + 1, live()

    lax.while_loop(cond, body, (jnp.int32(0), live()))
    for p in range(npairs):
        o_ref[:, p * LANES:(p + 1) * LANES] = acc_ref[p].astype(o_ref.dtype)


def _sb_attn(proj, B, S, npairs=SB_W // LANES, tq=256):
    T = B * S
    nq = S // tq
    wblk = npairs * LANES
    ng = SB_W // wblk
    cq = COL_SB // wblk
    ck = (COL_SB + SB_W) // wblk
    cv = (COL_SB + 2 * SB_W) // wblk
    return pl.pallas_call(
        functools.partial(_sb_kernel, npairs=npairs, tq=tq),
        out_shape=jax.ShapeDtypeStruct((T, SB_W), BF16),
        grid=(B * ng, nq),
        in_specs=[
            pl.BlockSpec((tq, wblk), lambda g, i: ((g // ng) * nq + i, cq + g % ng)),
            pl.BlockSpec((S, wblk), lambda g, i: (g // ng, ck + g % ng)),
            pl.BlockSpec((S, wblk), lambda g, i: (g // ng, cv + g % ng)),
        ],
        out_specs=pl.BlockSpec((tq, wblk), lambda g, i: ((g // ng) * nq + i, g % ng)),
        scratch_shapes=[pltpu.VMEM((npairs, tq, LANES), F32),
                        pltpu.VMEM((npairs, tq, 2 * SB_BLOCK), F32)],
        compiler_params=_cparams(("parallel", "arbitrary")),
        name="sb_attn",
    )(proj, proj, proj)


GATE_BLK = 512


def _mix_kernel(h0_ref, oa_ref, ob_ref, ga0_ref, ga1_ref, gb0_ref, gb1_ref,
                wa_ref, wb_ref, wm_ref, g1_ref, b1_ref, o_ref):
    rs = h0_ref.shape[0] // MIX_SUBTILES
    rows = [slice(k * rs, (k + 1) * rs) for k in range(MIX_SUBTILES)]
    halves = [(slice(0, GATE_BLK), ga0_ref, gb0_ref), (slice(GATE_BLK, 2 * GATE_BLK), ga1_ref, gb1_ref)]
    yab = [[(_dot(oa_ref[r, :], wa_ref[:, cs]), _dot(ob_ref[r, :], wb_ref[:, cs])) for cs, _, _ in halves]
           for r in rows]
    ys = []
    for k, r in enumerate(rows):
        y = None
        for (cs, ga_ref, gb_ref), (ya, yb) in zip(halves, yab[k]):
            merged = (jax.nn.sigmoid(ga_ref[r, :].astype(F32)) * ya
                      + jax.nn.sigmoid(gb_ref[r, :].astype(F32)) * yb)
            part = _dot(merged.astype(BF16), wm_ref[cs, :])
            y = part if y is None else y + part
        ys.append(y)
    for k, r in enumerate(rows):
        o_ref[r, :] = _layer_norm(DN_ALPHA * h0_ref[r, :] + ys[k], g1_ref[...], b1_ref[...])


MIX_SUBTILES = 4


def _mix(h0, oa, ob, proj, wa, wb, wm, g1, b1, tm=1024):
    T = h0.shape[0]
    c0 = COL_GATE // GATE_BLK

    def gate_spec(k):
        return pl.BlockSpec((tm, GATE_BLK), lambda i, k=k: (i, c0 + k))

    return pl.pallas_call(
        _mix_kernel,
        out_shape=jax.ShapeDtypeStruct((T, D_MODEL), F32),
        grid=(T // tm,),
        in_specs=[
            pl.BlockSpec((tm, D_MODEL), lambda i: (i, 0)),
            pl.BlockSpec((tm, HG_W), lambda i: (i, 0)),
            pl.BlockSpec((tm, SB_W), lambda i: (i, 0)),
            gate_spec(0), gate_spec(1), gate_spec(2), gate_spec(3),
            _const_spec((HG_W, D_MODEL)), _const_spec((SB_W, D_MODEL)), _const_spec((D_MODEL, D_MODEL)),
            _const_spec((1, D_MODEL)), _const_spec((1, D_MODEL)),
        ],
        out_specs=pl.BlockSpec((tm, D_MODEL), lambda i: (i, 0)),
        compiler_params=_cparams(("parallel",)),
        name="mix",
    )(h0, oa, ob, proj, proj, proj, proj, wa, wb, wm, g1, b1)


XA_SUBTILES = 4


def _xattn_kernel(h_ref, m_ref, wk_ref, wv_ref, wq_ref, wo_ref, g_ref, b_ref, o_ref, k_ref, v_ref):
    @pl.when(pl.program_id(1) == 0)
    def _():
        m = m_ref[0].astype(BF16)
        k_ref[0] = _dot(m, wk_ref[...]).astype(BF16)
        v_ref[0] = _dot(m, wv_ref[...]).astype(BF16)

    nsub = XA_SUBTILES
    rs = h_ref.shape[0] // nsub
    rows = [slice(k * rs, (k + 1) * rs) for k in range(nsub)]
    heads = [slice(hd * XA_HEAD_DIM, (hd + 1) * XA_HEAD_DIM) for hd in range(XA_HEADS)]
    scale = XA_HEAD_DIM ** -0.5
    hs = [h_ref[r, :] for r in rows]
    qs = [_dot(h.astype(BF16), wq_ref[...]).astype(BF16) for h in hs]
    ss = [[_dot_nt(q[:, ls], k_ref[0, :, ls]) * scale for ls in heads] for q in qs]
    os_ = []
    for k in range(nsub):
        outs = []
        for hd, ls in enumerate(heads):
            s = ss[k][hd]
            p = jnp.exp(s - jnp.max(s, axis=-1, keepdims=True))
            p = p / jnp.sum(p, axis=-1, keepdims=True)
            outs.append(_dot(p.astype(BF16), v_ref[0, :, ls]).astype(BF16))
        os_.append(jnp.concatenate(outs, axis=-1))
    ys = [_dot(o, wo_ref[...]) for o in os_]
    for k in range(nsub):
        o_ref[rows[k], :] = _layer_norm(DN_ALPHA * hs[k] + ys[k], g_ref[...], b_ref[...])


def _xattn(h1, mem, wk, wv, wq, wo, g2, b2, B, S, tm=1024):
    T = B * S
    M = mem.shape[1]
    W = wk.shape[1]
    nsb = S // tm
    return pl.pallas_call(
        _xattn_kernel,
        out_shape=jax.ShapeDtypeStruct((T, D_MODEL), F32),
        grid=(B, nsb),
        in_specs=[
            pl.BlockSpec((tm, D_MODEL), lambda b, s: (b * nsb + s, 0)),
            pl.BlockSpec((1, M, D_MODEL), lambda b, s: (b, 0, 0)),
            _const_spec(wk.shape), _const_spec(wv.shape),
            _const_spec(wq.shape), _const_spec(wo.shape),
            _const_spec((1, D_MODEL)), _const_spec((1, D_MODEL)),
        ],
        out_specs=pl.BlockSpec((tm, D_MODEL), lambda b, s: (b * nsb + s, 0)),
        scratch_shapes=[pltpu.VMEM((1, M, W), BF16), pltpu.VMEM((1, M, W), BF16)],
        compiler_params=_cparams(("parallel", "arbitrary")),
        name="xattn",
    )(h1, mem, wk, wv, wq, wo, g2, b2)


MOE_TM = 1024
MOE_M1 = 160
MOE_M2 = 64
MOE_EPS = 4
SEG_ALIGN = 16
MOE_QCH = 512
MOE_OUT_SUBTILES = 4


def _moe_slots(tm):
    return -(-(2 * tm + N_EXPERTS * SEG_ALIGN + MOE_M1 + MOE_M2) // MOE_QCH) * MOE_QCH


def _seg_pad(cnt):
    return jnp.ceil(cnt * (1.0 / SEG_ALIGN)) * SEG_ALIGN
SUBLANES = 8
GROUP_ROWS = 16


def _hdot(a, b):
    return jnp.dot(a, b, preferred_element_type=F32, precision=lax.Precision.HIGHEST)


def _split2(a):
    hi = a.astype(BF16)
    return hi, (a - hi.astype(F32)).astype(BF16)


def _first_argmax0(vals, row, height):
    m = jnp.max(vals, axis=0, keepdims=True)
    idx = jnp.min(jnp.where(vals == m, row, height), axis=0, keepdims=True)
    return m, idx


def _select_rows(shape, rows):
    r = lax.broadcasted_iota(jnp.int32, shape, 0)
    out = jnp.zeros(shape, F32)
    for k, v in enumerate(rows):
        out = jnp.where(r == k, v, out)
    return out


def _route_kernel(h_ref, wt_ref, bg_ref, be_ref, rinfo_ref, cinfo_ref, meta_ref):
    tm = h_ref.shape[0]
    h = h_ref[...]
    neg = jnp.float32(-jnp.inf)
    rg = lax.broadcasted_iota(jnp.int32, (GROUP_ROWS, tm), 0)
    re = lax.broadcasted_iota(jnp.int32, (N_EXPERTS, tm), 0)
    nw = GROUP_ROWS + N_EXPERTS
    hh, hl = _split2(h)
    wh, wl = _split2(wt_ref[...])
    lg = _dot_nt(jnp.concatenate([wh, wl], axis=0), hh)
    lg = lg[:nw] + (lg[nw:] + _dot_nt(wh, hl))
    gl = jnp.where(rg < N_GROUPS, lg[:GROUP_ROWS] + bg_ref[...], neg)
    el = lg[GROUP_ROWS:] + be_ref[...]
    gm, g_idx = _first_argmax0(gl, rg, GROUP_ROWS)
    g_top = 1.0 / jnp.sum(jnp.exp(gl - gm), axis=0, keepdims=True)
    ml = jnp.where(re // EXPERTS_PER_GROUP == g_idx, el, neg)
    m1, i1 = _first_argmax0(ml, re, N_EXPERTS)
    ml2 = jnp.where(re == i1, neg, ml)
    m2, i2 = _first_argmax0(ml2, re, N_EXPERTS)
    e2 = jnp.exp(m2 - m1)
    gate0 = g_top / (1.0 + e2)
    gate1 = g_top * e2 / (1.0 + e2)

    oh0 = jnp.where(re == i1, 1.0, 0.0)
    oh1 = jnp.where(re == i2, 1.0, 0.0)
    cnt0 = jnp.sum(oh0, axis=1, keepdims=True)
    cnt1 = jnp.sum(oh1, axis=1, keepdims=True)
    er = lax.broadcasted_iota(jnp.int32, (N_EXPERTS, N_EXPERTS), 0)
    ec = lax.broadcasted_iota(jnp.int32, (N_EXPERTS, N_EXPERTS), 1)
    seg = _hdot(jnp.where(ec < er, 1.0, 0.0),
                jnp.broadcast_to(_seg_pad(cnt0 + cnt1), (N_EXPERTS, LANES)))[:, 0:1]
    tr = lax.broadcasted_iota(jnp.int32, (tm, tm), 0)
    tc = lax.broadcasted_iota(jnp.int32, (tm, tm), 1)
    earlier = jnp.where(tr < tc, 1.0, 0.0).astype(BF16)
    cum = _dot(jnp.concatenate([oh0, oh1], axis=0).astype(BF16), earlier)
    slot0 = jnp.sum(oh0 * (seg + cum[:N_EXPERTS]), axis=0, keepdims=True)
    slot1 = jnp.sum(oh1 * (seg + cnt0 + cum[N_EXPERTS:]), axis=0, keepdims=True)

    info = (slot0, slot1, gate0, gate1)
    rinfo_ref[0] = _select_rows((SUBLANES, tm), info)
    cinfo_ref[...] = _select_rows((LANES, tm), info).T

    ohs = jnp.concatenate([oh0 + oh1, jnp.zeros((LANES - N_EXPERTS, tm), F32)], axis=0).astype(BF16)
    cnt_row = _dot_nt(jnp.ones((SUBLANES, tm), BF16), ohs)
    lr = lax.broadcasted_iota(jnp.int32, (LANES, LANES), 0)
    lc = lax.broadcasted_iota(jnp.int32, (LANES, LANES), 1)
    seg_row = _hdot(_seg_pad(cnt_row), jnp.where(lr < lc, 1.0, 0.0))
    r8 = lax.broadcasted_iota(jnp.int32, (SUBLANES, LANES), 0)
    meta_ref[0] = jnp.where(r8 == 0, seg_row, jnp.where(r8 == 1, cnt_row, 0.0)).astype(jnp.int32)


def _route(h2, wt, bg, be, tm=MOE_TM):
    T = h2.shape[0]
    nt = T // tm
    return pl.pallas_call(
        _route_kernel,
        out_shape=(jax.ShapeDtypeStruct((nt, SUBLANES, tm), F32),
                   jax.ShapeDtypeStruct((T, LANES), F32),
                   jax.ShapeDtypeStruct((nt, SUBLANES, LANES), jnp.int32)),
        grid=(nt,),
        in_specs=[pl.BlockSpec((tm, D_MODEL), lambda i: (i, 0)),
                  _const_spec(wt.shape), _const_spec(bg.shape), _const_spec(be.shape)],
        out_specs=(pl.BlockSpec((1, SUBLANES, tm), lambda i: (i, 0, 0)),
                   pl.BlockSpec((tm, LANES), lambda i: (i, 0)),
                   pl.BlockSpec((1, SUBLANES, LANES), lambda i: (i, 0, 0))),
        compiler_params=_cparams(("parallel",)),
        name="moe_route",
    )(h2, wt, bg, be)


def _moe_kernel(meta_ref, h_ref, rinfo_ref, cinfo_ref, w1_ref, w3_ref, w2_ref, g_ref, b_ref,
                o_ref, hb_ref, ys_ref):
    t = pl.program_id(0)
    eg = pl.program_id(1)
    tm = h_ref.shape[0]

    @pl.when(eg == 0)
    def _():
        hb_ref[...] = h_ref[...].astype(BF16)
        ys_ref[...] = jnp.zeros_like(ys_ref)

    def chunks(js, r0s, m):
        r0s = [pl.multiple_of(r0, SEG_ALIGN) for r0 in r0s]
        info = rinfo_ref[0]
        slot0 = info[0:1].astype(jnp.int32)
        slot1 = info[1:2].astype(jnp.int32)
        xs = []
        for r0 in r0s:
            s = lax.broadcasted_iota(jnp.int32, (m, tm), 0) + r0
            pick = jnp.where((s == slot0) | (s == slot1), 1.0, 0.0).astype(BF16)
            xs.append(_dot(pick, hb_ref[...]).astype(BF16))
        ab = [(_dot(x, w1_ref[j]), _dot(x, w3_ref[j])) for j, x in zip(js, xs)]
        hids = [(a * jax.nn.sigmoid(a) * b).astype(BF16) for a, b in ab]
        ys = [_dot(hid, w2_ref[j]).astype(BF16) for j, hid in zip(js, hids)]
        for r0, y in zip(r0s, ys):
            ys_ref[pl.ds(r0, m), :] = y

    base = eg * MOE_EPS
    starts = [meta_ref[(2 * t) * N_EXPERTS + base + j] for j in range(MOE_EPS)]
    cnts = [meta_ref[(2 * t + 1) * N_EXPERTS + base + j] for j in range(MOE_EPS)]

    for j in range(MOE_EPS):
        @pl.when(cnts[j] > MOE_M1)
        def _(j=j):
            def extra(k, c):
                chunks([j], [starts[j] + MOE_M1 + k * MOE_M2], MOE_M2)
                return c

            lax.fori_loop(0, (cnts[j] - MOE_M1 + MOE_M2 - 1) // MOE_M2, extra, 0)

    chunks(list(range(MOE_EPS)), starts, MOE_M1)

    @pl.when(eg == N_EXPERTS // MOE_EPS - 1)
    def _():
        rs = tm // MOE_OUT_SUBTILES
        for k in range(MOE_OUT_SUBTILES):
            r = slice(k * rs, (k + 1) * rs)
            ci = cinfo_ref[r, :]
            slot0 = ci[:, 0:1].astype(jnp.int32)
            slot1 = ci[:, 1:2].astype(jnp.int32)
            y = None
            for c in range(ys_ref.shape[0] // MOE_QCH):
                sc = lax.broadcasted_iota(jnp.int32, (rs, MOE_QCH), 1) + c * MOE_QCH
                q = (jnp.where(sc == slot0, ci[:, 2:3], 0.0)
                     + jnp.where(sc == slot1, ci[:, 3:4], 0.0)).astype(BF16)
                part = _dot(q, ys_ref[c * MOE_QCH:(c + 1) * MOE_QCH, :])
                y = part if y is None else y + part
            o_ref[r, :] = _layer_norm(DN_ALPHA * h_ref[r, :] + y, g_ref[...], b_ref[...])


def _moe(h2, rinfo, cinfo, meta, w1, w3, w2, g3, b3, tm=MOE_TM):
    T = h2.shape[0]
    return pl.pallas_call(
        _moe_kernel,
        out_shape=jax.ShapeDtypeStruct((T, D_MODEL), F32),
        grid_spec=pltpu.PrefetchScalarGridSpec(
            num_scalar_prefetch=1,
            grid=(T // tm, N_EXPERTS // MOE_EPS),
            in_specs=[
                pl.BlockSpec((tm, D_MODEL), lambda i, e, m: (i, 0)),
                pl.BlockSpec((1, SUBLANES, tm), lambda i, e, m: (i, 0, 0)),
                pl.BlockSpec((tm, LANES), lambda i, e, m: (i, 0)),
                pl.BlockSpec((MOE_EPS, D_MODEL, EXPERT_FF), lambda i, e, m: (e, 0, 0)),
                pl.BlockSpec((MOE_EPS, D_MODEL, EXPERT_FF), lambda i, e, m: (e, 0, 0)),
                pl.BlockSpec((MOE_EPS, EXPERT_FF, D_MODEL), lambda i, e, m: (e, 0, 0)),
                pl.BlockSpec((1, D_MODEL), lambda i, e, m: (0, 0)),
                pl.BlockSpec((1, D_MODEL), lambda i, e, m: (0, 0)),
            ],
            out_specs=pl.BlockSpec((tm, D_MODEL), lambda i, e, m: (i, 0)),
            scratch_shapes=[pltpu.VMEM((tm, D_MODEL), BF16), pltpu.VMEM((_moe_slots(tm), D_MODEL), BF16)],
        ),
        compiler_params=_cparams(("parallel", "arbitrary")),
        name="moe",
    )(meta, h2, rinfo, cinfo, w1, w3, w2, g3, b3)


def _moe_layer(h2, wg, bg, we, be, w1, w3, w2, g3, b3, tm=MOE_TM):
    pad = GROUP_ROWS - N_GROUPS
    wt = jnp.concatenate([jnp.pad(wg.astype(F32).T, ((0, pad), (0, 0))), we.astype(F32).T], axis=0)
    bgc = jnp.pad(bg.astype(F32).reshape(-1, 1), ((0, pad), (0, 0)))
    rinfo, cinfo, meta = _route(h2, wt, bgc, be.astype(F32).reshape(-1, 1), tm=tm)
    meta = meta[:, :2, :N_EXPERTS].reshape(-1)
    return _moe(h2, rinfo, cinfo, meta, w1, w3, w2, g3, b3, tm=tm)


def kernel(x, mem, ln_in_g, ln_in_b, w_in, hg_lb_logits, hg_norm_g, w_branch_a, w_branch_b, w_mix_out,
           ln1_g, ln1_b, xa_wq, xa_wk, xa_wv, xa_wo, ln2_g, ln2_b, router_wg, router_bg, router_we,
           router_be, moe_w1, moe_w3, moe_w2, ln3_g, ln3_b):
    B, S, D = x.shape
    T = B * S
    row = lambda a: a.reshape(1, -1).astype(F32)
    x2 = x.reshape(T, D)
    gin, bin_ = row(ln_in_g), row(ln_in_b)
    assert w_in.shape[0] == DEPTH
    l = 0
    proj, h0 = _in_proj(x2, gin, bin_, w_in[l].astype(BF16))
    oa = _hgrn2(proj, hg_lb_logits.astype(F32), row(hg_norm_g[l]), B, S, l)
    ob = _sb_attn(proj, B, S)
    h1 = _mix(h0, oa, ob, proj, w_branch_a[l].astype(BF16), w_branch_b[l].astype(BF16),
              w_mix_out[l].astype(BF16), row(ln1_g[l]), row(ln1_b[l]))
    h2 = _xattn(h1, mem, xa_wk[l].astype(BF16), xa_wv[l].astype(BF16), xa_wq[l].astype(BF16),
                xa_wo[l].astype(BF16), row(ln2_g[l]), row(ln2_b[l]), B, S)
    out = _moe_layer(h2, router_wg[l], router_bg[l], router_we[l], router_be[l],
                     moe_w1[l].astype(BF16), moe_w3[l].astype(BF16), moe_w2[l].astype(BF16),
                     row(ln3_g[l]), row(ln3_b[l]))
    return out.reshape(B, S, D)
```
